```python
import jax, jax.numpy as jnp
from jax import lax
import numpy as np

D_MODEL = 1024
BATCH = 4
SEQ = 4096
DEPTH = 1

LRU_WIDTH = D_MODEL // 2
LRU_HEADS = 8
LRU_HEAD_DIM = LRU_WIDTH // LRU_HEADS
CONV_WIDTH = 4
LRU_C = 8.0
POOL_WIDTH = D_MODEL // 2
POOL_WINDOWS = (2, 4, 8, 16)
POOL_GROUPS = len(POOL_WINDOWS)
POOL_GROUP_DIM = POOL_WIDTH // POOL_GROUPS
MIX_WIDTH = LRU_WIDTH + POOL_WIDTH
IN_WIDTH = 2 * LRU_WIDTH + POOL_WIDTH
N_EXPERTS = 32
TOP_K = 4
D_EXPERT = D_MODEL
SWIGLU_LIMIT = 7.0
SWIGLU_ALPHA = 1.702
EXPERT_BLOCK = 256
NORM_EPS = 1e-6

kernel_name = "hybrid_rglru_pool_moe_adaln"


def rms_norm(x, g):
    x32 = x.astype(jnp.float32)
    y = x32 * lax.rsqrt(jnp.mean(x32 * x32, axis=-1, keepdims=True) + NORM_EPS)
    return (y * g.astype(jnp.float32)).astype(x.dtype)


def modulate(h, shift, scale):
    return h * (1.0 + scale[:, None, :]) + shift[:, None, :]


def causal_depthwise_conv(x, w, b):
    S = x.shape[1]
    xp = jnp.pad(x, ((0, 0), (CONV_WIDTH - 1, 0), (0, 0)))
    y = b
    for k in range(CONV_WIDTH):
        s0 = CONV_WIDTH - 1 - k
        y = y + xp[:, s0:s0 + S] * w[k]
    return y


def _lin_combine(e1, e2):
    a1, b1 = e1
    a2, b2 = e2
    return a1 * a2, a2 * b1 + b2


def rg_lru(x, gate_a_w, gate_a_b, gate_x_w, gate_x_b, lam):
    B, S, _ = x.shape
    xh = x.reshape(B, S, LRU_HEADS, LRU_HEAD_DIM)
    r = jax.nn.sigmoid(jnp.einsum('bshi,hij->bshj', xh, gate_a_w).reshape(B, S, LRU_WIDTH) + gate_a_b)
    i = jax.nn.sigmoid(jnp.einsum('bshi,hij->bshj', xh, gate_x_w).reshape(B, S, LRU_WIDTH) + gate_x_b)
    log_a = -LRU_C * r.astype(jnp.float32) * jax.nn.softplus(-lam.astype(jnp.float32))
    a = jnp.exp(log_a)
    mult = jnp.sqrt(-jnp.expm1(2.0 * log_a))
    bterm = mult * (i * x).astype(jnp.float32)
    _, h = lax.associative_scan(_lin_combine, (a, bterm), axis=1)
    return h.astype(x.dtype)


def causal_pool_mixer(x, pool_w, pool_b, pool_scale):
    B, S, C = x.shape
    x32 = x.astype(jnp.float32)
    cs = jnp.pad(jnp.cumsum(x32, axis=1), ((0, 0), (1, 0), (0, 0)))
    t = jnp.arange(S)
    outs = []
    for g, w in enumerate(POOL_WINDOWS):
        lo, hi = g * POOL_GROUP_DIM, (g + 1) * POOL_GROUP_DIM
        csg = cs[..., lo:hi]
        lagged = jnp.pad(csg[:, :S + 1 - w], ((0, 0), (w - 1, 0), (0, 0)))
        count = jnp.minimum(t + 1, w).astype(jnp.float32)[None, :, None]
        outs.append((csg[:, 1:] - lagged) / count - x32[..., lo:hi])
    p = jnp.concatenate(outs, axis=-1).reshape(B, S, POOL_GROUPS, POOL_GROUP_DIM)
    y = jnp.einsum('bsgc,gcd->bsgd', p, pool_w.astype(jnp.float32)).reshape(B, S, C) + pool_b
    return (y * pool_scale).astype(x.dtype)


def clamped_swiglu(u):
    glu = jnp.minimum(u[..., ::2], SWIGLU_LIMIT)
    lin = jnp.clip(u[..., 1::2], -SWIGLU_LIMIT, SWIGLU_LIMIT)
    return glu * jax.nn.sigmoid(SWIGLU_ALPHA * glu) * (lin + 1.0)


def moe_ffn(h, router_w, router_b, w1, b1, w2, b2):
    B, S, D = h.shape
    T = B * S
    ht = h.reshape(T, D)
    logits = (ht @ router_w + router_b).astype(jnp.float32)
    top_val, top_idx = lax.top_k(logits, TOP_K)
    gates = jax.nn.softmax(top_val, axis=-1)
    flat_e = top_idx.reshape(-1).astype(jnp.int32)
    flat_tok = jnp.arange(T * TOP_K, dtype=jnp.int32) // TOP_K
    flat_w = gates.reshape(-1)
    order = jnp.argsort(flat_e)
    sorted_e = flat_e[order]
    counts = jnp.bincount(flat_e, length=N_EXPERTS).astype(jnp.int32)
    starts = jnp.cumsum(counts) - counts
    padded = (counts + EXPERT_BLOCK - 1) // EXPERT_BLOCK * EXPERT_BLOCK
    pad_ends = jnp.cumsum(padded)
    pad_starts = pad_ends - padded
    rank = jnp.arange(T * TOP_K, dtype=jnp.int32) - starts[sorted_e]
    dest = pad_starts[sorted_e] + rank
    n_blocks = -(-(T * TOP_K) // EXPERT_BLOCK) + N_EXPERTS
    cap = n_blocks * EXPERT_BLOCK
    tok_buf = jnp.full((cap,), T, dtype=jnp.int32).at[dest].set(flat_tok[order])
    w_buf = jnp.zeros((cap,), jnp.float32).at[dest].set(flat_w[order])
    block_start = jnp.arange(n_blocks, dtype=jnp.int32) * EXPERT_BLOCK
    block_e = jnp.minimum(jnp.searchsorted(pad_ends, block_start, side='right'), N_EXPERTS - 1)
    h_pad = jnp.concatenate([ht, jnp.zeros((1, D), ht.dtype)], axis=0)

    def expert_block(args):
        idx, wt, e = args
        xb = h_pad[idx]
        u = xb @ w1[e] + b1[e]
        y = clamped_swiglu(u) @ w2[e] + b2[e]
        return y.astype(jnp.float32) * wt[:, None]

    y = lax.map(expert_block, (tok_buf.reshape(n_blocks, EXPERT_BLOCK),
                               w_buf.reshape(n_blocks, EXPERT_BLOCK), block_e))
    out = jnp.zeros((T + 1, D), jnp.float32).at[tok_buf].add(y.reshape(cap, D))[:T]
    return out.reshape(B, S, D).astype(h.dtype)


def setup_inputs(seed: int = 0) -> dict:
    key = jax.random.key(seed)
    ks = jax.random.split(key, 28)
    f32 = jnp.float32
    nrm = lambda k, shape, s: jax.random.normal(k, shape, f32) * s
    gain = lambda k: 1.0 + 0.05 * jax.random.normal(k, (DEPTH, D_MODEL), f32)
    u = jax.random.uniform(ks[13], (DEPTH, LRU_WIDTH), f32, minval=0.9, maxval=0.999)
    p = u ** (1.0 / LRU_C)
    lru_lambda = jnp.log(p) - jnp.log1p(-p)
    return {
        "x": nrm(ks[0], (BATCH, SEQ, D_MODEL), 1.0),
        "c": nrm(ks[1], (BATCH, D_MODEL), 1.0),
        "w_ada": nrm(ks[2], (DEPTH, D_MODEL, 6 * D_MODEL), D_MODEL ** -0.5),
        "b_ada": nrm(ks[3], (DEPTH, 6 * D_MODEL), 0.02),
        "norm_mix_pre": gain(ks[4]),
        "norm_mix_post": gain(ks[5]),
        "w_in": nrm(ks[6], (DEPTH, D_MODEL, IN_WIDTH), D_MODEL ** -0.5),
        "conv_w": nrm(ks[7], (DEPTH, CONV_WIDTH, LRU_WIDTH), CONV_WIDTH ** -0.5),
        "conv_b": nrm(ks[8], (DEPTH, LRU_WIDTH), 0.02),
        "gate_a_w": nrm(ks[9], (DEPTH, LRU_HEADS, LRU_HEAD_DIM, LRU_HEAD_DIM), LRU_HEAD_DIM ** -0.5),
        "gate_a_b": nrm(ks[10], (DEPTH, LRU_WIDTH), 0.02),
        "gate_x_w": nrm(ks[11], (DEPTH, LRU_HEADS, LRU_HEAD_DIM, LRU_HEAD_DIM), LRU_HEAD_DIM ** -0.5),
        "gate_x_b": nrm(ks[12], (DEPTH, LRU_WIDTH), 0.02),
        "lru_lambda": lru_lambda,
        "pool_w": nrm(ks[14], (DEPTH, POOL_GROUPS, POOL_GROUP_DIM, POOL_GROUP_DIM), POOL_GROUP_DIM ** -0.5),
        "pool_b": nrm(ks[15], (DEPTH, POOL_WIDTH), 0.02),
        "pool_scale": 1.0 + 0.05 * jax.random.normal(ks[16], (DEPTH, POOL_WIDTH), f32),
        "w_out": nrm(ks[17], (DEPTH, MIX_WIDTH, D_MODEL), MIX_WIDTH ** -0.5),
        "norm_ffn_pre": gain(ks[18]),
        "norm_ffn_post": gain(ks[19]),
        "router_w": nrm(ks[20], (DEPTH, D_MODEL, N_EXPERTS), D_MODEL ** -0.5),
        "router_b": nrm(ks[21], (DEPTH, N_EXPERTS), 0.01),
        "expert_w1": nrm(ks[22], (DEPTH, N_EXPERTS, D_MODEL, 2 * D_EXPERT), D_MODEL ** -0.5),
        "expert_b1": nrm(ks[23], (DEPTH, N_EXPERTS, 2 * D_EXPERT), 0.02),
        "expert_w2": nrm(ks[24], (DEPTH, N_EXPERTS, D_EXPERT, D_MODEL), D_EXPERT ** -0.5),
        "expert_b2": nrm(ks[25], (DEPTH, N_EXPERTS, D_MODEL), 0.02),
    }


def reference(x, c, w_ada, b_ada, norm_mix_pre, norm_mix_post, w_in, conv_w, conv_b,
              gate_a_w, gate_a_b, gate_x_w, gate_x_b, lru_lambda, pool_w, pool_b, pool_scale,
              w_out, norm_ffn_pre, norm_ffn_post, router_w, router_b,
              expert_w1, expert_b1, expert_w2, expert_b2):
    c_act = jax.nn.silu(c)
    for l in range(DEPTH):
        mod = c_act @ w_ada[l] + b_ada[l]
        shift_m, scale_m, gate_m, shift_f, scale_f, gate_f = jnp.split(mod, 6, axis=-1)

        h = modulate(rms_norm(x, norm_mix_pre[l]), shift_m, scale_m)
        u = h @ w_in[l]
        xa = u[..., :LRU_WIDTH]
        ga = u[..., LRU_WIDTH:2 * LRU_WIDTH]
        xb = u[..., 2 * LRU_WIDTH:]
        xa = causal_depthwise_conv(xa, conv_w[l], conv_b[l])
        ya = rg_lru(xa, gate_a_w[l], gate_a_b[l], gate_x_w[l], gate_x_b[l], lru_lambda[l]) * jax.nn.gelu(ga)
        yb = causal_pool_mixer(xb, pool_w[l], pool_b[l], pool_scale[l])
        mix = jnp.concatenate([ya, yb], axis=-1) @ w_out[l]
        x = x + gate_m[:, None, :] * rms_norm(mix, norm_mix_post[l])

        h = modulate(rms_norm(x, norm_ffn_pre[l]), shift_f, scale_f)
        f = moe_ffn(h, router_w[l], router_b[l], expert_w1[l], expert_b1[l], expert_w2[l], expert_b2[l])
        x = x + gate_f[:, None, :] * rms_norm(f, norm_ffn_post[l])
    return x
```

```python
import functools

import jax
import jax.numpy as jnp
from jax import lax
from jax.experimental import pallas as pl
from jax.experimental.pallas import tpu as pltpu

F32 = jnp.float32
BF16 = jnp.bfloat16
I32 = jnp.int32
U32 = jnp.uint32

D_MODEL = 1024
LRU_WIDTH = 512
LRU_HEADS = 8
POOL_WIDTH = 512
POOL_WINDOWS = (2, 4, 8, 16)
POOL_GROUP_DIM = 128
CONV_WIDTH = 4
IN_WIDTH = 2 * LRU_WIDTH + POOL_WIDTH
LRU_C = 8.0
N_EXPERTS = 32
TOP_K = 4
D_EXPERT = 1024
SWIGLU_LIMIT = 7.0
SWIGLU_ALPHA = 1.702
NORM_EPS = 1e-6

SUBLANES = 8
LANES = 128
TOKEN_TILE = 256
FFN_BLOCK = 256
CONV_CARRY = SUBLANES
POOL_CARRY = 16
SORT_ROWS = 1280
RUN_SIZES = (256, 128, 64, 32, 16, 8)
VMEM_LIMIT = 52 * 1024 * 1024


def _rms(v, g):
    return v * lax.rsqrt(jnp.mean(v * v, axis=-1, keepdims=True) + NORM_EPS) * g


def _dot(a, b):
    return jnp.dot(a, b, preferred_element_type=F32)


def _dot_nt(a, b):
    return lax.dot_general(a, b, (((1,), (1,)), ((), ())), preferred_element_type=F32)


def _split_bf16(v):
    hi = v.astype(BF16)
    lo = (v - hi.astype(F32)).astype(BF16)
    return hi, lo


def _pack_pair(a, b):
    ua = lax.bitcast_convert_type(a, U32)
    ub = lax.bitcast_convert_type(b, U32)
    return (ua >> 16) | (ub & jnp.uint32(0xFFFF0000))


def _unpack_pair(w):
    a = lax.bitcast_convert_type(w << 16, F32).astype(BF16)
    b = lax.bitcast_convert_type(w & jnp.uint32(0xFFFF0000), F32).astype(BF16)
    return a, b


def _ada_kernel(c_ref, w_ref, b_ref, o_ref):
    c = c_ref[...]
    ca = c * jax.nn.sigmoid(c)
    o_ref[...] = jnp.dot(ca, w_ref[...], preferred_element_type=F32,
                         precision=lax.Precision.HIGHEST) + b_ref[...]


def _ada_call(c, w_ada, b_ada):
    batch = c.shape[0]
    n_out = w_ada.shape[1]
    bn = D_MODEL
    return pl.pallas_call(
        _ada_kernel,
        grid=(n_out // bn,),
        in_specs=[
            pl.BlockSpec((batch, D_MODEL), lambda j: (0, 0)),
            pl.BlockSpec((D_MODEL, bn), lambda j: (0, j)),
            pl.BlockSpec((1, bn), lambda j: (0, j)),
        ],
        out_specs=pl.BlockSpec((batch, bn), lambda j: (0, j)),
        out_shape=jax.ShapeDtypeStruct((batch, n_out), F32),
        name="ada",
    )(c, w_ada, b_ada.reshape(1, n_out))


def _lru_scan(a, b, h0, out_ref):
    ts, c = a.shape
    groups = ts // SUBLANES
    a3 = a.reshape(groups, SUBLANES, c)
    b3 = b.reshape(groups, SUBLANES, c)
    row = lax.broadcasted_iota(I32, (groups, SUBLANES, c), 1)
    d = 1
    while d < SUBLANES:
        a_sh = pltpu.roll(a3, d, axis=1)
        b_sh = pltpu.roll(b3, d, axis=1)
        m = row >= d
        b3 = jnp.where(m, a3 * b_sh + b3, b3)
        a3 = jnp.where(m, a3 * a_sh, a3)
        d *= 2
    h = h0
    for g in range(groups):
        hg = a3[g] * h + b3[g]
        out_ref[pl.ds(g * SUBLANES, SUBLANES), :] = hg
        h = hg[SUBLANES - 1:SUBLANES, :]
    return h


def _mixer_kernel(x_ref, mod_ref, nrm_ref, w_in_ref, convw_ref, lruv_ref, wg_ref,
                  poolv_ref, wp_ref, w_out_ref, rwh_ref, rwl_ref, rb_ref,
                  x1_ref, h2_ref, gate_ref, pos_ref, cnt_ref,
                  xa_ext, xb_ext, hcar, hbuf):
    s = pl.program_id(1)
    ts = TOKEN_TILE

    @pl.when(s == 0)
    def _():
        xa_ext[0:CONV_CARRY, :] = jnp.zeros((CONV_CARRY, LRU_WIDTH), F32)
        xb_ext[0:POOL_CARRY, :] = jnp.zeros((POOL_CARRY, POOL_WIDTH), F32)
        hcar[...] = jnp.zeros(hcar.shape, F32)

    x = x_ref[...]
    mod = mod_ref[...]
    shift_m, scale_m, gate_m = mod[0:1], mod[1:2], mod[2:3]
    shift_f, scale_f = mod[3:4], mod[4:5]
    nrm = nrm_ref[...]

    h = _rms(x, nrm[0:1]) * (1.0 + scale_m) + shift_m
    u = _dot(h.astype(BF16), w_in_ref[...])
    xa_raw = u[:, :LRU_WIDTH]
    ga = u[:, LRU_WIDTH:2 * LRU_WIDTH]
    xb = u[:, 2 * LRU_WIDTH:]

    lruv = lruv_ref[...]
    conv_b, gate_a_b, gate_x_b, lam = lruv[0:1], lruv[1:2], lruv[2:3], lruv[3:4]
    convw = convw_ref[...]
    xa_ext[CONV_CARRY:, :] = xa_raw
    ext = xa_ext[...]
    acc = ext * convw[0:1]
    for k in range(1, CONV_WIDTH):
        acc = acc + pltpu.roll(ext, k, axis=0) * convw[k:k + 1]
    xa = acc[CONV_CARRY:, :] + conv_b
    xa_ext[0:CONV_CARRY, :] = xa_raw[ts - CONV_CARRY:, :]

    g = _dot(xa.astype(BF16), wg_ref[...])
    r = jax.nn.sigmoid(g[:, :LRU_WIDTH] + gate_a_b)
    i = jax.nn.sigmoid(g[:, LRU_WIDTH:] + gate_x_b)
    softplus_neg_lam = jnp.maximum(-lam, 0.0) + jnp.log(1.0 + jnp.exp(-jnp.abs(lam)))
    log_a = -LRU_C * r * softplus_neg_lam
    a = jnp.exp(log_a)
    mult = jnp.sqrt(1.0 - a * a)
    bterm = mult * (i * xa)
    h_last = _lru_scan(a, bterm, hcar[0:1, :], hbuf)
    hcar[0:1, :] = h_last
    ya = hbuf[...] * jax.nn.gelu(ga)

    xb_ext[POOL_CARRY:, :] = xb
    t_glob = s * ts + lax.broadcasted_iota(I32, (ts, POOL_GROUP_DIM), 0)
    pooled = []
    for gi, w in enumerate(POOL_WINDOWS):
        lo = gi * POOL_GROUP_DIM
        e = xb_ext[:, lo:lo + POOL_GROUP_DIM]
        acc_w = e
        span = 1
        while span < w:
            acc_w = acc_w + pltpu.roll(acc_w, span, axis=0)
            span *= 2
        count = jnp.minimum(t_glob + 1, w).astype(F32)
        pooled.append(acc_w[POOL_CARRY:, :] / count - xb[:, lo:lo + POOL_GROUP_DIM])
    xb_ext[0:POOL_CARRY, :] = xb[ts - POOL_CARRY:, :]
    p = jnp.concatenate(pooled, axis=-1)
    poolv = poolv_ref[...]
    yb = (_dot(p.astype(BF16), wp_ref[...]) + poolv[0:1]) * poolv[1:2]

    mix = _dot(jnp.concatenate([ya, yb], axis=-1).astype(BF16), w_out_ref[...])
    x1 = x + gate_m * _rms(mix, nrm[1:2])
    x1_ref[...] = x1

    h2 = _rms(x1, nrm[2:3]) * (1.0 + scale_f) + shift_f
    h2_hi, h2_lo = _split_bf16(h2)
    h2_ref[...] = h2_hi

    rwh = rwh_ref[...]
    logits = _dot_nt(rwh, h2_hi) + _dot_nt(rwl_ref[...], h2_hi) + _dot_nt(rwh, h2_lo)
    logits = logits + rb_ref[...]

    e_iota = lax.broadcasted_iota(I32, (N_EXPERTS, ts), 0)
    work = logits
    sels, vals = [], []
    for _ in range(TOP_K):
        m = jnp.max(work, axis=0, keepdims=True)
        idx = jnp.min(jnp.where(work == m, e_iota, N_EXPERTS), axis=0, keepdims=True)
        sel = e_iota == idx
        work = jnp.where(sel, -jnp.inf, work)
        sels.append(sel)
        vals.append(m)
    exps = [jnp.exp(v - vals[0]) for v in vals]
    denom = exps[0] + exps[1] + exps[2] + exps[3]
    gates = [ex / denom for ex in exps]

    member = jnp.zeros((N_EXPERTS, ts), F32)
    for sel in sels:
        member = jnp.where(sel, 1.0, member)
    tri = (lax.broadcasted_iota(I32, (ts, ts), 0) < lax.broadcasted_iota(I32, (ts, ts), 1))
    rank = _dot(member.astype(BF16), jnp.where(tri, 1.0, 0.0).astype(BF16))
    n_e = jnp.sum(member, axis=1, keepdims=True).astype(I32)
    run_len = ((n_e + (SUBLANES - 1)) >> 3) << 3
    ltri = (lax.broadcasted_iota(I32, (N_EXPERTS, N_EXPERTS), 1)
            < lax.broadcasted_iota(I32, (N_EXPERTS, N_EXPERTS), 0))
    run_len_b = jnp.broadcast_to(run_len.astype(F32), (N_EXPERTS, ts)).astype(BF16)
    run_off = _dot(jnp.where(ltri, 1.0, 0.0).astype(BF16), run_len_b)
    slot = rank + run_off
    pos_rows = [jnp.sum(jnp.where(sel, slot, 0.0), axis=0, keepdims=True) for sel in sels]

    zeros_f = jnp.zeros((SUBLANES - TOP_K, ts), F32)
    gate_ref[...] = jnp.concatenate(gates + [zeros_f], axis=0)
    pos_ref[...] = jnp.concatenate(pos_rows + [zeros_f], axis=0).astype(I32)
    cnt_ref[...] = jnp.broadcast_to(n_e, (N_EXPERTS, LANES))


def _mixer_call(x, mod3, nrm, w_in_b, convw, lruv, wg, poolv, wp, w_out_b, rwh, rwl, rb):
    batch, seq, _ = x.shape
    ns = seq // TOKEN_TILE
    nt = batch * ns
    tokens = batch * seq
    const = lambda shape: pl.BlockSpec(shape, lambda b, s: (0,) * len(shape))
    tile_idx = lambda b, s: (0, b * ns + s)
    return pl.pallas_call(
        _mixer_kernel,
        grid=(batch, ns),
        in_specs=[
            pl.BlockSpec((None, TOKEN_TILE, D_MODEL), lambda b, s: (b, s, 0)),
            pl.BlockSpec((None, 6, D_MODEL), lambda b, s: (b, 0, 0)),
            const(nrm.shape), const(w_in_b.shape), const(convw.shape), const(lruv.shape),
            const(wg.shape), const(poolv.shape), const(wp.shape), const(w_out_b.shape),
            const(rwh.shape), const(rwl.shape), const(rb.shape),
        ],
        out_specs=[
            pl.BlockSpec((None, TOKEN_TILE, D_MODEL), lambda b, s: (b, s, 0)),
            pl.BlockSpec((TOKEN_TILE, D_MODEL), lambda b, s: (b * ns + s, 0)),
            pl.BlockSpec((SUBLANES, TOKEN_TILE), tile_idx),
            pl.BlockSpec((SUBLANES, TOKEN_TILE), tile_idx),
            pl.BlockSpec((None, N_EXPERTS, LANES), lambda b, s: (b * ns + s, 0, 0)),
        ],
        out_shape=[
            jax.ShapeDtypeStruct((batch, seq, D_MODEL), F32),
            jax.ShapeDtypeStruct((tokens, D_MODEL), BF16),
            jax.ShapeDtypeStruct((SUBLANES, tokens), F32),
            jax.ShapeDtypeStruct((SUBLANES, tokens), I32),
            jax.ShapeDtypeStruct((nt, N_EXPERTS, LANES), I32),
        ],
        scratch_shapes=[
            pltpu.VMEM((TOKEN_TILE + CONV_CARRY, LRU_WIDTH), F32),
            pltpu.VMEM((TOKEN_TILE + POOL_CARRY, POOL_WIDTH), F32),
            pltpu.VMEM((SUBLANES, LRU_WIDTH), F32),
            pltpu.VMEM((TOKEN_TILE, LRU_WIDTH), F32),
        ],
        compiler_params=pltpu.CompilerParams(
            dimension_semantics=("arbitrary", "arbitrary"), vmem_limit_bytes=VMEM_LIMIT),
        name="mixer",
    )(x, mod3, nrm, w_in_b, convw, lruv, wg, poolv, wp, w_out_b, rwh, rwl, rb)


def _run_copies(length, src_ref, src_off, dst_ref, dst_off, sem, wait=False):
    for size in RUN_SIZES:
        hit = (length & size) != 0

        @pl.when(hit)
        def _(src_off=src_off, dst_off=dst_off, size=size):
            copy = pltpu.make_async_copy(
                src_ref.at[pl.ds(pl.multiple_of(src_off, SUBLANES), size)],
                dst_ref.at[pl.ds(pl.multiple_of(dst_off, SUBLANES), size)],
                sem)
            if wait:
                copy.wait()
            else:
                copy.start()

        step = jnp.where(hit, size, 0)
        src_off = src_off + step
        dst_off = dst_off + step
    return src_off, dst_off


def _wait_rows(rows, hbm_ref, vmem_ref, sem):
    @pl.when(rows > 0)
    def _():
        n = pl.multiple_of(rows, SUBLANES)
        pltpu.make_async_copy(hbm_ref.at[pl.ds(0, n)], vmem_ref.at[pl.ds(0, n)], sem).wait()


def _dispatch_kernel(run_dst_ref, run_len_ref, tail_dst_ref, tail_len_ref, nused_ref,
                     h2_ref, pos_ref, xs_ref, sorted_ref, zero_ref, sem, zsem):
    j = pl.program_id(0)
    half = D_MODEL // 2

    @pl.when(j == 0)
    def _():
        zero_ref[...] = jnp.zeros(zero_ref.shape, U32)

        for wait in (False, True):
            def tail_body(e, carry, wait=wait):
                _run_copies(tail_len_ref[e], zero_ref, jnp.int32(0), xs_ref, tail_dst_ref[e], zsem,
                            wait=wait)
                return carry

            lax.fori_loop(0, N_EXPERTS, tail_body, 0)

            def spare_body(b, carry, wait=wait):
                copy = pltpu.make_async_copy(
                    zero_ref, xs_ref.at[pl.ds(pl.multiple_of(b * FFN_BLOCK, FFN_BLOCK), FFN_BLOCK)], zsem)
                if wait:
                    copy.wait()
                else:
                    copy.start()
                return carry

            lax.fori_loop(nused_ref[0], xs_ref.shape[0] // FFN_BLOCK, spare_body, 0)

    pos = pos_ref[...]
    r_iota = lax.broadcasted_iota(I32, (SORT_ROWS, TOKEN_TILE), 0)
    perm = jnp.zeros((SORT_ROWS, TOKEN_TILE), F32)
    for k in range(TOP_K):
        perm = jnp.where(r_iota == pos[k:k + 1, :], 1.0, perm)
    rows = _dot(perm.astype(BF16), h2_ref[...])
    sorted_ref[...] = _pack_pair(rows[:, :half], rows[:, half:])

    def run_body(e, src_off):
        n = run_len_ref[j * N_EXPERTS + e]
        src_off, _ = _run_copies(n, sorted_ref, src_off, xs_ref, run_dst_ref[j * N_EXPERTS + e], sem)
        return src_off

    total = lax.fori_loop(0, N_EXPERTS, run_body, 0)
    _wait_rows(total, xs_ref, sorted_ref, sem)


def _dispatch_call(run_dst, run_len, tail_dst, tail_len, nused, h2, pos, cap):
    tokens = h2.shape[0]
    nt = tokens // TOKEN_TILE
    half = D_MODEL // 2
    grid_spec = pltpu.PrefetchScalarGridSpec(
        num_scalar_prefetch=5,
        grid=(nt,),
        in_specs=[
            pl.BlockSpec((TOKEN_TILE, D_MODEL), lambda j, *_: (j, 0)),
            pl.BlockSpec((SUBLANES, TOKEN_TILE), lambda j, *_: (0, j)),
        ],
        out_specs=pl.BlockSpec(memory_space=pl.ANY),
        scratch_shapes=[
            pltpu.VMEM((SORT_ROWS, half), U32),
            pltpu.VMEM((FFN_BLOCK, half), U32),
            pltpu.SemaphoreType.DMA(()),
            pltpu.SemaphoreType.DMA(()),
        ],
    )
    return pl.pallas_call(
        _dispatch_kernel,
        grid_spec=grid_spec,
        out_shape=jax.ShapeDtypeStruct((cap, half), U32),
        compiler_params=pltpu.CompilerParams(
            dimension_semantics=("arbitrary",), vmem_limit_bytes=VMEM_LIMIT),
        name="dispatch",
    )(run_dst, run_len, tail_dst, tail_len, nused, h2, pos)


def _ffn_kernel(blk_e_ref, nused_ref, xs_ref, w1_ref, b1_ref, w2_ref, b2_ref, y_ref):
    i = pl.program_id(0)
    half = D_MODEL // 2

    @pl.when(i < nused_ref[0])
    def _():
        x_a, x_b = _unpack_pair(xs_ref[...])
        u = _dot(x_a, w1_ref[0:half, :]) + _dot(x_b, w1_ref[half:, :]) + b1_ref[...]
        glu = jnp.minimum(u[:, :D_EXPERT], SWIGLU_LIMIT)
        lin = jnp.clip(u[:, D_EXPERT:], -SWIGLU_LIMIT, SWIGLU_LIMIT)
        act = glu * jax.nn.sigmoid(SWIGLU_ALPHA * glu) * (lin + 1.0)
        y = _dot(act.astype(BF16), w2_ref[...]) + b2_ref[...]
        yb = y.astype(BF16).astype(F32)
        y_ref[...] = _pack_pair(yb[:, :half], yb[:, half:])

    @pl.when(i >= nused_ref[0])
    def _():
        y_ref[...] = jnp.zeros(y_ref.shape, U32)


def _ffn_call(blk_e, nused, xs, w1p, b1p, w2b, b2):
    cap, half = xs.shape
    nb = cap // FFN_BLOCK
    row_blk = lambda i, be, nu: (jnp.minimum(i, nu[0] - 1), 0)
    grid_spec = pltpu.PrefetchScalarGridSpec(
        num_scalar_prefetch=2,
        grid=(nb,),
        in_specs=[
            pl.BlockSpec((FFN_BLOCK, half), row_blk),
            pl.BlockSpec((None, D_MODEL, 2 * D_EXPERT), lambda i, be, nu: (be[i], 0, 0)),
            pl.BlockSpec((None, 1, 2 * D_EXPERT), lambda i, be, nu: (be[i], 0, 0)),
            pl.BlockSpec((None, D_EXPERT, D_MODEL), lambda i, be, nu: (be[i], 0, 0)),
            pl.BlockSpec((None, 1, D_MODEL), lambda i, be, nu: (be[i], 0, 0)),
        ],
        out_specs=pl.BlockSpec((FFN_BLOCK, half), lambda i, be, nu: (i, 0)),
    )
    return pl.pallas_call(
        _ffn_kernel,
        grid_spec=grid_spec,
        out_shape=jax.ShapeDtypeStruct((cap, half), U32),
        compiler_params=pltpu.CompilerParams(
            dimension_semantics=("arbitrary",), vmem_limit_bytes=VMEM_LIMIT),
        name="ffn",
    )(blk_e, nused, xs, w1p, b1p, w2b, b2)


def _combine_kernel(run_dst_ref, run_len_ref, y_ref, pos_ref, gate_ref, x1_ref, mod_ref, nrm_ref,
                    out_ref, ybuf, sem):
    j = pl.program_id(0)

    @pl.when(j == 0)
    def _():
        ybuf[...] = jnp.zeros(ybuf.shape, U32)

    def run_body(e, dst_off):
        n = run_len_ref[j * N_EXPERTS + e]
        _, dst_off = _run_copies(n, y_ref, run_dst_ref[j * N_EXPERTS + e], ybuf, dst_off, sem)
        return dst_off

    total = lax.fori_loop(0, N_EXPERTS, run_body, 0)

    pos_t = jnp.transpose(pos_ref[...].astype(F32))
    gate_t = jnp.transpose(gate_ref[...])
    c_iota = lax.broadcasted_iota(I32, (TOKEN_TILE, SORT_ROWS), 1).astype(F32)
    wsel = jnp.zeros((TOKEN_TILE, SORT_ROWS), F32)
    for k in range(TOP_K):
        wsel = jnp.where(c_iota == pos_t[:, k:k + 1], gate_t[:, k:k + 1], wsel)
    w_hi, w_lo = _split_bf16(wsel)

    _wait_rows(total, y_ref, ybuf, sem)
    y_a, y_b = _unpack_pair(ybuf[...])
    f = jnp.concatenate([_dot(w_hi, y_a) + _dot(w_lo, y_a), _dot(w_hi, y_b) + _dot(w_lo, y_b)],
                        axis=-1)
    gate_f = mod_ref[5:6, :]
    out_ref[...] = x1_ref[...] + gate_f * _rms(f, nrm_ref[...])


def _combine_call(run_dst, run_len, y, pos, gate, x1, mod3, nrm_post):
    batch, seq, _ = x1.shape
    ns = seq // TOKEN_TILE
    nt = batch * ns
    half = D_MODEL // 2
    grid_spec = pltpu.PrefetchScalarGridSpec(
        num_scalar_prefetch=2,
        grid=(nt,),
        in_specs=[
            pl.BlockSpec(memory_space=pl.ANY),
            pl.BlockSpec((SUBLANES, TOKEN_TILE), lambda j, *_: (0, j)),
            pl.BlockSpec((SUBLANES, TOKEN_TILE), lambda j, *_: (0, j)),
            pl.BlockSpec((None, TOKEN_TILE, D_MODEL), lambda j, *_: (j // ns, j % ns, 0)),
            pl.BlockSpec((None, 6, D_MODEL), lambda j, *_: (j // ns, 0, 0)),
            pl.BlockSpec((1, D_MODEL), lambda j, *_: (0, 0)),
        ],
        out_specs=pl.BlockSpec((None, TOKEN_TILE, D_MODEL), lambda j, *_: (j // ns, j % ns, 0)),
        scratch_shapes=[
            pltpu.VMEM((SORT_ROWS, half), U32),
            pltpu.SemaphoreType.DMA(()),
        ],
    )
    return pl.pallas_call(
        _combine_kernel,
        grid_spec=grid_spec,
        out_shape=jax.ShapeDtypeStruct((batch, seq, D_MODEL), F32),
        compiler_params=pltpu.CompilerParams(
            dimension_semantics=("arbitrary",), vmem_limit_bytes=VMEM_LIMIT),
        name="combine",
    )(run_dst, run_len, y, pos, gate, x1, mod3, nrm_post)


def _block_diag(w):
    heads, d, _ = w.shape
    eye = jnp.eye(heads, dtype=w.dtype)
    return jnp.einsum("hij,hg->higj", w, eye).reshape(heads * d, heads * d)


def _route_plan(cnt):
    nt = cnt.shape[0]
    run_len = (cnt + (SUBLANES - 1)) // SUBLANES * SUBLANES
    region = jnp.sum(run_len, axis=0)
    region_blk = (region + FFN_BLOCK - 1) // FFN_BLOCK * FFN_BLOCK
    region_end = jnp.cumsum(region_blk)
    region_start = region_end - region_blk
    run_dst = region_start[None, :] + jnp.cumsum(run_len, axis=0) - run_len
    tail_dst = region_start + region
    tail_len = region_blk - region
    nused = region_end[-1] // FFN_BLOCK
    max_rows = nt * TOKEN_TILE * TOP_K + nt * N_EXPERTS * (SUBLANES - 1) + N_EXPERTS * (FFN_BLOCK - SUBLANES)
    nb = -(-max_rows // FFN_BLOCK)
    blk_start = jnp.arange(nb, dtype=I32) * FFN_BLOCK
    blk_e = jnp.searchsorted(region_end, jnp.minimum(blk_start, region_end[-1] - 1), side="right")
    blk_e = jnp.minimum(blk_e, N_EXPERTS - 1).astype(I32)
    return (run_dst.reshape(-1).astype(I32), run_len.reshape(-1).astype(I32),
            tail_dst.astype(I32), tail_len.astype(I32), blk_e,
            nused.reshape(1).astype(I32), nb * FFN_BLOCK)


def kernel(x, c, w_ada, b_ada, norm_mix_pre, norm_mix_post, w_in, conv_w, conv_b, gate_a_w, gate_a_b, gate_x_w, gate_x_b, lru_lambda, pool_w, pool_b, pool_scale, w_out, norm_ffn_pre, norm_ffn_post, router_w, router_b, expert_w1, expert_b1, expert_w2, expert_b2):
    depth = w_ada.shape[0]
    batch = x.shape[0]
    for l in range(depth):
        mod3 = _ada_call(c, w_ada[l], b_ada[l]).reshape(batch, 6, D_MODEL)

        nrm = jnp.stack([norm_mix_pre[l], norm_mix_post[l], norm_ffn_pre[l]])
        lruv = jnp.stack([conv_b[l], gate_a_b[l], gate_x_b[l], lru_lambda[l]])
        wg = jnp.concatenate([_block_diag(gate_a_w[l]), _block_diag(gate_x_w[l])], axis=1).astype(BF16)
        poolv = jnp.stack([pool_b[l], pool_scale[l]])
        wp = _block_diag(pool_w[l]).astype(BF16)
        rw_t = router_w[l].T
        rwh = rw_t.astype(BF16)
        rwl = (rw_t - rwh.astype(F32)).astype(BF16)
        rb = router_b[l].reshape(N_EXPERTS, 1)

        x1, h2, gate, pos, cnt = _mixer_call(
            x, mod3, nrm, w_in[l].astype(BF16), conv_w[l], lruv, wg, poolv, wp,
            w_out[l].astype(BF16), rwh, rwl, rb)

        run_dst, run_len, tail_dst, tail_len, blk_e, nused, cap = _route_plan(cnt[:, :, 0])
        xs = _dispatch_call(run_dst, run_len, tail_dst, tail_len, nused, h2, pos, cap)

        w1p = jnp.concatenate([expert_w1[l][:, :, 0::2], expert_w1[l][:, :, 1::2]], axis=-1).astype(BF16)
        b1p = jnp.concatenate([expert_b1[l][:, 0::2], expert_b1[l][:, 1::2]], axis=-1)
        y = _ffn_call(blk_e, nused, xs, w1p, b1p.reshape(N_EXPERTS, 1, 2 * D_EXPERT),
                      expert_w2[l].astype(BF16), expert_b2[l].reshape(N_EXPERTS, 1, D_MODEL))

        x = _combine_call(run_dst, run_len, y, pos, gate, x1, mod3,
                          norm_ffn_post[l].reshape(1, D_MODEL))
    return x
```

```python
import functools

import jax
import jax.numpy as jnp
from jax import lax
from jax.experimental import pallas as pl
from jax.experimental.pallas import tpu as pltpu

F32 = jnp.float32
BF16 = jnp.bfloat16
I32 = jnp.int32
U32 = jnp.uint32

D_MODEL = 1024
LRU_WIDTH = 512
LRU_HEADS = 8
POOL_WIDTH = 512
POOL_WINDOWS = (2, 4, 8, 16)
POOL_GROUP_DIM = 128
CONV_WIDTH = 4
IN_WIDTH = 2 * LRU_WIDTH + POOL_WIDTH
LRU_C = 8.0
N_EXPERTS = 32
TOP_K = 4
D_EXPERT = 1024
SWIGLU_LIMIT = 7.0
SWIGLU_ALPHA = 1.702
NORM_EPS = 1e-6

SUBLANES = 8
LANES = 128
TOKEN_TILE = 256
FFN_BLOCK = 256
CONV_CARRY = SUBLANES
POOL_CARRY = 16
SORT_ROWS = 1280
RUN_SIZES = (256, 128, 64, 32, 16, 8)
VMEM_LIMIT = 52 * 1024 * 1024


def _rms(v, g):
    return v * lax.rsqrt(jnp.mean(v * v, axis=-1, keepdims=True) + NORM_EPS) * g


def _dot(a, b):
    return jnp.dot(a, b, preferred_element_type=F32)


def _dot_nt(a, b):
    return lax.dot_general(a, b, (((1,), (1,)), ((), ())), preferred_element_type=F32)


def _split_bf16(v):
    hi = v.astype(BF16)
    lo = (v - hi.astype(F32)).astype(BF16)
    return hi, lo


def _pack_pair(a, b):
    ua = lax.bitcast_convert_type(a, U32)
    ub = lax.bitcast_convert_type(b, U32)
    return (ua >> 16) | (ub & jnp.uint32(0xFFFF0000))


def _unpack_pair(w):
    a = lax.bitcast_convert_type(w << 16, F32).astype(BF16)
    b = lax.bitcast_convert_type(w & jnp.uint32(0xFFFF0000), F32).astype(BF16)
    return a, b


def _ada_kernel(c_ref, w_ref, b_ref, o_ref):
    c = c_ref[...]
    ca = c * jax.nn.sigmoid(c)
    o_ref[...] = jnp.dot(ca, w_ref[...], preferred_element_type=F32,
                         precision=lax.Precision.HIGHEST) + b_ref[...]


def _ada_call(c, w_ada, b_ada):
    batch = c.shape[0]
    n_out = w_ada.shape[1]
    bn = D_MODEL
    return pl.pallas_call(
        _ada_kernel,
        grid=(n_out // bn,),
        in_specs=[
            pl.BlockSpec((batch, D_MODEL), lambda j: (0, 0)),
            pl.BlockSpec((D_MODEL, bn), lambda j: (0, j)),
            pl.BlockSpec((1, bn), lambda j: (0, j)),
        ],
        out_specs=pl.BlockSpec((batch, bn), lambda j: (0, j)),
        out_shape=jax.ShapeDtypeStruct((batch, n_out), F32),
        name="ada",
    )(c, w_ada, b_ada.reshape(1, n_out))


def _lru_scan(a, b, h0, out_ref):
    ts, c = a.shape
    groups = ts // SUBLANES
    a3 = a.reshape(groups, SUBLANES, c)
    b3 = b.reshape(groups, SUBLANES, c)
    row = lax.broadcasted_iota(I32, (groups, SUBLANES, c), 1)
    d = 1
    while d < SUBLANES:
        a_sh = pltpu.roll(a3, d, axis=1)
        b_sh = pltpu.roll(b3, d, axis=1)
        m = row >= d
        b3 = jnp.where(m, a3 * b_sh + b3, b3)
        a3 = jnp.where(m, a3 * a_sh, a3)
        d *= 2
    h = h0
    for g in range(groups):
        hg = a3[g] * h + b3[g]
        out_ref[pl.ds(g * SUBLANES, SUBLANES), :] = hg
        h = hg[SUBLANES - 1:SUBLANES, :]
    return h


def _mixer_kernel(x_ref, mod_ref, nrm_ref, w_in_ref, convw_ref, lruv_ref, wg_ref,
                  poolv_ref, wp_ref, w_out_ref, rwh_ref, rwl_ref, rb_ref,
                  x1_ref, h2_ref, gate_ref, pos_ref, cnt_ref,
                  xa_ext, xb_ext, hcar, hbuf):
    s = pl.program_id(1)
    ts = TOKEN_TILE

    @pl.when(s == 0)
    def _():
        xa_ext[0:CONV_CARRY, :] = jnp.zeros((CONV_CARRY, LRU_WIDTH), F32)
        xb_ext[0:POOL_CARRY, :] = jnp.zeros((POOL_CARRY, POOL_WIDTH), F32)
        hcar[...] = jnp.zeros(hcar.shape, F32)

    x = x_ref[...]
    mod = mod_ref[...]
    shift_m, scale_m, gate_m = mod[0:1], mod[1:2], mod[2:3]
    shift_f, scale_f = mod[3:4], mod[4:5]
    nrm = nrm_ref[...]

    h = _rms(x, nrm[0:1]) * (1.0 + scale_m) + shift_m
    u = _dot(h.astype(BF16), w_in_ref[...])
    xa_raw = u[:, :LRU_WIDTH]
    ga = u[:, LRU_WIDTH:2 * LRU_WIDTH]
    xb = u[:, 2 * LRU_WIDTH:]

    lruv = lruv_ref[...]
    conv_b, gate_a_b, gate_x_b, lam = lruv[0:1], lruv[1:2], lruv[2:3], lruv[3:4]
    convw = convw_ref[...]
    xa_ext[CONV_CARRY:, :] = xa_raw
    ext = xa_ext[...]
    acc = ext * convw[0:1]
    for k in range(1, CONV_WIDTH):
        acc = acc + pltpu.roll(ext, k, axis=0) * convw[k:k + 1]
    xa = acc[CONV_CARRY:, :] + conv_b
    xa_ext[0:CONV_CARRY, :] = xa_raw[ts - CONV_CARRY:, :]

    g = _dot(xa.astype(BF16), wg_ref[...])
    r = jax.nn.sigmoid(g[:, :LRU_WIDTH] + gate_a_b)
    i = jax.nn.sigmoid(g[:, LRU_WIDTH:] + gate_x_b)
    softplus_neg_lam = jnp.maximum(-lam, 0.0) + jnp.log(1.0 + jnp.exp(-jnp.abs(lam)))
    log_a = -LRU_C * r * softplus_neg_lam
    a = jnp.exp(log_a)
    mult = jnp.sqrt(1.0 - a * a)
    bterm = mult * (i * xa)
    h_last = _lru_scan(a, bterm, hcar[0:1, :], hbuf)
    hcar[0:1, :] = h_last
    ya = hbuf[...] * jax.nn.gelu(ga)

    xb_ext[POOL_CARRY:, :] = xb
    t_glob = s * ts + lax.broadcasted_iota(I32, (ts, POOL_GROUP_DIM), 0)
    pooled = []
    for gi, w in enumerate(POOL_WINDOWS):
        lo = gi * POOL_GROUP_DIM
        e = xb_ext[:, lo:lo + POOL_GROUP_DIM]
        acc_w = e
        span = 1
        while span < w:
            acc_w = acc_w + pltpu.roll(acc_w, span, axis=0)
            span *= 2
        count = jnp.minimum(t_glob + 1, w).astype(F32)
        pooled.append(acc_w[POOL_CARRY:, :] / count - xb[:, lo:lo + POOL_GROUP_DIM])
    xb_ext[0:POOL_CARRY, :] = xb[ts - POOL_CARRY:, :]
    p = jnp.concatenate(pooled, axis=-1)
    poolv = poolv_ref[...]
    yb = (_dot(p.astype(BF16), wp_ref[...]) + poolv[0:1]) * poolv[1:2]

    mix = _dot(jnp.concatenate([ya, yb], axis=-1).astype(BF16), w_out_ref[...])
    x1 = x + gate_m * _rms(mix, nrm[1:2])
    x1_ref[...] = x1

    h2 = _rms(x1, nrm[2:3]) * (1.0 + scale_f) + shift_f
    h2_hi, h2_lo = _split_bf16(h2)
    h2_ref[...] = h2_hi

    rwh = rwh_ref[...]
    logits = _dot_nt(rwh, h2_hi) + _dot_nt(rwl_ref[...], h2_hi) + _dot_nt(rwh, h2_lo)
    logits = logits + rb_ref[...]

    e_iota = lax.broadcasted_iota(I32, (N_EXPERTS, ts), 0)
    work = logits
    sels, vals = [], []
    for _ in range(TOP_K):
        m = jnp.max(work, axis=0, keepdims=True)
        idx = jnp.min(jnp.where(work == m, e_iota, N_EXPERTS), axis=0, keepdims=True)
        sel = e_iota == idx
        work = jnp.where(sel, -jnp.inf, work)
        sels.append(sel)
        vals.append(m)
    exps = [jnp.exp(v - vals[0]) for v in vals]
    denom = exps[0] + exps[1] + exps[2] + exps[3]
    gates = [ex / denom for ex in exps]

    member = jnp.zeros((N_EXPERTS, ts), F32)
    for sel in sels:
        member = jnp.where(sel, 1.0, member)
    tri = (lax.broadcasted_iota(I32, (ts, ts), 0) < lax.broadcasted_iota(I32, (ts, ts), 1))
    rank = _dot(member.astype(BF16), jnp.where(tri, 1.0, 0.0).astype(BF16))
    n_e = jnp.sum(member, axis=1, keepdims=True).astype(I32)
    run_len = ((n_e + (SUBLANES - 1)) >> 3) << 3
    ltri = (lax.broadcasted_iota(I32, (N_EXPERTS, N_EXPERTS), 1)
            < lax.broadcasted_iota(I32, (N_EXPERTS, N_EXPERTS), 0))
    run_len_b = jnp.broadcast_to(run_len.astype(F32), (N_EXPERTS, ts)).astype(BF16)
    run_off = _dot(jnp.where(ltri, 1.0, 0.0).astype(BF16), run_len_b)
    slot = rank + run_off
    pos_rows = [jnp.sum(jnp.where(sel, slot, 0.0), axis=0, keepdims=True) for sel in sels]

    zeros_f = jnp.zeros((SUBLANES - TOP_K, ts), F32)
    gate_ref[...] = jnp.concatenate(gates + [zeros_f], axis=0)
    pos_ref[...] = jnp.concatenate(pos_rows + [zeros_f], axis=0).astype(I32)
    cnt_ref[...] = jnp.broadcast_to(n_e, (N_EXPERTS, LANES))


def _mixer_call(x, mod3, nrm, w_in_b, convw, lruv, wg, poolv, wp, w_out_b, rwh, rwl, rb):
    batch, seq, _ = x.shape
    ns = seq // TOKEN_TILE
    nt = batch * ns
    tokens = batch * seq
    const = lambda shape: pl.BlockSpec(shape, lambda b, s: (0,) * len(shape))
    tile_idx = lambda b, s: (0, b * ns + s)
    return pl.pallas_call(
        _mixer_kernel,
        grid=(batch, ns),
        in_specs=[
            pl.BlockSpec((None, TOKEN_TILE, D_MODEL), lambda b, s: (b, s, 0)),
            pl.BlockSpec((None, 6, D_MODEL), lambda b, s: (b, 0, 0)),
            const(nrm.shape), const(w_in_b.shape), const(convw.shape), const(lruv.shape),
            const(wg.shape), const(poolv.shape), const(wp.shape), const(w_out_b.shape),
            const(rwh.shape), const(rwl.shape), const(rb.shape),
        ],
        out_specs=[
            pl.BlockSpec((None, TOKEN_TILE, D_MODEL), lambda b, s: (b, s, 0)),
            pl.BlockSpec((TOKEN_TILE, D_MODEL), lambda b, s: (b * ns + s, 0)),
            pl.BlockSpec((SUBLANES, TOKEN_TILE), tile_idx),
            pl.BlockSpec((SUBLANES, TOKEN_TILE), tile_idx),
            pl.BlockSpec((None, N_EXPERTS, LANES), lambda b, s: (b * ns + s, 0, 0)),
        ],
        out_shape=[
            jax.ShapeDtypeStruct((batch, seq, D_MODEL), F32),
            jax.ShapeDtypeStruct((tokens, D_MODEL), BF16),
            jax.ShapeDtypeStruct((SUBLANES, tokens), F32),
            jax.ShapeDtypeStruct((SUBLANES, tokens), I32),
            jax.ShapeDtypeStruct((nt, N_EXPERTS, LANES), I32),
        ],
        scratch_shapes=[
            pltpu.VMEM((TOKEN_TILE + CONV_CARRY, LRU_WIDTH), F32),
            pltpu.VMEM((TOKEN_TILE + POOL_CARRY, POOL_WIDTH), F32),
            pltpu.VMEM((SUBLANES, LRU_WIDTH), F32),
            pltpu.VMEM((TOKEN_TILE, LRU_WIDTH), F32),
        ],
        compiler_params=pltpu.CompilerParams(
            dimension_semantics=("arbitrary", "arbitrary"), vmem_limit_bytes=VMEM_LIMIT),
        name="mixer",
    )(x, mod3, nrm, w_in_b, convw, lruv, wg, poolv, wp, w_out_b, rwh, rwl, rb)


def _run_copies(length, src_ref, src_off, dst_ref, dst_off, sem, wait=False):
    for size in RUN_SIZES:
        hit = (length & size) != 0

        @pl.when(hit)
        def _(src_off=src_off, dst_off=dst_off, size=size):
            copy = pltpu.make_async_copy(
                src_ref.at[pl.ds(pl.multiple_of(src_off, SUBLANES), size)],
                dst_ref.at[pl.ds(pl.multiple_of(dst_off, SUBLANES), size)],
                sem)
            if wait:
                copy.wait()
            else:
                copy.start()

        step = jnp.where(hit, size, 0)
        src_off = src_off + step
        dst_off = dst_off + step
    return src_off, dst_off


def _wait_rows(rows, hbm_ref, vmem_ref, sem):
    @pl.when(rows > 0)
    def _():
        n = pl.multiple_of(rows, SUBLANES)
        pltpu.make_async_copy(hbm_ref.at[pl.ds(0, n)], vmem_ref.at[pl.ds(0, n)], sem).wait()


def _dispatch_kernel(run_dst_ref, run_len_ref, tail_dst_ref, tail_len_ref, nused_ref,
                     h2_ref, pos_ref, xs_ref, sorted_ref, zero_ref, sem, zsem):
    j = pl.program_id(0)
    half = D_MODEL // 2

    @pl.when(j == 0)
    def _():
        zero_ref[...] = jnp.zeros(zero_ref.shape, U32)

        for wait in (False, True):
            def tail_body(e, carry, wait=wait):
                _run_copies(tail_len_ref[e], zero_ref, jnp.int32(0), xs_ref, tail_dst_ref[e], zsem,
                            wait=wait)
                return carry

            lax.fori_loop(0, N_EXPERTS, tail_body, 0)

            def spare_body(b, carry, wait=wait):
                copy = pltpu.make_async_copy(
                    zero_ref, xs_ref.at[pl.ds(pl.multiple_of(b * FFN_BLOCK, FFN_BLOCK), FFN_BLOCK)], zsem)
                if wait:
                    copy.wait()
                else:
                    copy.start()
                return carry

            lax.fori_loop(nused_ref[0], xs_ref.shape[0] // FFN_BLOCK, spare_body, 0)

    pos = pos_ref[...]
    r_iota = lax.broadcasted_iota(I32, (SORT_ROWS, TOKEN_TILE), 0)
    perm = jnp.zeros((SORT_ROWS, TOKEN_TILE), F32)
    for k in range(TOP_K):
        perm = jnp.where(r_iota == pos[k:k + 1, :], 1.0, perm)
    rows = _dot(perm.astype(BF16), h2_ref[...])
    sorted_ref[...] = _pack_pair(rows[:, :half], rows[:, half:])

    def run_body(e, src_off):
        n = run_len_ref[j * N_EXPERTS + e]
        src_off, _ = _run_copies(n, sorted_ref, src_off, xs_ref, run_dst_ref[j * N_EXPERTS + e], sem)
        return src_off

    total = lax.fori_loop(0, N_EXPERTS, run_body, 0)
    _wait_rows(total, xs_ref, sorted_ref, sem)


def _dispatch_call(run_dst, run_len, tail_dst, tail_len, nused, h2, pos, cap):
    tokens = h2.shape[0]
    nt = tokens // TOKEN_TILE
    half = D_MODEL // 2
    grid_spec = pltpu.PrefetchScalarGridSpec(
        num_scalar_prefetch=5,
        grid=(nt,),
        in_specs=[
            pl.BlockSpec((TOKEN_TILE, D_MODEL), lambda j, *_: (j, 0)),
            pl.BlockSpec((SUBLANES, TOKEN_TILE), lambda j, *_: (0, j)),
        ],
        out_specs=pl.BlockSpec(memory_space=pl.ANY),
        scratch_shapes=[
            pltpu.VMEM((SORT_ROWS, half), U32),
            pltpu.VMEM((FFN_BLOCK, half), U32),
            pltpu.SemaphoreType.DMA(()),
            pltpu.SemaphoreType.DMA(()),
        ],
    )
    return pl.pallas_call(
        _dispatch_kernel,
        grid_spec=grid_spec,
        out_shape=jax.ShapeDtypeStruct((cap, half), U32),
        compiler_params=pltpu.CompilerParams(
            dimension_semantics=("arbitrary",), vmem_limit_bytes=VMEM_LIMIT),
        name="dispatch",
    )(run_dst, run_len, tail_dst, tail_len, nused, h2, pos)


W1_CHUNK = 256


def _ffn_kernel(blk_e_ref, nused_ref, xs_ref, w1_ref, b1_ref, w2_ref, b2_ref, y_ref, w1s, w2s):
    i = pl.program_id(0)
    half = D_MODEL // 2
    prev = jnp.maximum(i - 1, 0)
    new_expert = jnp.logical_or(i == 0, blk_e_ref[i] != blk_e_ref[prev])

    @pl.when(jnp.logical_and(i < nused_ref[0], new_expert))
    def _():
        hc = W1_CHUNK // 2
        r = lax.broadcasted_iota(I32, (W1_CHUNK, W1_CHUNK), 0)
        q = lax.broadcasted_iota(I32, (W1_CHUNK, W1_CHUNK), 1)
        src_col = jnp.where(q < hc, 2 * q, 2 * (q - hc) + 1)
        sel = jnp.where(r == src_col, 1.0, 0.0).astype(BF16)
        for c in range(2 * D_EXPERT // W1_CHUNK):
            chunk = w1_ref[:, c * W1_CHUNK:(c + 1) * W1_CHUNK].astype(BF16)
            picked = _dot(chunk, sel).astype(BF16)
            w1s[:, c * hc:(c + 1) * hc] = picked[:, :hc]
            w1s[:, D_EXPERT + c * hc:D_EXPERT + (c + 1) * hc] = picked[:, hc:]
        w2s[...] = w2_ref[...].astype(BF16)

    @pl.when(i < nused_ref[0])
    def _():
        x_a, x_b = _unpack_pair(xs_ref[...])
        u = _dot(x_a, w1s[0:half, :]) + _dot(x_b, w1s[half:, :]) + b1_ref[...]
        glu = jnp.minimum(u[:, :D_EXPERT], SWIGLU_LIMIT)
        lin = jnp.clip(u[:, D_EXPERT:], -SWIGLU_LIMIT, SWIGLU_LIMIT)
        act = glu * jax.nn.sigmoid(SWIGLU_ALPHA * glu) * (lin + 1.0)
        y = _dot(act.astype(BF16), w2s[...]) + b2_ref[...]
        yb = y.astype(BF16).astype(F32)
        y_ref[...] = _pack_pair(yb[:, :half], yb[:, half:])

    @pl.when(i >= nused_ref[0])
    def _():
        y_ref[...] = jnp.zeros(y_ref.shape, U32)


def _ffn_call(blk_e, nused, xs, w1, b1p, w2, b2):
    cap, half = xs.shape
    nb = cap // FFN_BLOCK
    row_blk = lambda i, be, nu: (jnp.minimum(i, nu[0] - 1), 0)
    grid_spec = pltpu.PrefetchScalarGridSpec(
        num_scalar_prefetch=2,
        grid=(nb,),
        in_specs=[
            pl.BlockSpec((FFN_BLOCK, half), row_blk),
            pl.BlockSpec((None, D_MODEL, 2 * D_EXPERT), lambda i, be, nu: (be[i], 0, 0)),
            pl.BlockSpec((None, 1, 2 * D_EXPERT), lambda i, be, nu: (be[i], 0, 0)),
            pl.BlockSpec((None, D_EXPERT, D_MODEL), lambda i, be, nu: (be[i], 0, 0)),
            pl.BlockSpec((None, 1, D_MODEL), lambda i, be, nu: (be[i], 0, 0)),
        ],
        out_specs=pl.BlockSpec((FFN_BLOCK, half), lambda i, be, nu: (i, 0)),
        scratch_shapes=[
            pltpu.VMEM((D_MODEL, 2 * D_EXPERT), BF16),
            pltpu.VMEM((D_EXPERT, D_MODEL), BF16),
        ],
    )
    return pl.pallas_call(
        _ffn_kernel,
        grid_spec=grid_spec,
        out_shape=jax.ShapeDtypeStruct((cap, half), U32),
        compiler_params=pltpu.CompilerParams(
            dimension_semantics=("arbitrary",), vmem_limit_bytes=VMEM_LIMIT),
        name="ffn",
    )(blk_e, nused, xs, w1, b1p, w2, b2)


def _combine_kernel(run_dst_ref, run_len_ref, y_ref, pos_ref, gate_ref, x1_ref, mod_ref, nrm_ref,
                    out_ref, ybuf, sem):
    j = pl.program_id(0)

    @pl.when(j == 0)
    def _():
        ybuf[...] = jnp.zeros(ybuf.shape, U32)

    def run_body(e, dst_off):
        n = run_len_ref[j * N_EXPERTS + e]
        _, dst_off = _run_copies(n, y_ref, run_dst_ref[j * N_EXPERTS + e], ybuf, dst_off, sem)
        return dst_off

    total = lax.fori_loop(0, N_EXPERTS, run_body, 0)

    pos_t = jnp.transpose(pos_ref[...].astype(F32))
    gate_t = jnp.transpose(gate_ref[...])
    c_iota = lax.broadcasted_iota(I32, (TOKEN_TILE, SORT_ROWS), 1).astype(F32)
    wsel = jnp.zeros((TOKEN_TILE, SORT_ROWS), F32)
    for k in range(TOP_K):
        wsel = jnp.where(c_iota == pos_t[:, k:k + 1], gate_t[:, k:k + 1], wsel)
    w_hi, w_lo = _split_bf16(wsel)

    _wait_rows(total, y_ref, ybuf, sem)
    y_a, y_b = _unpack_pair(ybuf[...])
    f = jnp.concatenate([_dot(w_hi, y_a) + _dot(w_lo, y_a), _dot(w_hi, y_b) + _dot(w_lo, y_b)],
                        axis=-1)
    gate_f = mod_ref[5:6, :]
    out_ref[...] = x1_ref[...] + gate_f * _rms(f, nrm_ref[...])


def _combine_call(run_dst, run_len, y, pos, gate, x1, mod3, nrm_post):
    batch, seq, _ = x1.shape
    ns = seq // TOKEN_TILE
    nt = batch * ns
    half = D_MODEL // 2
    grid_spec = pltpu.PrefetchScalarGridSpec(
        num_scalar_prefetch=2,
        grid=(nt,),
        in_specs=[
            pl.BlockSpec(memory_space=pl.ANY),
            pl.BlockSpec((SUBLANES, TOKEN_TILE), lambda j, *_: (0, j)),
            pl.BlockSpec((SUBLANES, TOKEN_TILE), lambda j, *_: (0, j)),
            pl.BlockSpec((None, TOKEN_TILE, D_MODEL), lambda j, *_: (j // ns, j % ns, 0)),
            pl.BlockSpec((None, 6, D_MODEL), lambda j, *_: (j // ns, 0, 0)),
            pl.BlockSpec((1, D_MODEL), lambda j, *_: (0, 0)),
        ],
        out_specs=pl.BlockSpec((None, TOKEN_TILE, D_MODEL), lambda j, *_: (j // ns, j % ns, 0)),
        scratch_shapes=[
            pltpu.VMEM((SORT_ROWS, half), U32),
            pltpu.SemaphoreType.DMA(()),
        ],
    )
    return pl.pallas_call(
        _combine_kernel,
        grid_spec=grid_spec,
        out_shape=jax.ShapeDtypeStruct((batch, seq, D_MODEL), F32),
        compiler_params=pltpu.CompilerParams(
            dimension_semantics=("arbitrary",), vmem_limit_bytes=VMEM_LIMIT),
        name="combine",
    )(run_dst, run_len, y, pos, gate, x1, mod3, nrm_post)


def _block_diag(w):
    heads, d, _ = w.shape
    n = heads * d
    tiled = jnp.tile(w.reshape(n, d), (1, heads))
    same_head = (jnp.arange(n)[:, None] // d) == (jnp.arange(n)[None, :] // d)
    return jnp.where(same_head, tiled, 0.0)


def _route_plan(cnt):
    nt = cnt.shape[0]
    run_len = (cnt + (SUBLANES - 1)) // SUBLANES * SUBLANES
    region = jnp.sum(run_len, axis=0)
    region_blk = (region + FFN_BLOCK - 1) // FFN_BLOCK * FFN_BLOCK
    earlier_e = jnp.arange(N_EXPERTS)[None, :] < jnp.arange(N_EXPERTS)[:, None]
    region_start = jnp.sum(jnp.where(earlier_e, region_blk[None, :], 0), axis=1)
    region_end = region_start + region_blk
    earlier_t = jnp.arange(nt)[None, :] < jnp.arange(nt)[:, None]
    run_before = jnp.sum(jnp.where(earlier_t[:, :, None], run_len[None, :, :], 0), axis=1)
    run_dst = region_start[None, :] + run_before
    tail_dst = region_start + region
    tail_len = region_blk - region
    total_rows = jnp.sum(region_blk)
    nused = total_rows // FFN_BLOCK
    max_rows = nt * TOKEN_TILE * TOP_K + nt * N_EXPERTS * (SUBLANES - 1) + N_EXPERTS * (FFN_BLOCK - SUBLANES)
    nb = -(-max_rows // FFN_BLOCK)
    blk_start = jnp.arange(nb, dtype=I32) * FFN_BLOCK
    blk_row = jnp.minimum(blk_start, total_rows - 1)
    blk_e = jnp.sum((blk_row[:, None] >= region_end[None, :]).astype(I32), axis=1)
    blk_e = jnp.minimum(blk_e, N_EXPERTS - 1).astype(I32)
    return (run_dst.reshape(-1).astype(I32), run_len.reshape(-1).astype(I32),
            tail_dst.astype(I32), tail_len.astype(I32), blk_e,
            nused.reshape(1).astype(I32), nb * FFN_BLOCK)


def kernel(x, c, w_ada, b_ada, norm_mix_pre, norm_mix_post, w_in, conv_w, conv_b, gate_a_w, gate_a_b, gate_x_w, gate_x_b, lru_lambda, pool_w, pool_b, pool_scale, w_out, norm_ffn_pre, norm_ffn_post, router_w, router_b, expert_w1, expert_b1, expert_w2, expert_b2):
    depth = w_ada.shape[0]
    batch = x.shape[0]
    for l in range(depth):
        mod3 = _ada_call(c, w_ada[l], b_ada[l]).reshape(batch, 6, D_MODEL)

        nrm = jnp.stack([norm_mix_pre[l], norm_mix_post[l], norm_ffn_pre[l]])
        lruv = jnp.stack([conv_b[l], gate_a_b[l], gate_x_b[l], lru_lambda[l]])
        wg = jnp.concatenate([_block_diag(gate_a_w[l]), _block_diag(gate_x_w[l])], axis=1).astype(BF16)
        poolv = jnp.stack([pool_b[l], pool_scale[l]])
        wp = _block_diag(pool_w[l]).astype(BF16)
        rw_t = router_w[l].T
        rwh = rw_t.astype(BF16)
        rwl = (rw_t - rwh.astype(F32)).astype(BF16)
        rb = router_b[l].reshape(N_EXPERTS, 1)

        x1, h2, gate, pos, cnt = _mixer_call(
            x, mod3, nrm, w_in[l].astype(BF16), conv_w[l], lruv, wg, poolv, wp,
            w_out[l].astype(BF16), rwh, rwl, rb)

        run_dst, run_len, tail_dst, tail_len, blk_e, nused, cap = _route_plan(cnt[:, :, 0])
        xs = _dispatch_call(run_dst, run_len, tail_dst, tail_len, nused, h2, pos, cap)

        b1p = jnp.concatenate([expert_b1[l][:, 0::2], expert_b1[l][:, 1::2]], axis=-1)
        y = _ffn_call(blk_e, nused, xs, expert_w1[l], b1p.reshape(N_EXPERTS, 1, 2 * D_EXPERT),
                      expert_w2[l], expert_b2[l].reshape(N_EXPERTS, 1, D_MODEL))

        x = _combine_call(run_dst, run_len, y, pos, gate, x1, mod3,
                          norm_ffn_post[l].reshape(1, D_MODEL))
    return x
```

```python
import functools

import jax
import jax.numpy as jnp
from jax import lax
from jax.experimental import pallas as pl
from jax.experimental.pallas import tpu as pltpu

F32 = jnp.float32
BF16 = jnp.bfloat16
I32 = jnp.int32
U32 = jnp.uint32

D_MODEL = 1024
LRU_WIDTH = 512
LRU_HEADS = 8
POOL_WIDTH = 512
POOL_WINDOWS = (2, 4, 8, 16)
POOL_GROUP_DIM = 128
CONV_WIDTH = 4
IN_WIDTH = 2 * LRU_WIDTH + POOL_WIDTH
LRU_C = 8.0
N_EXPERTS = 32
TOP_K = 4
D_EXPERT = 1024
SWIGLU_LIMIT = 7.0
SWIGLU_ALPHA = 1.702
NORM_EPS = 1e-6

SUBLANES = 8
LANES = 128
TOKEN_TILE = 256
FFN_BLOCK = 256
CONV_CARRY = SUBLANES
POOL_CARRY = 16
SORT_ROWS = 1280
RUN_SIZES = (256, 128, 64, 32, 16, 8)
VMEM_LIMIT = 52 * 1024 * 1024


def _rms(v, g):
    return v * lax.rsqrt(jnp.mean(v * v, axis=-1, keepdims=True) + NORM_EPS) * g


def _dot(a, b):
    return jnp.dot(a, b, preferred_element_type=F32)


def _dot_nt(a, b):
    return lax.dot_general(a, b, (((1,), (1,)), ((), ())), preferred_element_type=F32)


def _split_bf16(v):
    hi = v.astype(BF16)
    lo = (v - hi.astype(F32)).astype(BF16)
    return hi, lo


def _pack_pair(a, b):
    ua = lax.bitcast_convert_type(a, U32)
    ub = lax.bitcast_convert_type(b, U32)
    return (ua >> 16) | (ub & jnp.uint32(0xFFFF0000))


def _unpack_pair(w):
    a = lax.bitcast_convert_type(w << 16, F32).astype(BF16)
    b = lax.bitcast_convert_type(w & jnp.uint32(0xFFFF0000), F32).astype(BF16)
    return a, b


def _ada_kernel(c_ref, w_ref, b_ref, o_ref):
    c = c_ref[...]
    ca = c * jax.nn.sigmoid(c)
    o_ref[...] = jnp.dot(ca, w_ref[...], preferred_element_type=F32,
                         precision=lax.Precision.HIGHEST) + b_ref[...]


def _ada_call(c, w_ada, b_ada):
    batch = c.shape[0]
    n_out = w_ada.shape[1]
    bn = D_MODEL
    return pl.pallas_call(
        _ada_kernel,
        grid=(n_out // bn,),
        in_specs=[
            pl.BlockSpec((batch, D_MODEL), lambda j: (0, 0)),
            pl.BlockSpec((D_MODEL, bn), lambda j: (0, j)),
            pl.BlockSpec((1, bn), lambda j: (0, j)),
        ],
        out_specs=pl.BlockSpec((batch, bn), lambda j: (0, j)),
        out_shape=jax.ShapeDtypeStruct((batch, n_out), F32),
        name="ada",
    )(c, w_ada, b_ada.reshape(1, n_out))


def _lru_scan(a, b, h0, out_ref):
    ts, c = a.shape
    groups = ts // SUBLANES
    a3 = a.reshape(groups, SUBLANES, c)
    b3 = b.reshape(groups, SUBLANES, c)
    row = lax.broadcasted_iota(I32, (groups, SUBLANES, c), 1)
    d = 1
    while d < SUBLANES:
        a_sh = pltpu.roll(a3, d, axis=1)
        b_sh = pltpu.roll(b3, d, axis=1)
        m = row >= d
        b3 = jnp.where(m, a3 * b_sh + b3, b3)
        a3 = jnp.where(m, a3 * a_sh, a3)
        d *= 2
    h = h0
    for g in range(groups):
        hg = a3[g] * h + b3[g]
        out_ref[pl.ds(g * SUBLANES, SUBLANES), :] = hg
        h = hg[SUBLANES - 1:SUBLANES, :]
    return h


def _mixer_kernel(x_ref, mod_ref, nrm_ref, w_in_ref, convw_ref, lruv_ref, wg_ref,
                  poolv_ref, wp_ref, w_out_ref, rwh_ref, rwl_ref, rb_ref,
                  x1_ref, h2_ref, gate_ref, pos_ref, cnt_ref,
                  xa_ext, xb_ext, hcar, hbuf):
    s = pl.program_id(1)
    ts = TOKEN_TILE

    @pl.when(s == 0)
    def _():
        xa_ext[0:CONV_CARRY, :] = jnp.zeros((CONV_CARRY, LRU_WIDTH), F32)
        xb_ext[0:POOL_CARRY, :] = jnp.zeros((POOL_CARRY, POOL_WIDTH), F32)
        hcar[...] = jnp.zeros(hcar.shape, F32)

    x = x_ref[...]
    mod = mod_ref[...]
    shift_m, scale_m, gate_m = mod[0:1], mod[1:2], mod[2:3]
    shift_f, scale_f = mod[3:4], mod[4:5]
    nrm = nrm_ref[...]

    h = _rms(x, nrm[0:1]) * (1.0 + scale_m) + shift_m
    u = _dot(h.astype(BF16), w_in_ref[...])
    xa_raw = u[:, :LRU_WIDTH]
    ga = u[:, LRU_WIDTH:2 * LRU_WIDTH]
    xb = u[:, 2 * LRU_WIDTH:]

    lruv = lruv_ref[...]
    conv_b, gate_a_b, gate_x_b, lam = lruv[0:1], lruv[1:2], lruv[2:3], lruv[3:4]
    convw = convw_ref[...]
    xa_ext[CONV_CARRY:, :] = xa_raw
    ext = xa_ext[...]
    acc = ext * convw[0:1]
    for k in range(1, CONV_WIDTH):
        acc = acc + pltpu.roll(ext, k, axis=0) * convw[k:k + 1]
    xa = acc[CONV_CARRY:, :] + conv_b
    xa_ext[0:CONV_CARRY, :] = xa_raw[ts - CONV_CARRY:, :]

    g = _dot(xa.astype(BF16), wg_ref[...])
    r = jax.nn.sigmoid(g[:, :LRU_WIDTH] + gate_a_b)
    i = jax.nn.sigmoid(g[:, LRU_WIDTH:] + gate_x_b)
    softplus_neg_lam = jnp.maximum(-lam, 0.0) + jnp.log(1.0 + jnp.exp(-jnp.abs(lam)))
    log_a = -LRU_C * r * softplus_neg_lam
    a = jnp.exp(log_a)
    mult = jnp.sqrt(1.0 - a * a)
    bterm = mult * (i * xa)
    h_last = _lru_scan(a, bterm, hcar[0:1, :], hbuf)
    hcar[0:1, :] = h_last
    ya = hbuf[...] * jax.nn.gelu(ga)

    xb_ext[POOL_CARRY:, :] = xb
    t_glob = s * ts + lax.broadcasted_iota(I32, (ts, POOL_GROUP_DIM), 0)
    pooled = []
    for gi, w in enumerate(POOL_WINDOWS):
        lo = gi * POOL_GROUP_DIM
        e = xb_ext[:, lo:lo + POOL_GROUP_DIM]
        acc_w = e
        span = 1
        while span < w:
            acc_w = acc_w + pltpu.roll(acc_w, span, axis=0)
            span *= 2
        count = jnp.minimum(t_glob + 1, w).astype(F32)
        pooled.append(acc_w[POOL_CARRY:, :] / count - xb[:, lo:lo + POOL_GROUP_DIM])
    xb_ext[0:POOL_CARRY, :] = xb[ts - POOL_CARRY:, :]
    p = jnp.concatenate(pooled, axis=-1)
    poolv = poolv_ref[...]
    yb = (_dot(p.astype(BF16), wp_ref[...]) + poolv[0:1]) * poolv[1:2]

    mix = _dot(jnp.concatenate([ya, yb], axis=-1).astype(BF16), w_out_ref[...])
    x1 = x + gate_m * _rms(mix, nrm[1:2])
    x1_ref[...] = x1

    h2 = _rms(x1, nrm[2:3]) * (1.0 + scale_f) + shift_f
    h2_hi, h2_lo = _split_bf16(h2)
    h2_ref[...] = h2_hi

    rwh = rwh_ref[...]
    logits = _dot_nt(rwh, h2_hi) + _dot_nt(rwl_ref[...], h2_hi) + _dot_nt(rwh, h2_lo)
    logits = logits + rb_ref[...]

    e_iota = lax.broadcasted_iota(I32, (N_EXPERTS, ts), 0)
    work = logits
    sels, vals = [], []
    for _ in range(TOP_K):
        m = jnp.max(work, axis=0, keepdims=True)
        idx = jnp.min(jnp.where(work == m, e_iota, N_EXPERTS), axis=0, keepdims=True)
        sel = e_iota == idx
        work = jnp.where(sel, -jnp.inf, work)
        sels.append(sel)
        vals.append(m)
    exps = [jnp.exp(v - vals[0]) for v in vals]
    denom = exps[0] + exps[1] + exps[2] + exps[3]
    gates = [ex / denom for ex in exps]

    member = jnp.zeros((N_EXPERTS, ts), F32)
    for sel in sels:
        member = jnp.where(sel, 1.0, member)
    tri = (lax.broadcasted_iota(I32, (ts, ts), 0) < lax.broadcasted_iota(I32, (ts, ts), 1))
    rank = _dot(member.astype(BF16), jnp.where(tri, 1.0, 0.0).astype(BF16))
    n_e = jnp.sum(member, axis=1, keepdims=True).astype(I32)
    run_len = ((n_e + (SUBLANES - 1)) >> 3) << 3
    ltri = (lax.broadcasted_iota(I32, (N_EXPERTS, N_EXPERTS), 1)
            < lax.broadcasted_iota(I32, (N_EXPERTS, N_EXPERTS), 0))
    run_len_b = jnp.broadcast_to(run_len.astype(F32), (N_EXPERTS, ts)).astype(BF16)
    run_off = _dot(jnp.where(ltri, 1.0, 0.0).astype(BF16), run_len_b)
    slot = rank + run_off
    pos_rows = [jnp.sum(jnp.where(sel, slot, 0.0), axis=0, keepdims=True) for sel in sels]

    zeros_f = jnp.zeros((SUBLANES - TOP_K, ts), F32)
    gate_ref[...] = jnp.concatenate(gates + [zeros_f], axis=0)
    pos_ref[...] = jnp.concatenate(pos_rows + [zeros_f], axis=0).astype(I32)
    cnt_ref[...] = jnp.broadcast_to(n_e, (N_EXPERTS, LANES))


def _mixer_call(x, mod3, nrm, w_in_b, convw, lruv, wg, poolv, wp, w_out_b, rwh, rwl, rb):
    batch, seq, _ = x.shape
    ns = seq // TOKEN_TILE
    nt = batch * ns
    tokens = batch * seq
    const = lambda shape: pl.BlockSpec(shape, lambda b, s: (0,) * len(shape))
    tile_idx = lambda b, s: (0, b * ns + s)
    return pl.pallas_call(
        _mixer_kernel,
        grid=(batch, ns),
        in_specs=[
            pl.BlockSpec((None, TOKEN_TILE, D_MODEL), lambda b, s: (b, s, 0)),
            pl.BlockSpec((None, 6, D_MODEL), lambda b, s: (b, 0, 0)),
            const(nrm.shape), const(w_in_b.shape), const(convw.shape), const(lruv.shape),
            const(wg.shape), const(poolv.shape), const(wp.shape), const(w_out_b.shape),
            const(rwh.shape), const(rwl.shape), const(rb.shape),
        ],
        out_specs=[
            pl.BlockSpec((None, TOKEN_TILE, D_MODEL), lambda b, s: (b, s, 0)),
            pl.BlockSpec((TOKEN_TILE, D_MODEL), lambda b, s: (b * ns + s, 0)),
            pl.BlockSpec((SUBLANES, TOKEN_TILE), tile_idx),
            pl.BlockSpec((SUBLANES, TOKEN_TILE), tile_idx),
            pl.BlockSpec((None, N_EXPERTS, LANES), lambda b, s: (b * ns + s, 0, 0)),
        ],
        out_shape=[
            jax.ShapeDtypeStruct((batch, seq, D_MODEL), F32),
            jax.ShapeDtypeStruct((tokens, D_MODEL), BF16),
            jax.ShapeDtypeStruct((SUBLANES, tokens), F32),
            jax.ShapeDtypeStruct((SUBLANES, tokens), I32),
            jax.ShapeDtypeStruct((nt, N_EXPERTS, LANES), I32),
        ],
        scratch_shapes=[
            pltpu.VMEM((TOKEN_TILE + CONV_CARRY, LRU_WIDTH), F32),
            pltpu.VMEM((TOKEN_TILE + POOL_CARRY, POOL_WIDTH), F32),
            pltpu.VMEM((SUBLANES, LRU_WIDTH), F32),
            pltpu.VMEM((TOKEN_TILE, LRU_WIDTH), F32),
        ],
        compiler_params=pltpu.CompilerParams(
            dimension_semantics=("arbitrary", "arbitrary"), vmem_limit_bytes=VMEM_LIMIT),
        name="mixer",
    )(x, mod3, nrm, w_in_b, convw, lruv, wg, poolv, wp, w_out_b, rwh, rwl, rb)


def _run_copies(length, src_ref, src_off, dst_ref, dst_off, sem, wait=False):
    for size in RUN_SIZES:
        hit = (length & size) != 0

        @pl.when(hit)
        def _(src_off=src_off, dst_off=dst_off, size=size):
            copy = pltpu.make_async_copy(
                src_ref.at[pl.ds(pl.multiple_of(src_off, SUBLANES), size)],
                dst_ref.at[pl.ds(pl.multiple_of(dst_off, SUBLANES), size)],
                sem)
            if wait:
                copy.wait()
            else:
                copy.start()

        step = jnp.where(hit, size, 0)
        src_off = src_off + step
        dst_off = dst_off + step
    return src_off, dst_off


def _wait_rows(rows, hbm_ref, vmem_ref, sem):
    @pl.when(rows > 0)
    def _():
        n = pl.multiple_of(rows, SUBLANES)
        pltpu.make_async_copy(hbm_ref.at[pl.ds(0, n)], vmem_ref.at[pl.ds(0, n)], sem).wait()


def _dispatch_kernel(run_dst_ref, run_len_ref, tile_rows_ref, tail_dst_ref, tail_len_ref, nused_ref,
                     h2_ref, pos_ref, xs_ref, sorted_ref, zero_ref, sem, zsem, *, nt):
    j = pl.program_id(0)
    last = nt - 1
    half = D_MODEL // 2

    @pl.when(j == 0)
    def _():
        zero_ref[...] = jnp.zeros(zero_ref.shape, U32)

        for wait in (False, True):
            def tail_body(e, carry, wait=wait):
                _run_copies(tail_len_ref[e], zero_ref, jnp.int32(0), xs_ref, tail_dst_ref[e], zsem,
                            wait=wait)
                return carry

            lax.fori_loop(0, N_EXPERTS, tail_body, 0)

            def spare_body(b, carry, wait=wait):
                copy = pltpu.make_async_copy(
                    zero_ref, xs_ref.at[pl.ds(pl.multiple_of(b * FFN_BLOCK, FFN_BLOCK), FFN_BLOCK)], zsem)
                if wait:
                    copy.wait()
                else:
                    copy.start()
                return carry

            lax.fori_loop(nused_ref[0], xs_ref.shape[0] // FFN_BLOCK, spare_body, 0)

    pos = pos_ref[...]
    r_iota = lax.broadcasted_iota(I32, (SORT_ROWS, TOKEN_TILE), 0)
    perm = jnp.zeros((SORT_ROWS, TOKEN_TILE), F32)
    for k in range(TOP_K):
        perm = jnp.where(r_iota == pos[k:k + 1, :], 1.0, perm)
    rows = _dot(perm.astype(BF16), h2_ref[...])

    slot = j % 2
    stage = sorted_ref.at[slot]

    @pl.when(j >= 2)
    def _():
        _wait_rows(tile_rows_ref[j - 2], xs_ref, stage, sem.at[slot])

    stage[...] = _pack_pair(rows[:, :half], rows[:, half:])

    def run_body(e, src_off):
        n = run_len_ref[j * N_EXPERTS + e]
        src_off, _ = _run_copies(n, stage, src_off, xs_ref, run_dst_ref[j * N_EXPERTS + e],
                                 sem.at[slot])
        return src_off

    lax.fori_loop(0, N_EXPERTS, run_body, 0)

    @pl.when(j == last)
    def _():
        _wait_rows(tile_rows_ref[j], xs_ref, stage, sem.at[slot])
        if last >= 1:
            _wait_rows(tile_rows_ref[j - 1], xs_ref, sorted_ref.at[1 - slot], sem.at[1 - slot])


def _dispatch_call(run_dst, run_len, tile_rows, tail_dst, tail_len, nused, h2, pos, cap):
    tokens = h2.shape[0]
    nt = tokens // TOKEN_TILE
    half = D_MODEL // 2
    grid_spec = pltpu.PrefetchScalarGridSpec(
        num_scalar_prefetch=6,
        grid=(nt,),
        in_specs=[
            pl.BlockSpec((TOKEN_TILE, D_MODEL), lambda j, *_: (j, 0)),
            pl.BlockSpec((SUBLANES, TOKEN_TILE), lambda j, *_: (0, j)),
        ],
        out_specs=pl.BlockSpec(memory_space=pl.ANY),
        scratch_shapes=[
            pltpu.VMEM((2, SORT_ROWS, half), U32),
            pltpu.VMEM((FFN_BLOCK, half), U32),
            pltpu.SemaphoreType.DMA((2,)),
            pltpu.SemaphoreType.DMA(()),
        ],
    )
    return pl.pallas_call(
        functools.partial(_dispatch_kernel, nt=nt),
        grid_spec=grid_spec,
        out_shape=jax.ShapeDtypeStruct((cap, half), U32),
        compiler_params=pltpu.CompilerParams(
            dimension_semantics=("arbitrary",), vmem_limit_bytes=VMEM_LIMIT),
        name="dispatch",
    )(run_dst, run_len, tile_rows, tail_dst, tail_len, nused, h2, pos)


W1_CHUNK = 256


def _ffn_kernel(blk_e_ref, nused_ref, next_e_ref, xs_ref, w1_hbm, b1_ref, w2_hbm, b2_ref, y_ref,
                w1_ref, w2_ref, w1s, w2s, wsem):
    i = pl.program_id(0)
    half = D_MODEL // 2
    prev = jnp.maximum(i - 1, 0)
    expert = blk_e_ref[i]
    new_expert = jnp.logical_or(i == 0, expert != blk_e_ref[prev])

    def fetch(e):
        return (pltpu.make_async_copy(w1_hbm.at[e], w1_ref, wsem.at[0]),
                pltpu.make_async_copy(w2_hbm.at[e], w2_ref, wsem.at[1]))

    @pl.when(i == 0)
    def _():
        for copy in fetch(expert):
            copy.start()

    @pl.when(jnp.logical_and(i < nused_ref[0], new_expert))
    def _():
        for copy in fetch(expert):
            copy.wait()
        hc = W1_CHUNK // 2
        r = lax.broadcasted_iota(I32, (W1_CHUNK, W1_CHUNK), 0)
        q = lax.broadcasted_iota(I32, (W1_CHUNK, W1_CHUNK), 1)
        src_col = jnp.where(q < hc, 2 * q, 2 * (q - hc) + 1)
        sel = jnp.where(r == src_col, 1.0, 0.0).astype(BF16)
        for c in range(2 * D_EXPERT // W1_CHUNK):
            chunk = w1_ref[:, c * W1_CHUNK:(c + 1) * W1_CHUNK].astype(BF16)
            picked = _dot(chunk, sel).astype(BF16)
            w1s[:, c * hc:(c + 1) * hc] = picked[:, :hc]
            w1s[:, D_EXPERT + c * hc:D_EXPERT + (c + 1) * hc] = picked[:, hc:]
        w2s[...] = w2_ref[...].astype(BF16)

        nxt = next_e_ref[expert]

        @pl.when(nxt >= 0)
        def _():
            for copy in fetch(nxt):
                copy.start()

    @pl.when(i < nused_ref[0])
    def _():
        x_a, x_b = _unpack_pair(xs_ref[...])
        u = _dot(x_a, w1s[0:half, :]) + _dot(x_b, w1s[half:, :]) + b1_ref[...]
        glu = jnp.minimum(u[:, :D_EXPERT], SWIGLU_LIMIT)
        lin = jnp.clip(u[:, D_EXPERT:], -SWIGLU_LIMIT, SWIGLU_LIMIT)
        act = glu * jax.nn.sigmoid(SWIGLU_ALPHA * glu) * (lin + 1.0)
        y = _dot(act.astype(BF16), w2s[...]) + b2_ref[...]
        yb = y.astype(BF16).astype(F32)
        y_ref[...] = _pack_pair(yb[:, :half], yb[:, half:])

    @pl.when(i >= nused_ref[0])
    def _():
        y_ref[...] = jnp.zeros(y_ref.shape, U32)


def _ffn_call(blk_e, nused, next_e, xs, w1, b1p, w2, b2):
    cap, half = xs.shape
    nb = cap // FFN_BLOCK
    row_blk = lambda i, be, nu, ne: (jnp.minimum(i, nu[0] - 1), 0)
    grid_spec = pltpu.PrefetchScalarGridSpec(
        num_scalar_prefetch=3,
        grid=(nb,),
        in_specs=[
            pl.BlockSpec((FFN_BLOCK, half), row_blk),
            pl.BlockSpec(memory_space=pl.ANY),
            pl.BlockSpec((None, 1, 2 * D_EXPERT), lambda i, be, nu, ne: (be[i], 0, 0)),
            pl.BlockSpec(memory_space=pl.ANY),
            pl.BlockSpec((None, 1, D_MODEL), lambda i, be, nu, ne: (be[i], 0, 0)),
        ],
        out_specs=pl.BlockSpec((FFN_BLOCK, half), lambda i, be, nu, ne: (i, 0)),
        scratch_shapes=[
            pltpu.VMEM((D_MODEL, 2 * D_EXPERT), F32),
            pltpu.VMEM((D_EXPERT, D_MODEL), F32),
            pltpu.VMEM((D_MODEL, 2 * D_EXPERT), BF16),
            pltpu.VMEM((D_EXPERT, D_MODEL), BF16),
            pltpu.SemaphoreType.DMA((2,)),
        ],
    )
    return pl.pallas_call(
        _ffn_kernel,
        grid_spec=grid_spec,
        out_shape=jax.ShapeDtypeStruct((cap, half), U32),
        compiler_params=pltpu.CompilerParams(
            dimension_semantics=("arbitrary",), vmem_limit_bytes=VMEM_LIMIT),
        name="ffn",
    )(blk_e, nused, next_e, xs, w1, b1p, w2, b2)


def _combine_kernel(run_dst_ref, run_len_ref, tile_rows_ref, y_ref, pos_ref, gate_ref, x1_ref, mod_ref,
                    nrm_ref, out_ref, ybuf, sem, *, nt):
    j = pl.program_id(0)
    slot = j % 2

    def fetch_tile(t, s):
        def run_body(e, dst_off):
            n = run_len_ref[t * N_EXPERTS + e]
            _, dst_off = _run_copies(n, y_ref, run_dst_ref[t * N_EXPERTS + e], ybuf.at[s], dst_off,
                                     sem.at[s])
            return dst_off

        lax.fori_loop(0, N_EXPERTS, run_body, 0)

    @pl.when(j == 0)
    def _():
        ybuf[...] = jnp.zeros(ybuf.shape, U32)
        fetch_tile(j, slot)

    @pl.when(j + 1 < nt)
    def _():
        fetch_tile(j + 1, 1 - slot)

    pos_t = jnp.transpose(pos_ref[...].astype(F32))
    gate_t = jnp.transpose(gate_ref[...])
    c_iota = lax.broadcasted_iota(I32, (TOKEN_TILE, SORT_ROWS), 1).astype(F32)
    wsel = jnp.zeros((TOKEN_TILE, SORT_ROWS), F32)
    for k in range(TOP_K):
        wsel = jnp.where(c_iota == pos_t[:, k:k + 1], gate_t[:, k:k + 1], wsel)
    w_hi, w_lo = _split_bf16(wsel)

    _wait_rows(tile_rows_ref[j], y_ref, ybuf.at[slot], sem.at[slot])
    y_a, y_b = _unpack_pair(ybuf[slot])
    f = jnp.concatenate([_dot(w_hi, y_a) + _dot(w_lo, y_a), _dot(w_hi, y_b) + _dot(w_lo, y_b)],
                        axis=-1)
    gate_f = mod_ref[5:6, :]
    out_ref[...] = x1_ref[...] + gate_f * _rms(f, nrm_ref[...])


def _combine_call(run_dst, run_len, tile_rows, y, pos, gate, x1, mod3, nrm_post):
    batch, seq, _ = x1.shape
    ns = seq // TOKEN_TILE
    nt = batch * ns
    half = D_MODEL // 2
    grid_spec = pltpu.PrefetchScalarGridSpec(
        num_scalar_prefetch=3,
        grid=(nt,),
        in_specs=[
            pl.BlockSpec(memory_space=pl.ANY),
            pl.BlockSpec((SUBLANES, TOKEN_TILE), lambda j, *_: (0, j)),
            pl.BlockSpec((SUBLANES, TOKEN_TILE), lambda j, *_: (0, j)),
            pl.BlockSpec((None, TOKEN_TILE, D_MODEL), lambda j, *_: (j // ns, j % ns, 0)),
            pl.BlockSpec((None, 6, D_MODEL), lambda j, *_: (j // ns, 0, 0)),
            pl.BlockSpec((1, D_MODEL), lambda j, *_: (0, 0)),
        ],
        out_specs=pl.BlockSpec((None, TOKEN_TILE, D_MODEL), lambda j, *_: (j // ns, j % ns, 0)),
        scratch_shapes=[
            pltpu.VMEM((2, SORT_ROWS, half), U32),
            pltpu.SemaphoreType.DMA((2,)),
        ],
    )
    return pl.pallas_call(
        functools.partial(_combine_kernel, nt=nt),
        grid_spec=grid_spec,
        out_shape=jax.ShapeDtypeStruct((batch, seq, D_MODEL), F32),
        compiler_params=pltpu.CompilerParams(
            dimension_semantics=("arbitrary",), vmem_limit_bytes=VMEM_LIMIT),
        name="combine",
    )(run_dst, run_len, tile_rows, y, pos, gate, x1, mod3, nrm_post)


def _block_diag(w):
    heads, d, _ = w.shape
    n = heads * d
    tiled = jnp.tile(w.reshape(n, d), (1, heads))
    same_head = (jnp.arange(n)[:, None] // d) == (jnp.arange(n)[None, :] // d)
    return jnp.where(same_head, tiled, 0.0)


def _route_plan(cnt):
    nt = cnt.shape[0]
    run_len = (cnt + (SUBLANES - 1)) // SUBLANES * SUBLANES
    region = jnp.sum(run_len, axis=0)
    region_blk = (region + FFN_BLOCK - 1) // FFN_BLOCK * FFN_BLOCK
    earlier_e = jnp.arange(N_EXPERTS)[None, :] < jnp.arange(N_EXPERTS)[:, None]
    region_start = jnp.sum(jnp.where(earlier_e, region_blk[None, :], 0), axis=1)
    region_end = region_start + region_blk
    earlier_t = jnp.arange(nt)[None, :] < jnp.arange(nt)[:, None]
    run_before = jnp.sum(jnp.where(earlier_t[:, :, None], run_len[None, :, :], 0), axis=1)
    run_dst = region_start[None, :] + run_before
    tail_dst = region_start + region
    tail_len = region_blk - region
    total_rows = jnp.sum(region_blk)
    nused = total_rows // FFN_BLOCK
    max_rows = nt * TOKEN_TILE * TOP_K + nt * N_EXPERTS * (SUBLANES - 1) + N_EXPERTS * (FFN_BLOCK - SUBLANES)
    nb = -(-max_rows // FFN_BLOCK)
    blk_start = jnp.arange(nb, dtype=I32) * FFN_BLOCK
    blk_row = jnp.minimum(blk_start, total_rows - 1)
    blk_e = jnp.sum((blk_row[:, None] >= region_end[None, :]).astype(I32), axis=1)
    blk_e = jnp.minimum(blk_e, N_EXPERTS - 1).astype(I32)
    later_used = jnp.logical_and(jnp.arange(N_EXPERTS)[None, :] > jnp.arange(N_EXPERTS)[:, None],
                                 region_blk[None, :] > 0)
    next_e = jnp.min(jnp.where(later_used, jnp.arange(N_EXPERTS)[None, :], N_EXPERTS), axis=1)
    next_e = jnp.where(next_e == N_EXPERTS, -1, next_e)
    tile_rows = jnp.sum(run_len, axis=1)
    return dict(run_dst=run_dst.reshape(-1).astype(I32), run_len=run_len.reshape(-1).astype(I32),
                tile_rows=tile_rows.astype(I32), tail_dst=tail_dst.astype(I32),
                tail_len=tail_len.astype(I32), blk_e=blk_e, next_e=next_e.astype(I32),
                nused=nused.reshape(1).astype(I32)), nb * FFN_BLOCK


def kernel(x, c, w_ada, b_ada, norm_mix_pre, norm_mix_post, w_in, conv_w, conv_b, gate_a_w, gate_a_b, gate_x_w, gate_x_b, lru_lambda, pool_w, pool_b, pool_scale, w_out, norm_ffn_pre, norm_ffn_post, router_w, router_b, expert_w1, expert_b1, expert_w2, expert_b2):
    depth = w_ada.shape[0]
    batch = x.shape[0]
    for l in range(depth):
        mod3 = _ada_call(c, w_ada[l], b_ada[l]).reshape(batch, 6, D_MODEL)

        nrm = jnp.stack([norm_mix_pre[l], norm_mix_post[l], norm_ffn_pre[l]])
        lruv = jnp.stack([conv_b[l], gate_a_b[l], gate_x_b[l], lru_lambda[l]])
        wg = jnp.concatenate([_block_diag(gate_a_w[l]), _block_diag(gate_x_w[l])], axis=1).astype(BF16)
        poolv = jnp.stack([pool_b[l], pool_scale[l]])
        wp = _block_diag(pool_w[l]).astype(BF16)
        rw_t = router_w[l].T
        rwh = rw_t.astype(BF16)
        rwl = (rw_t - rwh.astype(F32)).astype(BF16)
        rb = router_b[l].reshape(N_EXPERTS, 1)

        x1, h2, gate, pos, cnt = _mixer_call(
            x, mod3, nrm, w_in[l].astype(BF16), conv_w[l], lruv, wg, poolv, wp,
            w_out[l].astype(BF16), rwh, rwl, rb)

        plan, cap = _route_plan(cnt[:, :, 0])
        xs = _dispatch_call(plan["run_dst"], plan["run_len"], plan["tile_rows"], plan["tail_dst"],
                            plan["tail_len"], plan["nused"], h2, pos, cap)

        b1p = jnp.concatenate([expert_b1[l][:, 0::2], expert_b1[l][:, 1::2]], axis=-1)
        y = _ffn_call(plan["blk_e"], plan["nused"], plan["next_e"], xs, expert_w1[l],
                      b1p.reshape(N_EXPERTS, 1, 2 * D_EXPERT),
                      expert_w2[l], expert_b2[l].reshape(N_EXPERTS, 1, D_MODEL))

        x = _combine_call(plan["run_dst"], plan["run_len"], plan["tile_rows"], y, pos, gate, x1, mod3,
                          norm_ffn_post[l].reshape(1, D_MODEL))
    return x
```

```python
import functools

import jax
import jax.numpy as jnp
from jax import lax
from jax.experimental import pallas as pl
from jax.experimental.pallas import tpu as pltpu

F32 = jnp.float32
BF16 = jnp.bfloat16
I32 = jnp.int32
U32 = jnp.uint32

D_MODEL = 1024
LRU_WIDTH = 512
LRU_HEADS = 8
POOL_WIDTH = 512
POOL_WINDOWS = (2, 4, 8, 16)
POOL_GROUP_DIM = 128
CONV_WIDTH = 4
IN_WIDTH = 2 * LRU_WIDTH + POOL_WIDTH
LRU_C = 8.0
N_EXPERTS = 32
TOP_K = 4
D_EXPERT = 1024
SWIGLU_LIMIT = 7.0
SWIGLU_ALPHA = 1.702
NORM_EPS = 1e-6

SUBLANES = 8
LANES = 128
TOKEN_TILE = 256
FFN_BLOCK = 256
MIXER_BATCH_GROUP = 2
MIXER_PHASES = 6
CONV_CARRY = SUBLANES
POOL_CARRY = 16
SORT_ROWS = 1280
RUN_SIZES = (256, 128, 64, 32, 16, 8)
UNPACK_ROWS = 64
VMEM_LIMIT = 52 * 1024 * 1024


def _rms(v, g):
    return v * lax.rsqrt(jnp.mean(v * v, axis=-1, keepdims=True) + NORM_EPS) * g


def _dot(a, b):
    return jnp.dot(a, b, preferred_element_type=F32)


def _dot_nt(a, b):
    return lax.dot_general(a, b, (((1,), (1,)), ((), ())), preferred_element_type=F32)


def _split_bf16(v):
    hi = v.astype(BF16)
    lo = (v - hi.astype(F32)).astype(BF16)
    return hi, lo


def _pack_pair(a, b):
    ua = lax.bitcast_convert_type(a, U32)
    ub = lax.bitcast_convert_type(b, U32)
    return (ua >> 16) | (ub & jnp.uint32(0xFFFF0000))


def _unpack_pair(w):
    a = lax.bitcast_convert_type(w << 16, F32).astype(BF16)
    b = lax.bitcast_convert_type(w & jnp.uint32(0xFFFF0000), F32).astype(BF16)
    return a, b


def _ada_kernel(c_ref, w_ref, b_ref, o_ref):
    c = c_ref[...]
    ca = c * jax.nn.sigmoid(c)
    o_ref[...] = jnp.dot(ca, w_ref[...], preferred_element_type=F32,
                         precision=lax.Precision.HIGHEST) + b_ref[...]


def _ada_call(c, w_ada, b_ada):
    batch = c.shape[0]
    n_out = w_ada.shape[1]
    bn = D_MODEL
    return pl.pallas_call(
        _ada_kernel,
        grid=(n_out // bn,),
        in_specs=[
            pl.BlockSpec((batch, D_MODEL), lambda j: (0, 0)),
            pl.BlockSpec((D_MODEL, bn), lambda j: (0, j)),
            pl.BlockSpec((1, bn), lambda j: (0, j)),
        ],
        out_specs=pl.BlockSpec((batch, bn), lambda j: (0, j)),
        out_shape=jax.ShapeDtypeStruct((batch, n_out), F32),
        name="ada",
    )(c, w_ada, b_ada.reshape(1, n_out))


def _lru_scan(a, b, h0, out_ref):
    ts, c = a.shape
    groups = ts // SUBLANES
    a3 = a.reshape(groups, SUBLANES, c)
    b3 = b.reshape(groups, SUBLANES, c)
    row = lax.broadcasted_iota(I32, (groups, SUBLANES, c), 1)
    d = 1
    while d < SUBLANES:
        a_sh = pltpu.roll(a3, d, axis=1)
        b_sh = pltpu.roll(b3, d, axis=1)
        m = row >= d
        b3 = jnp.where(m, a3 * b_sh + b3, b3)
        a3 = jnp.where(m, a3 * a_sh, a3)
        d *= 2
    h = h0
    for g in range(groups):
        hg = a3[g] * h + b3[g]
        out_ref[pl.ds(g * SUBLANES, SUBLANES), :] = hg
        h = hg[SUBLANES - 1:SUBLANES, :]
    return h


def _mixer_kernel(x_ref, mod_ref, nrm_ref, w_in_ref, convw_ref, lruv_ref, wg_ref,
                  poolv_ref, wp_ref, w_out_ref, rwh_ref, rb_ref, rankw_ref,
                  x1_ref, h2_ref, gate_ref, pos_ref, cnt_ref,
                  xa_ext, xb_ext, hcar, hbuf):
    s = pl.program_id(1)
    shared = (nrm_ref, w_in_ref, convw_ref, lruv_ref, wg_ref, poolv_ref, wp_ref, w_out_ref,
              rwh_ref, rb_ref, rankw_ref)
    per_row = (x_ref, mod_ref, x1_ref, h2_ref, gate_ref, pos_ref, cnt_ref, xa_ext, xb_ext, hcar, hbuf)

    @pl.when(s == 0)
    def _():
        xa_ext[:, 0:CONV_CARRY, :] = jnp.zeros((xa_ext.shape[0], CONV_CARRY, LRU_WIDTH), F32)
        xb_ext[:, 0:POOL_CARRY, :] = jnp.zeros((xb_ext.shape[0], POOL_CARRY, POOL_WIDTH), F32)
        hcar[...] = jnp.zeros(hcar.shape, F32)

    tiles = [_mixer_tile(s, shared, *(ref.at[p] for ref in per_row)) for p in range(x_ref.shape[0])]
    for step in range(MIXER_PHASES - 1 + len(tiles) - 1):
        for p, tile in enumerate(tiles):
            if 0 <= step - p < MIXER_PHASES - 1:
                next(tile)
    for tile in tiles:
        assert next(tile, "done") == "done"


def _mixer_tile(s, shared, x_ref, mod_ref, x1_ref, h2_ref, gate_ref, pos_ref, cnt_ref,
                xa_ext, xb_ext, hcar, hbuf):
    (nrm_ref, w_in_ref, convw_ref, lruv_ref, wg_ref, poolv_ref, wp_ref, w_out_ref,
     rwh_ref, rb_ref, rankw_ref) = shared
    ts = TOKEN_TILE
    x = x_ref[...]
    mod = mod_ref[...]
    shift_m, scale_m, gate_m = mod[0:1], mod[1:2], mod[2:3]
    shift_f, scale_f = mod[3:4], mod[4:5]
    nrm = nrm_ref[...]

    h = _rms(x, nrm[0:1]) * (1.0 + scale_m) + shift_m
    u = _dot(h.astype(BF16), w_in_ref[...])
    yield
    xa_raw = u[:, :LRU_WIDTH]
    ga = u[:, LRU_WIDTH:2 * LRU_WIDTH]
    xb = u[:, 2 * LRU_WIDTH:]

    lruv = lruv_ref[...]
    conv_b, gate_a_b, gate_x_b, lam = lruv[0:1], lruv[1:2], lruv[2:3], lruv[3:4]
    convw = convw_ref[...]
    xa_ext[CONV_CARRY:, :] = xa_raw
    ext = xa_ext[...]
    acc = ext * convw[0:1]
    for k in range(1, CONV_WIDTH):
        acc = acc + pltpu.roll(ext, k, axis=0) * convw[k:k + 1]
    xa = acc[CONV_CARRY:, :] + conv_b
    xa_ext[0:CONV_CARRY, :] = xa_raw[ts - CONV_CARRY:, :]

    g = _dot(xa.astype(BF16), wg_ref[...])
    yield
    r = jax.nn.sigmoid(g[:, :LRU_WIDTH] + gate_a_b)
    i = jax.nn.sigmoid(g[:, LRU_WIDTH:] + gate_x_b)
    softplus_neg_lam = jnp.maximum(-lam, 0.0) + jnp.log(1.0 + jnp.exp(-jnp.abs(lam)))
    log_a = -LRU_C * r * softplus_neg_lam
    a = jnp.exp(log_a)
    mult = jnp.sqrt(1.0 - a * a)
    bterm = mult * (i * xa)
    h_last = _lru_scan(a, bterm, hcar[0:1, :], hbuf)
    hcar[0:1, :] = h_last
    ya = hbuf[...] * jax.nn.gelu(ga)
    yield

    xb_ext[POOL_CARRY:, :] = xb
    t_glob = s * ts + lax.broadcasted_iota(I32, (ts, POOL_GROUP_DIM), 0)
    pooled = []
    for gi, w in enumerate(POOL_WINDOWS):
        lo = gi * POOL_GROUP_DIM
        e = xb_ext[:, lo:lo + POOL_GROUP_DIM]
        acc_w = e
        span = 1
        while span < w:
            acc_w = acc_w + pltpu.roll(acc_w, span, axis=0)
            span *= 2
        count = jnp.minimum(t_glob + 1, w).astype(F32)
        pooled.append(acc_w[POOL_CARRY:, :] / count - xb[:, lo:lo + POOL_GROUP_DIM])
    xb_ext[0:POOL_CARRY, :] = xb[ts - POOL_CARRY:, :]
    p = jnp.concatenate(pooled, axis=-1)
    poolv = poolv_ref[...]
    yb = (_dot(p.astype(BF16), wp_ref[...]) + poolv[0:1]) * poolv[1:2]
    yield

    mix =_dot(jnp.concatenate([ya, yb], axis=-1).astype(BF16), w_out_ref[...])
    x1 = x + gate_m * _rms(mix, nrm[1:2])
    x1_ref[...] = x1

    h2 = _rms(x1, nrm[2:3]) * (1.0 + scale_f) + shift_f
    h2_hi, h2_lo = _split_bf16(h2)
    h2_ref[...] = h2_hi
    yield

    hi_terms = _dot(h2_hi, rwh_ref[...])
    lo_term = _dot(h2_lo, rwh_ref[:, 0:LANES])
    logits_t = hi_terms[:, :LANES] + hi_terms[:, LANES:] + lo_term
    logits = jnp.transpose(logits_t)[0:N_EXPERTS, :] + rb_ref[...]

    e_iota = lax.broadcasted_iota(I32, (N_EXPERTS, ts), 0)
    beaten = jnp.zeros((N_EXPERTS, ts), F32)
    for e2 in range(N_EXPERTS):
        other = logits[e2:e2 + 1, :]
        ahead_strict = jnp.where(other > logits, 1.0, 0.0)
        ahead_or_tie = jnp.where(other >= logits, 1.0, 0.0)
        beaten = beaten + jnp.where(e_iota > e2, ahead_or_tie, ahead_strict)
    sels = [beaten == float(k) for k in range(TOP_K)]
    vals = [jnp.sum(jnp.where(sel, logits, 0.0), axis=0, keepdims=True) for sel in sels]
    exps = [jnp.exp(v - vals[0]) for v in vals]
    denom = exps[0] + exps[1] + exps[2] + exps[3]
    gates = [ex / denom for ex in exps]

    member = jnp.where(beaten < float(TOP_K), 1.0, 0.0)
    rank_cnt = _dot(member.astype(BF16), rankw_ref[...])
    rank, n_e = rank_cnt[:, :ts], rank_cnt[:, ts:].astype(I32)
    run_len = ((n_e + (SUBLANES - 1)) >> 3) << 3
    ltri = (lax.broadcasted_iota(I32, (N_EXPERTS, N_EXPERTS), 1)
            < lax.broadcasted_iota(I32, (N_EXPERTS, N_EXPERTS), 0))
    run_off = _dot(jnp.where(ltri, 1.0, 0.0).astype(BF16), run_len.astype(F32).astype(BF16))
    slot = rank + run_off
    pos_rows = [jnp.sum(jnp.where(sel, slot, 0.0), axis=0, keepdims=True) for sel in sels]

    zeros_f = jnp.zeros((SUBLANES - TOP_K, ts), F32)
    gate_ref[...] = jnp.concatenate(gates + [zeros_f], axis=0)
    pos_ref[...] = jnp.concatenate(pos_rows + [zeros_f], axis=0).astype(I32)
    cnt_ref[...] = n_e[:, :LANES]


def _mixer_call(x, mod3, nrm, w_in_b, convw, lruv, wg, poolv, wp, w_out_b, rwh, rb):
    batch, seq, _ = x.shape
    ns = seq // TOKEN_TILE
    t_row = jnp.arange(TOKEN_TILE)[:, None]
    t_col = jnp.arange(2 * TOKEN_TILE)[None, :]
    rankw = jnp.where(t_col >= TOKEN_TILE, 1.0, jnp.where(t_row < t_col, 1.0, 0.0)).astype(BF16)
    grp = MIXER_BATCH_GROUP if batch % MIXER_BATCH_GROUP == 0 else 1
    const = lambda shape: pl.BlockSpec(shape, lambda b, s: (0,) * len(shape))
    return pl.pallas_call(
        _mixer_kernel,
        grid=(batch // grp, ns),
        in_specs=[
            pl.BlockSpec((grp, TOKEN_TILE, D_MODEL), lambda b, s: (b, s, 0)),
            pl.BlockSpec((grp, 6, D_MODEL), lambda b, s: (b, 0, 0)),
            const(nrm.shape), const(w_in_b.shape), const(convw.shape), const(lruv.shape),
            const(wg.shape), const(poolv.shape), const(wp.shape), const(w_out_b.shape),
            const(rwh.shape), const(rb.shape), const(rankw.shape),
        ],
        out_specs=[
            pl.BlockSpec((grp, TOKEN_TILE, D_MODEL), lambda b, s: (b, s, 0)),
            pl.BlockSpec((grp, TOKEN_TILE, D_MODEL), lambda b, s: (b, s, 0)),
            pl.BlockSpec((grp, SUBLANES, TOKEN_TILE), lambda b, s: (b, 0, s)),
            pl.BlockSpec((grp, SUBLANES, TOKEN_TILE), lambda b, s: (b, 0, s)),
            pl.BlockSpec((grp, None, N_EXPERTS, LANES), lambda b, s: (b, s, 0, 0)),
        ],
        out_shape=[
            jax.ShapeDtypeStruct((batch, seq, D_MODEL), F32),
            jax.ShapeDtypeStruct((batch, seq, D_MODEL), BF16),
            jax.ShapeDtypeStruct((batch, SUBLANES, seq), F32),
            jax.ShapeDtypeStruct((batch, SUBLANES, seq), I32),
            jax.ShapeDtypeStruct((batch, ns, N_EXPERTS, LANES), I32),
        ],
        scratch_shapes=[
            pltpu.VMEM((grp, TOKEN_TILE + CONV_CARRY, LRU_WIDTH), F32),
            pltpu.VMEM((grp, TOKEN_TILE + POOL_CARRY, POOL_WIDTH), F32),
            pltpu.VMEM((grp, SUBLANES, LRU_WIDTH), F32),
            pltpu.VMEM((grp, TOKEN_TILE, LRU_WIDTH), F32),
        ],
        compiler_params=pltpu.CompilerParams(
            dimension_semantics=("arbitrary", "arbitrary"), vmem_limit_bytes=VMEM_LIMIT),
        name="mixer",
    )(x, mod3, nrm, w_in_b, convw, lruv, wg, poolv, wp, w_out_b, rwh, rb, rankw)


def _run_copies(length, src_ref, src_off, dst_ref, dst_off, sem, wait=False):
    for size in RUN_SIZES:
        hit = (length & size) != 0

        @pl.when(hit)
        def _(src_off=src_off, dst_off=dst_off, size=size):
            copy = pltpu.make_async_copy(
                src_ref.at[pl.ds(pl.multiple_of(src_off, SUBLANES), size)],
                dst_ref.at[pl.ds(pl.multiple_of(dst_off, SUBLANES), size)],
                sem)
            if wait:
                copy.wait()
            else:
                copy.start()

        step = jnp.where(hit, size, 0)
        src_off = src_off + step
        dst_off = dst_off + step
    return src_off, dst_off


def _wait_rows(rows, hbm_ref, vmem_ref, sem):
    @pl.when(rows > 0)
    def _():
        n = pl.multiple_of(rows, SUBLANES)
        pltpu.make_async_copy(hbm_ref.at[pl.ds(0, n)], vmem_ref.at[pl.ds(0, n)], sem).wait()


def _dispatch_kernel(run_dst_ref, run_len_ref, tile_rows_ref, tail_dst_ref, tail_len_ref, nused_ref,
                     h2_ref, pos_ref, xs_ref, sorted_ref, zero_ref, perm_ref, sem, zsem, *, nt):
    j = pl.program_id(0)
    last = nt - 1
    half = D_MODEL // 2

    @pl.when(j == 0)
    def _():
        zero_ref[...] = jnp.zeros(zero_ref.shape, U32)

        for wait in (False, True):
            def tail_body(e, carry, wait=wait):
                _run_copies(tail_len_ref[e], zero_ref, jnp.int32(0), xs_ref, tail_dst_ref[e], zsem,
                            wait=wait)
                return carry

            lax.fori_loop(0, N_EXPERTS, tail_body, 0)

            def spare_body(b, carry, wait=wait):
                copy = pltpu.make_async_copy(
                    zero_ref, xs_ref.at[pl.ds(pl.multiple_of(b * FFN_BLOCK, FFN_BLOCK), FFN_BLOCK)], zsem)
                if wait:
                    copy.wait()
                else:
                    copy.start()
                return carry

            lax.fori_loop(nused_ref[0], xs_ref.shape[0] // FFN_BLOCK, spare_body, 0)

    pos = pos_ref[...]
    for r0 in range(0, SORT_ROWS, LANES):
        r_iota = lax.broadcasted_iota(I32, (LANES, TOKEN_TILE), 0) + r0
        perm = jnp.zeros((LANES, TOKEN_TILE), F32)
        for k in range(TOP_K):
            perm = jnp.where(r_iota == pos[k:k + 1, :], 1.0, perm)
        perm_ref[r0:r0 + LANES, :] = perm.astype(BF16)
    rows = _dot(perm_ref[...], h2_ref[...])

    slot = j % 2
    stage = sorted_ref.at[slot]

    @pl.when(j >= 2)
    def _():
        _wait_rows(tile_rows_ref[j - 2], xs_ref, stage, sem.at[slot])

    stage[...] = _pack_pair(rows[:, :half], rows[:, half:])

    def run_body(e, src_off):
        n = run_len_ref[j * N_EXPERTS + e]
        src_off, _ = _run_copies(n, stage, src_off, xs_ref, run_dst_ref[j * N_EXPERTS + e],
                                 sem.at[slot])
        return src_off

    lax.fori_loop(0, N_EXPERTS, run_body, 0)

    @pl.when(j == last)
    def _():
        _wait_rows(tile_rows_ref[j], xs_ref, stage, sem.at[slot])
        if last >= 1:
            _wait_rows(tile_rows_ref[j - 1], xs_ref, sorted_ref.at[1 - slot], sem.at[1 - slot])


def _dispatch_call(run_dst, run_len, tile_rows, tail_dst, tail_len, nused, h2, pos, cap):
    tokens = h2.shape[0]
    nt = tokens // TOKEN_TILE
    ns = pos.shape[2] // TOKEN_TILE
    half = D_MODEL // 2
    grid_spec = pltpu.PrefetchScalarGridSpec(
        num_scalar_prefetch=6,
        grid=(nt,),
        in_specs=[
            pl.BlockSpec((TOKEN_TILE, D_MODEL), lambda j, *_: (j, 0)),
            pl.BlockSpec((None, SUBLANES, TOKEN_TILE), lambda j, *_: (j // ns, 0, j % ns)),
        ],
        out_specs=pl.BlockSpec(memory_space=pl.ANY),
        scratch_shapes=[
            pltpu.VMEM((2, SORT_ROWS, half), U32),
            pltpu.VMEM((FFN_BLOCK, half), U32),
            pltpu.VMEM((SORT_ROWS, TOKEN_TILE), BF16),
            pltpu.SemaphoreType.DMA((2,)),
            pltpu.SemaphoreType.DMA(()),
        ],
    )
    return pl.pallas_call(
        functools.partial(_dispatch_kernel, nt=nt),
        grid_spec=grid_spec,
        out_shape=jax.ShapeDtypeStruct((cap, half), U32),
        compiler_params=pltpu.CompilerParams(
            dimension_semantics=("arbitrary",), vmem_limit_bytes=VMEM_LIMIT),
        name="dispatch",
    )(run_dst, run_len, tile_rows, tail_dst, tail_len, nused, h2, pos)


W1_CHUNK = 256


def _ffn_kernel(blk_e_ref, nused_ref, next_e_ref, xs_ref, w1_hbm, b1_ref, w2_hbm, b2_ref, y_ref,
                w1_ref, w2_ref, w1s, w2s, wsem):
    i = pl.program_id(0)
    half = D_MODEL // 2
    prev = jnp.maximum(i - 1, 0)
    expert = blk_e_ref[i]
    new_expert = jnp.logical_or(i == 0, expert != blk_e_ref[prev])

    def fetch(e):
        return (pltpu.make_async_copy(w1_hbm.at[e], w1_ref, wsem.at[0]),
                pltpu.make_async_copy(w2_hbm.at[e], w2_ref, wsem.at[1]))

    @pl.when(i == 0)
    def _():
        for copy in fetch(expert):
            copy.start()

    @pl.when(jnp.logical_and(i < nused_ref[0], new_expert))
    def _():
        for copy in fetch(expert):
            copy.wait()
        hc = W1_CHUNK // 2
        r = lax.broadcasted_iota(I32, (W1_CHUNK, W1_CHUNK), 0)
        q = lax.broadcasted_iota(I32, (W1_CHUNK, W1_CHUNK), 1)
        src_col = jnp.where(q < hc, 2 * q, 2 * (q - hc) + 1)
        sel = jnp.where(r == src_col, 1.0, 0.0).astype(BF16)
        for c in range(2 * D_EXPERT // W1_CHUNK):
            chunk = w1_ref[:, c * W1_CHUNK:(c + 1) * W1_CHUNK].astype(BF16)
            picked = _dot(chunk, sel).astype(BF16)
            w1s[:, c * hc:(c + 1) * hc] = picked[:, :hc]
            w1s[:, D_EXPERT + c * hc:D_EXPERT + (c + 1) * hc] = picked[:, hc:]
        w2s[...] = w2_ref[...].astype(BF16)

        nxt = next_e_ref[expert]

        @pl.when(nxt >= 0)
        def _():
            for copy in fetch(nxt):
                copy.start()

    @pl.when(i < nused_ref[0])
    def _():
        x_a, x_b = _unpack_pair(xs_ref[...])
        u = _dot(x_a, w1s[0:half, :]) + _dot(x_b, w1s[half:, :]) + b1_ref[...]
        glu = jnp.minimum(u[:, :D_EXPERT], SWIGLU_LIMIT)
        lin = jnp.clip(u[:, D_EXPERT:], -SWIGLU_LIMIT, SWIGLU_LIMIT)
        act = glu * jax.nn.sigmoid(SWIGLU_ALPHA * glu) * (lin + 1.0)
        y = _dot(act.astype(BF16), w2s[...]) + b2_ref[...]
        yb = y.astype(BF16).astype(F32)
        y_ref[...] = _pack_pair(yb[:, :half], yb[:, half:])

    @pl.when(i >= nused_ref[0])
    def _():
        y_ref[...] = jnp.zeros(y_ref.shape, U32)


def _ffn_call(blk_e, nused, next_e, xs, w1, b1p, w2, b2):
    cap, half = xs.shape
    nb = cap // FFN_BLOCK
    row_blk = lambda i, be, nu, ne: (jnp.minimum(i, nu[0] - 1), 0)
    grid_spec = pltpu.PrefetchScalarGridSpec(
        num_scalar_prefetch=3,
        grid=(nb,),
        in_specs=[
            pl.BlockSpec((FFN_BLOCK, half), row_blk),
            pl.BlockSpec(memory_space=pl.ANY),
            pl.BlockSpec((None, 1, 2 * D_EXPERT), lambda i, be, nu, ne: (be[i], 0, 0)),
            pl.BlockSpec(memory_space=pl.ANY),
            pl.BlockSpec((None, 1, D_MODEL), lambda i, be, nu, ne: (be[i], 0, 0)),
        ],
        out_specs=pl.BlockSpec((FFN_BLOCK, half), lambda i, be, nu, ne: (i, 0)),
        scratch_shapes=[
            pltpu.VMEM((D_MODEL, 2 * D_EXPERT), F32),
            pltpu.VMEM((D_EXPERT, D_MODEL), F32),
            pltpu.VMEM((D_MODEL, 2 * D_EXPERT), BF16),
            pltpu.VMEM((D_EXPERT, D_MODEL), BF16),
            pltpu.SemaphoreType.DMA((2,)),
        ],
    )
    return pl.pallas_call(
        _ffn_kernel,
        grid_spec=grid_spec,
        out_shape=jax.ShapeDtypeStruct((cap, half), U32),
        compiler_params=pltpu.CompilerParams(
            dimension_semantics=("arbitrary",), vmem_limit_bytes=VMEM_LIMIT),
        name="ffn",
    )(blk_e, nused, next_e, xs, w1, b1p, w2, b2)


def _combine_kernel(run_dst_ref, run_len_ref, tile_rows_ref, y_ref, pos_ref, gate_ref, x1_ref, mod_ref,
                    nrm_ref, out_ref, ybuf, w_hi_ref, w_lo_ref, y_a_ref, y_b_ref, sem, *, nt):
    j = pl.program_id(0)
    slot = j % 2

    def fetch_tile(t, s):
        def run_body(e, dst_off):
            n = run_len_ref[t * N_EXPERTS + e]
            _, dst_off = _run_copies(n, y_ref, run_dst_ref[t * N_EXPERTS + e], ybuf.at[s], dst_off,
                                     sem.at[s])
            return dst_off

        lax.fori_loop(0, N_EXPERTS, run_body, 0)

    @pl.when(j == 0)
    def _():
        ybuf[...] = jnp.zeros(ybuf.shape, U32)
        fetch_tile(j, slot)

    @pl.when(j + 1 < nt)
    def _():
        fetch_tile(j + 1, 1 - slot)

    pos_t = jnp.transpose(pos_ref[...].astype(F32))
    gate_t = jnp.transpose(gate_ref[...])
    for c0 in range(0, SORT_ROWS, LANES):
        c_iota = (lax.broadcasted_iota(I32, (TOKEN_TILE, LANES), 1) + c0).astype(F32)
        wsel = jnp.zeros((TOKEN_TILE, LANES), F32)
        for k in range(TOP_K):
            wsel = jnp.where(c_iota == pos_t[:, k:k + 1], gate_t[:, k:k + 1], wsel)
        hi, lo = _split_bf16(wsel)
        w_hi_ref[:, c0:c0 + LANES] = hi
        w_lo_ref[:, c0:c0 + LANES] = lo

    _wait_rows(tile_rows_ref[j], y_ref, ybuf.at[slot], sem.at[slot])
    for r0 in range(0, SORT_ROWS, UNPACK_ROWS):
        a, b = _unpack_pair(ybuf[slot, r0:r0 + UNPACK_ROWS, :])
        y_a_ref[r0:r0 + UNPACK_ROWS, :] = a
        y_b_ref[r0:r0 + UNPACK_ROWS, :] = b
    w_hi, w_lo = w_hi_ref[...], w_lo_ref[...]
    y_a, y_b = y_a_ref[...], y_b_ref[...]
    f = jnp.concatenate([_dot(w_hi, y_a) + _dot(w_lo, y_a), _dot(w_hi, y_b) + _dot(w_lo, y_b)],
                        axis=-1)
    gate_f = mod_ref[5:6, :]
    out_ref[...] = x1_ref[...] + gate_f * _rms(f, nrm_ref[...])


def _combine_call(run_dst, run_len, tile_rows, y, pos, gate, x1, mod3, nrm_post):
    batch, seq, _ = x1.shape
    ns = seq // TOKEN_TILE
    nt = batch * ns
    half = D_MODEL // 2
    grid_spec = pltpu.PrefetchScalarGridSpec(
        num_scalar_prefetch=3,
        grid=(nt,),
        in_specs=[
            pl.BlockSpec(memory_space=pl.ANY),
            pl.BlockSpec((None, SUBLANES, TOKEN_TILE), lambda j, *_: (j // ns, 0, j % ns)),
            pl.BlockSpec((None, SUBLANES, TOKEN_TILE), lambda j, *_: (j // ns, 0, j % ns)),
            pl.BlockSpec((None, TOKEN_TILE, D_MODEL), lambda j, *_: (j // ns, j % ns, 0)),
            pl.BlockSpec((None, 6, D_MODEL), lambda j, *_: (j // ns, 0, 0)),
            pl.BlockSpec((1, D_MODEL), lambda j, *_: (0, 0)),
        ],
        out_specs=pl.BlockSpec((None, TOKEN_TILE, D_MODEL), lambda j, *_: (j // ns, j % ns, 0)),
        scratch_shapes=[
            pltpu.VMEM((2, SORT_ROWS, half), U32),
            pltpu.VMEM((TOKEN_TILE, SORT_ROWS), BF16),
            pltpu.VMEM((TOKEN_TILE, SORT_ROWS), BF16),
            pltpu.VMEM((SORT_ROWS, half), BF16),
            pltpu.VMEM((SORT_ROWS, half), BF16),
            pltpu.SemaphoreType.DMA((2,)),
        ],
    )
    return pl.pallas_call(
        functools.partial(_combine_kernel, nt=nt),
        grid_spec=grid_spec,
        out_shape=jax.ShapeDtypeStruct((batch, seq, D_MODEL), F32),
        compiler_params=pltpu.CompilerParams(
            dimension_semantics=("arbitrary",), vmem_limit_bytes=VMEM_LIMIT),
        name="combine",
    )(run_dst, run_len, tile_rows, y, pos, gate, x1, mod3, nrm_post)


def _block_diag(w):
    heads, d, _ = w.shape
    n = heads * d
    tiled = jnp.tile(w.reshape(n, d), (1, heads))
    same_head = (jnp.arange(n)[:, None] // d) == (jnp.arange(n)[None, :] // d)
    return jnp.where(same_head, tiled, 0.0)


def _route_plan(cnt):
    nt = cnt.shape[0]
    run_len = (cnt + (SUBLANES - 1)) // SUBLANES * SUBLANES
    region = jnp.sum(run_len, axis=0)
    region_blk = (region + FFN_BLOCK - 1) // FFN_BLOCK * FFN_BLOCK
    earlier_e = jnp.arange(N_EXPERTS)[None, :] < jnp.arange(N_EXPERTS)[:, None]
    region_start = jnp.sum(jnp.where(earlier_e, region_blk[None, :], 0), axis=1)
    region_end = region_start + region_blk
    earlier_t = jnp.arange(nt)[None, :] < jnp.arange(nt)[:, None]
    run_before = jnp.sum(jnp.where(earlier_t[:, :, None], run_len[None, :, :], 0), axis=1)
    run_dst = region_start[None, :] + run_before
    tail_dst = region_start + region
    tail_len = region_blk - region
    total_rows = jnp.sum(region_blk)
    nused = total_rows // FFN_BLOCK
    max_rows = nt * TOKEN_TILE * TOP_K + nt * N_EXPERTS * (SUBLANES - 1) + N_EXPERTS * (FFN_BLOCK - SUBLANES)
    nb = -(-max_rows // FFN_BLOCK)
    blk_start = jnp.arange(nb, dtype=I32) * FFN_BLOCK
    blk_row = jnp.minimum(blk_start, total_rows - 1)
    blk_e = jnp.sum((blk_row[:, None] >= region_end[None, :]).astype(I32), axis=1)
    blk_e = jnp.minimum(blk_e, N_EXPERTS - 1).astype(I32)
    later_used = jnp.logical_and(jnp.arange(N_EXPERTS)[None, :] > jnp.arange(N_EXPERTS)[:, None],
                                 region_blk[None, :] > 0)
    next_e = jnp.min(jnp.where(later_used, jnp.arange(N_EXPERTS)[None, :], N_EXPERTS), axis=1)
    next_e = jnp.where(next_e == N_EXPERTS, -1, next_e)
    tile_rows = jnp.sum(run_len, axis=1)
    return dict(run_dst=run_dst.reshape(-1).astype(I32), run_len=run_len.reshape(-1).astype(I32),
                tile_rows=tile_rows.astype(I32), tail_dst=tail_dst.astype(I32),
                tail_len=tail_len.astype(I32), blk_e=blk_e, next_e=next_e.astype(I32),
                nused=nused.reshape(1).astype(I32)), nb * FFN_BLOCK


def kernel(x, c, w_ada, b_ada, norm_mix_pre, norm_mix_post, w_in, conv_w, conv_b, gate_a_w, gate_a_b, gate_x_w, gate_x_b, lru_lambda, pool_w, pool_b, pool_scale, w_out, norm_ffn_pre, norm_ffn_post, router_w, router_b, expert_w1, expert_b1, expert_w2, expert_b2):
    depth = w_ada.shape[0]
    batch = x.shape[0]
    for l in range(depth):
        mod3 = _ada_call(c, w_ada[l], b_ada[l]).reshape(batch, 6, D_MODEL)

        nrm = jnp.stack([norm_mix_pre[l], norm_mix_post[l], norm_ffn_pre[l]])
        lruv = jnp.stack([conv_b[l], gate_a_b[l], gate_x_b[l], lru_lambda[l]])
        wg = jnp.concatenate([_block_diag(gate_a_w[l]), _block_diag(gate_x_w[l])], axis=1).astype(BF16)
        poolv = jnp.stack([pool_b[l], pool_scale[l]])
        wp = _block_diag(pool_w[l]).astype(BF16)
        rw_pad = jnp.pad(router_w[l], ((0, 0), (0, LANES - N_EXPERTS)))
        rw_hi = rw_pad.astype(BF16)
        rw_lo = (rw_pad - rw_hi.astype(F32)).astype(BF16)
        rwh = jnp.concatenate([rw_hi, rw_lo], axis=1)
        rb = router_b[l].reshape(N_EXPERTS, 1)

        x1, h2, gate, pos, cnt = _mixer_call(
            x, mod3, nrm, w_in[l].astype(BF16), conv_w[l], lruv, wg, poolv, wp,
            w_out[l].astype(BF16), rwh, rb)

        plan, cap = _route_plan(cnt[:, :, :, 0].reshape(-1, N_EXPERTS))
        xs = _dispatch_call(plan["run_dst"], plan["run_len"], plan["tile_rows"], plan["tail_dst"],
                            plan["tail_len"], plan["nused"], h2.reshape(-1, D_MODEL), pos, cap)

        b1p = jnp.concatenate([expert_b1[l][:, 0::2], expert_b1[l][:, 1::2]], axis=-1)
        y = _ffn_call(plan["blk_e"], plan["nused"], plan["next_e"], xs, expert_w1[l],
                      b1p.reshape(N_EXPERTS, 1, 2 * D_EXPERT),
                      expert_w2[l], expert_b2[l].reshape(N_EXPERTS, 1, D_MODEL))

        x = _combine_call(plan["run_dst"], plan["run_len"], plan["tile_rows"], y, pos, gate, x1, mod3,
                          norm_ffn_post[l].reshape(1, D_MODEL))
    return x
```

```python
import functools

import jax
import jax.numpy as jnp
from jax import lax
from jax.experimental import pallas as pl
from jax.experimental.pallas import tpu as pltpu

F32 = jnp.float32
BF16 = jnp.bfloat16
I32 = jnp.int32
U32 = jnp.uint32

D_MODEL = 1024
LRU_WIDTH = 512
LRU_HEADS = 8
POOL_WIDTH = 512
POOL_WINDOWS = (2, 4, 8, 16)
POOL_GROUP_DIM = 128
CONV_WIDTH = 4
IN_WIDTH = 2 * LRU_WIDTH + POOL_WIDTH
LRU_C = 8.0
N_EXPERTS = 32
TOP_K = 4
D_EXPERT = 1024
SWIGLU_LIMIT = 7.0
SWIGLU_ALPHA = 1.702
NORM_EPS = 1e-6

SUBLANES = 8
LANES = 128
TOKEN_TILE = 256
FFN_BLOCK = 256
MIXER_BATCH_GROUP = 2
MIXER_PHASES = 6
CONV_CARRY = SUBLANES
POOL_CARRY = 16
SORT_ROWS = 1280
RUN_SIZES = (256, 128, 64, 32, 16, 8)
UNPACK_ROWS = 64
VMEM_LIMIT = 52 * 1024 * 1024


def _rms(v, g):
    return v * lax.rsqrt(jnp.mean(v * v, axis=-1, keepdims=True) + NORM_EPS) * g


def _dot(a, b):
    return jnp.dot(a, b, preferred_element_type=F32)


def _dot_nt(a, b):
    return lax.dot_general(a, b, (((1,), (1,)), ((), ())), preferred_element_type=F32)


def _split_bf16(v):
    hi = v.astype(BF16)
    lo = (v - hi.astype(F32)).astype(BF16)
    return hi, lo


def _pack_pair(a, b):
    ua = lax.bitcast_convert_type(a, U32)
    ub = lax.bitcast_convert_type(b, U32)
    return (ua >> 16) | (ub & jnp.uint32(0xFFFF0000))


def _unpack_pair(w):
    a = lax.bitcast_convert_type(w << 16, F32).astype(BF16)
    b = lax.bitcast_convert_type(w & jnp.uint32(0xFFFF0000), F32).astype(BF16)
    return a, b


def _ada_kernel(c_ref, w_ref, b_ref, o_ref):
    c = c_ref[...]
    ca = c * jax.nn.sigmoid(c)
    o_ref[...] = jnp.dot(ca, w_ref[...], preferred_element_type=F32,
                         precision=lax.Precision.HIGHEST) + b_ref[...]


def _ada_call(c, w_ada, b_ada):
    batch = c.shape[0]
    n_out = w_ada.shape[1]
    bn = D_MODEL
    return pl.pallas_call(
        _ada_kernel,
        grid=(n_out // bn,),
        in_specs=[
            pl.BlockSpec((batch, D_MODEL), lambda j: (0, 0)),
            pl.BlockSpec((D_MODEL, bn), lambda j: (0, j)),
            pl.BlockSpec((1, bn), lambda j: (0, j)),
        ],
        out_specs=pl.BlockSpec((batch, bn), lambda j: (0, j)),
        out_shape=jax.ShapeDtypeStruct((batch, n_out), F32),
        name="ada",
    )(c, w_ada, b_ada.reshape(1, n_out))


def _lru_scan(a, b, h0, out_ref):
    ts, c = a.shape
    groups = ts // SUBLANES
    a3 = a.reshape(groups, SUBLANES, c)
    b3 = b.reshape(groups, SUBLANES, c)
    row = lax.broadcasted_iota(I32, (groups, SUBLANES, c), 1)
    d = 1
    while d < SUBLANES:
        a_sh = pltpu.roll(a3, d, axis=1)
        b_sh = pltpu.roll(b3, d, axis=1)
        m = row >= d
        b3 = jnp.where(m, a3 * b_sh + b3, b3)
        a3 = jnp.where(m, a3 * a_sh, a3)
        d *= 2
    h = h0
    for g in range(groups):
        hg = a3[g] * h + b3[g]
        out_ref[pl.ds(g * SUBLANES, SUBLANES), :] = hg
        h = hg[SUBLANES - 1:SUBLANES, :]
    return h


def _mixer_kernel(x_ref, mod_ref, nrm_ref, w_in_ref, convw_ref, lruv_ref, wg_ref,
                  poolv_ref, wp_ref, w_out_ref, rwh_ref, rb_ref, rankw_ref,
                  x1_ref, h2_ref, gate_ref, pos_ref, cnt_ref,
                  xa_ext, xb_ext, hcar, hbuf):
    s = pl.program_id(1)
    shared = (nrm_ref, w_in_ref, convw_ref, lruv_ref, wg_ref, poolv_ref, wp_ref, w_out_ref,
              rwh_ref, rb_ref, rankw_ref)
    per_row = (x_ref, mod_ref, x1_ref, h2_ref, gate_ref, pos_ref, cnt_ref, xa_ext, xb_ext, hcar, hbuf)

    @pl.when(s == 0)
    def _():
        xa_ext[:, 0:CONV_CARRY, :] = jnp.zeros((xa_ext.shape[0], CONV_CARRY, LRU_WIDTH), F32)
        xb_ext[:, 0:POOL_CARRY, :] = jnp.zeros((xb_ext.shape[0], POOL_CARRY, POOL_WIDTH), F32)
        hcar[...] = jnp.zeros(hcar.shape, F32)

    tiles = [_mixer_tile(s, shared, *(ref.at[p] for ref in per_row)) for p in range(x_ref.shape[0])]
    for step in range(MIXER_PHASES - 1 + len(tiles) - 1):
        for p, tile in enumerate(tiles):
            if 0 <= step - p < MIXER_PHASES - 1:
                next(tile)
    for tile in tiles:
        assert next(tile, "done") == "done"


def _mixer_tile(s, shared, x_ref, mod_ref, x1_ref, h2_ref, gate_ref, pos_ref, cnt_ref,
                xa_ext, xb_ext, hcar, hbuf):
    (nrm_ref, w_in_ref, convw_ref, lruv_ref, wg_ref, poolv_ref, wp_ref, w_out_ref,
     rwh_ref, rb_ref, rankw_ref) = shared
    ts = TOKEN_TILE
    x = x_ref[...]
    mod = mod_ref[...]
    shift_m, scale_m, gate_m = mod[0:1], mod[1:2], mod[2:3]
    shift_f, scale_f = mod[3:4], mod[4:5]
    nrm = nrm_ref[...]

    h = _rms(x, nrm[0:1]) * (1.0 + scale_m) + shift_m
    u = _dot(h.astype(BF16), w_in_ref[...])
    yield
    xa_raw = u[:, :LRU_WIDTH]
    ga = u[:, LRU_WIDTH:2 * LRU_WIDTH]
    xb = u[:, 2 * LRU_WIDTH:]

    lruv = lruv_ref[...]
    conv_b, gate_a_b, gate_x_b, lam = lruv[0:1], lruv[1:2], lruv[2:3], lruv[3:4]
    convw = convw_ref[...]
    xa_ext[CONV_CARRY:, :] = xa_raw
    ext = xa_ext[...]
    acc = ext * convw[0:1]
    for k in range(1, CONV_WIDTH):
        acc = acc + pltpu.roll(ext, k, axis=0) * convw[k:k + 1]
    xa = acc[CONV_CARRY:, :] + conv_b
    xa_ext[0:CONV_CARRY, :] = xa_raw[ts - CONV_CARRY:, :]

    g = _dot(xa.astype(BF16), wg_ref[...])
    yield
    r = jax.nn.sigmoid(g[:, :LRU_WIDTH] + gate_a_b)
    i = jax.nn.sigmoid(g[:, LRU_WIDTH:] + gate_x_b)
    softplus_neg_lam = jnp.maximum(-lam, 0.0) + jnp.log(1.0 + jnp.exp(-jnp.abs(lam)))
    log_a = -LRU_C * r * softplus_neg_lam
    a = jnp.exp(log_a)
    mult = jnp.sqrt(1.0 - a * a)
    bterm = mult * (i * xa)
    h_last = _lru_scan(a, bterm, hcar[0:1, :], hbuf)
    hcar[0:1, :] = h_last
    ya = hbuf[...] * jax.nn.gelu(ga)
    yield

    xb_ext[POOL_CARRY:, :] = xb
    t_glob = s * ts + lax.broadcasted_iota(I32, (ts, POOL_GROUP_DIM), 0)
    pooled = []
    for gi, w in enumerate(POOL_WINDOWS):
        lo = gi * POOL_GROUP_DIM
        e = xb_ext[:, lo:lo + POOL_GROUP_DIM]
        acc_w = e
        span = 1
        while span < w:
            acc_w = acc_w + pltpu.roll(acc_w, span, axis=0)
            span *= 2
        count = jnp.minimum(t_glob + 1, w).astype(F32)
        pooled.append(acc_w[POOL_CARRY:, :] / count - xb[:, lo:lo + POOL_GROUP_DIM])
    xb_ext[0:POOL_CARRY, :] = xb[ts - POOL_CARRY:, :]
    p = jnp.concatenate(pooled, axis=-1)
    poolv = poolv_ref[...]
    yb = (_dot(p.astype(BF16), wp_ref[...]) + poolv[0:1]) * poolv[1:2]
    yield

    mix =_dot(jnp.concatenate([ya, yb], axis=-1).astype(BF16), w_out_ref[...])
    x1 = x + gate_m * _rms(mix, nrm[1:2])
    x1_ref[...] = x1

    h2 = _rms(x1, nrm[2:3]) * (1.0 + scale_f) + shift_f
    h2_hi, h2_lo = _split_bf16(h2)
    h2_ref[...] = h2_hi
    yield

    hi_terms = _dot(h2_hi, rwh_ref[...])
    lo_term = _dot(h2_lo, rwh_ref[:, 0:LANES])
    logits_t = hi_terms[:, :LANES] + hi_terms[:, LANES:] + lo_term
    logits = jnp.transpose(logits_t)[0:N_EXPERTS, :] + rb_ref[...]

    e_iota = lax.broadcasted_iota(I32, (N_EXPERTS, ts), 0)
    beaten = jnp.zeros((N_EXPERTS, ts), F32)
    for e2 in range(N_EXPERTS):
        other = logits[e2:e2 + 1, :]
        ahead_strict = jnp.where(other > logits, 1.0, 0.0)
        ahead_or_tie = jnp.where(other >= logits, 1.0, 0.0)
        beaten = beaten + jnp.where(e_iota > e2, ahead_or_tie, ahead_strict)
    sels = [beaten == float(k) for k in range(TOP_K)]
    vals = [jnp.sum(jnp.where(sel, logits, 0.0), axis=0, keepdims=True) for sel in sels]
    exps = [jnp.exp(v - vals[0]) for v in vals]
    denom = exps[0] + exps[1] + exps[2] + exps[3]
    gates = [ex / denom for ex in exps]

    member = jnp.where(beaten < float(TOP_K), 1.0, 0.0)
    rank_cnt = _dot(member.astype(BF16), rankw_ref[...])
    rank, n_e = rank_cnt[:, :ts], rank_cnt[:, ts:].astype(I32)
    run_len = ((n_e + (SUBLANES - 1)) >> 3) << 3
    ltri = (lax.broadcasted_iota(I32, (N_EXPERTS, N_EXPERTS), 1)
            < lax.broadcasted_iota(I32, (N_EXPERTS, N_EXPERTS), 0))
    run_off = _dot(jnp.where(ltri, 1.0, 0.0).astype(BF16), run_len.astype(F32).astype(BF16))
    slot = rank + run_off
    pos_rows = [jnp.sum(jnp.where(sel, slot, 0.0), axis=0, keepdims=True) for sel in sels]

    zeros_f = jnp.zeros((SUBLANES - TOP_K, ts), F32)
    gate_ref[...] = jnp.concatenate(gates + [zeros_f], axis=0)
    pos_ref[...] = jnp.concatenate(pos_rows + [zeros_f], axis=0).astype(I32)
    cnt_ref[...] = n_e[:, :LANES]


def _mixer_call(x, mod3, nrm, w_in_b, convw, lruv, wg, poolv, wp, w_out_b, rwh, rb):
    batch, seq, _ = x.shape
    ns = seq // TOKEN_TILE
    t_row = jnp.arange(TOKEN_TILE)[:, None]
    t_col = jnp.arange(2 * TOKEN_TILE)[None, :]
    rankw = jnp.where(t_col >= TOKEN_TILE, 1.0, jnp.where(t_row < t_col, 1.0, 0.0)).astype(BF16)
    grp = MIXER_BATCH_GROUP if batch % MIXER_BATCH_GROUP == 0 else 1
    const = lambda shape: pl.BlockSpec(shape, lambda b, s: (0,) * len(shape))
    return pl.pallas_call(
        _mixer_kernel,
        grid=(batch // grp, ns),
        in_specs=[
            pl.BlockSpec((grp, TOKEN_TILE, D_MODEL), lambda b, s: (b, s, 0)),
            pl.BlockSpec((grp, 6, D_MODEL), lambda b, s: (b, 0, 0)),
            const(nrm.shape), const(w_in_b.shape), const(convw.shape), const(lruv.shape),
            const(wg.shape), const(poolv.shape), const(wp.shape), const(w_out_b.shape),
            const(rwh.shape), const(rb.shape), const(rankw.shape),
        ],
        out_specs=[
            pl.BlockSpec((grp, TOKEN_TILE, D_MODEL), lambda b, s: (b, s, 0)),
            pl.BlockSpec((grp, TOKEN_TILE, D_MODEL), lambda b, s: (b, s, 0)),
            pl.BlockSpec((grp, SUBLANES, TOKEN_TILE), lambda b, s: (b, 0, s)),
            pl.BlockSpec((grp, SUBLANES, TOKEN_TILE), lambda b, s: (b, 0, s)),
            pl.BlockSpec((grp, None, N_EXPERTS, LANES), lambda b, s: (b, s, 0, 0)),
        ],
        out_shape=[
            jax.ShapeDtypeStruct((batch, seq, D_MODEL), F32),
            jax.ShapeDtypeStruct((batch, seq, D_MODEL), BF16),
            jax.ShapeDtypeStruct((batch, SUBLANES, seq), F32),
            jax.ShapeDtypeStruct((batch, SUBLANES, seq), I32),
            jax.ShapeDtypeStruct((batch, ns, N_EXPERTS, LANES), I32),
        ],
        scratch_shapes=[
            pltpu.VMEM((grp, TOKEN_TILE + CONV_CARRY, LRU_WIDTH), F32),
            pltpu.VMEM((grp, TOKEN_TILE + POOL_CARRY, POOL_WIDTH), F32),
            pltpu.VMEM((grp, SUBLANES, LRU_WIDTH), F32),
            pltpu.VMEM((grp, TOKEN_TILE, LRU_WIDTH), F32),
        ],
        compiler_params=pltpu.CompilerParams(
            dimension_semantics=("arbitrary", "arbitrary"), vmem_limit_bytes=VMEM_LIMIT),
        name="mixer",
    )(x, mod3, nrm, w_in_b, convw, lruv, wg, poolv, wp, w_out_b, rwh, rb, rankw)


def _run_copies(length, src_ref, src_off, dst_ref, dst_off, sem, wait=False):
    for size in RUN_SIZES:
        hit = (length & size) != 0

        @pl.when(hit)
        def _(src_off=src_off, dst_off=dst_off, size=size):
            copy = pltpu.make_async_copy(
                src_ref.at[pl.ds(pl.multiple_of(src_off, SUBLANES), size)],
                dst_ref.at[pl.ds(pl.multiple_of(dst_off, SUBLANES), size)],
                sem)
            if wait:
                copy.wait()
            else:
                copy.start()

        step = jnp.where(hit, size, 0)
        src_off = src_off + step
        dst_off = dst_off + step
    return src_off, dst_off


def _plan_index(tile, expert):
    return (tile + 1) * N_EXPERTS + expert


def _wait_rows(rows, hbm_ref, vmem_ref, sem):
    @pl.when(rows > 0)
    def _():
        n = pl.multiple_of(rows, SUBLANES)
        pltpu.make_async_copy(hbm_ref.at[pl.ds(0, n)], vmem_ref.at[pl.ds(0, n)], sem).wait()


def _dispatch_kernel(run_dst_ref, run_len_ref, tile_rows_ref, tail_dst_ref, tail_len_ref, nused_ref,
                     h2_ref, pos_ref, xs_ref, sorted_ref, zero_ref, perm_ref, sem, zsem, *, nt):
    j = pl.program_id(0)
    last = nt - 1
    half = D_MODEL // 2

    @pl.when(j == 0)
    def _():
        zero_ref[...] = jnp.zeros(zero_ref.shape, U32)

        for wait in (False, True):
            def tail_body(e, carry, wait=wait):
                _run_copies(tail_len_ref[e], zero_ref, jnp.int32(0), xs_ref, tail_dst_ref[e], zsem,
                            wait=wait)
                return carry

            lax.fori_loop(0, N_EXPERTS, tail_body, 0)

            def spare_body(b, carry, wait=wait):
                copy = pltpu.make_async_copy(
                    zero_ref, xs_ref.at[pl.ds(pl.multiple_of(b * FFN_BLOCK, FFN_BLOCK), FFN_BLOCK)], zsem)
                if wait:
                    copy.wait()
                else:
                    copy.start()
                return carry

            lax.fori_loop(nused_ref[0], xs_ref.shape[0] // FFN_BLOCK, spare_body, 0)

    slot = j % 2
    stage = sorted_ref.at[slot]

    def send_tile(t, s, unrolled):
        def run_body(e, src_off):
            k = _plan_index(t, e)
            src_off, _ = _run_copies(run_len_ref[k], sorted_ref.at[s], src_off, xs_ref, run_dst_ref[k],
                                     sem.at[s])
            return src_off

        if unrolled:
            src_off = jnp.int32(0)
            for e in range(N_EXPERTS):
                src_off = run_body(e, src_off)
        else:
            lax.fori_loop(0, N_EXPERTS, run_body, 0)

    send_tile(j - 1, 1 - slot, unrolled=True)

    pos = pos_ref[...]
    for r0 in range(0, SORT_ROWS, LANES):
        r_iota = lax.broadcasted_iota(I32, (LANES, TOKEN_TILE), 0) + r0
        perm = jnp.zeros((LANES, TOKEN_TILE), F32)
        for k in range(TOP_K):
            perm = jnp.where(r_iota == pos[k:k + 1, :], 1.0, perm)
        perm_ref[r0:r0 + LANES, :] = perm.astype(BF16)
    rows = _dot(perm_ref[...], h2_ref[...])

    @pl.when(j >= 2)
    def _():
        _wait_rows(tile_rows_ref[j - 2 + 1], xs_ref, stage, sem.at[slot])

    stage[...] = _pack_pair(rows[:, :half], rows[:, half:])

    @pl.when(j == last)
    def _():
        send_tile(j, slot, unrolled=False)
        _wait_rows(tile_rows_ref[j + 1], xs_ref, stage, sem.at[slot])
        if last >= 1:
            _wait_rows(tile_rows_ref[j - 1 + 1], xs_ref, sorted_ref.at[1 - slot], sem.at[1 - slot])


def _dispatch_call(run_dst, run_len, tile_rows, tail_dst, tail_len, nused, h2, pos, cap):
    tokens = h2.shape[0]
    nt = tokens // TOKEN_TILE
    ns = pos.shape[2] // TOKEN_TILE
    half = D_MODEL // 2
    grid_spec = pltpu.PrefetchScalarGridSpec(
        num_scalar_prefetch=6,
        grid=(nt,),
        in_specs=[
            pl.BlockSpec((TOKEN_TILE, D_MODEL), lambda j, *_: (j, 0)),
            pl.BlockSpec((None, SUBLANES, TOKEN_TILE), lambda j, *_: (j // ns, 0, j % ns)),
        ],
        out_specs=pl.BlockSpec(memory_space=pl.ANY),
        scratch_shapes=[
            pltpu.VMEM((2, SORT_ROWS, half), U32),
            pltpu.VMEM((FFN_BLOCK, half), U32),
            pltpu.VMEM((SORT_ROWS, TOKEN_TILE), BF16),
            pltpu.SemaphoreType.DMA((2,)),
            pltpu.SemaphoreType.DMA(()),
        ],
    )
    return pl.pallas_call(
        functools.partial(_dispatch_kernel, nt=nt),
        grid_spec=grid_spec,
        out_shape=jax.ShapeDtypeStruct((cap, half), U32),
        compiler_params=pltpu.CompilerParams(
            dimension_semantics=("arbitrary",), vmem_limit_bytes=VMEM_LIMIT),
        name="dispatch",
    )(run_dst, run_len, tile_rows, tail_dst, tail_len, nused, h2, pos)


W1_CHUNK = 256


def _ffn_kernel(blk_e_ref, nused_ref, next_e_ref, xs_ref, w1_hbm, b1_ref, w2_hbm, b2_ref, y_ref,
                w1_ref, w2_ref, w1s, w2s, wsem):
    i = pl.program_id(0)
    half = D_MODEL // 2
    prev = jnp.maximum(i - 1, 0)
    expert = blk_e_ref[i]
    new_expert = jnp.logical_or(i == 0, expert != blk_e_ref[prev])

    def fetch(e):
        return (pltpu.make_async_copy(w1_hbm.at[e], w1_ref, wsem.at[0]),
                pltpu.make_async_copy(w2_hbm.at[e], w2_ref, wsem.at[1]))

    @pl.when(i == 0)
    def _():
        for copy in fetch(expert):
            copy.start()

    @pl.when(jnp.logical_and(i < nused_ref[0], new_expert))
    def _():
        for copy in fetch(expert):
            copy.wait()
        hc = W1_CHUNK // 2
        r = lax.broadcasted_iota(I32, (W1_CHUNK, W1_CHUNK), 0)
        q = lax.broadcasted_iota(I32, (W1_CHUNK, W1_CHUNK), 1)
        src_col = jnp.where(q < hc, 2 * q, 2 * (q - hc) + 1)
        sel = jnp.where(r == src_col, 1.0, 0.0).astype(BF16)
        for c in range(2 * D_EXPERT // W1_CHUNK):
            chunk = w1_ref[:, c * W1_CHUNK:(c + 1) * W1_CHUNK].astype(BF16)
            picked = _dot(chunk, sel).astype(BF16)
            w1s[:, c * hc:(c + 1) * hc] = picked[:, :hc]
            w1s[:, D_EXPERT + c * hc:D_EXPERT + (c + 1) * hc] = picked[:, hc:]
        w2s[...] = w2_ref[...].astype(BF16)

        nxt = next_e_ref[expert]

        @pl.when(nxt >= 0)
        def _():
            for copy in fetch(nxt):
                copy.start()

    @pl.when(i < nused_ref[0])
    def _():
        x_a, x_b = _unpack_pair(xs_ref[...])
        u = _dot(jnp.concatenate([x_a, x_b], axis=-1), w1s[...]) + b1_ref[...]
        glu = jnp.minimum(u[:, :D_EXPERT], SWIGLU_LIMIT)
        lin = jnp.clip(u[:, D_EXPERT:], -SWIGLU_LIMIT, SWIGLU_LIMIT)
        act = glu * jax.nn.sigmoid(SWIGLU_ALPHA * glu) * (lin + 1.0)
        y = _dot(act.astype(BF16), w2s[...]) + b2_ref[...]
        yb = y.astype(BF16).astype(F32)
        y_ref[...] = _pack_pair(yb[:, :half], yb[:, half:])

    @pl.when(i >= nused_ref[0])
    def _():
        y_ref[...] = jnp.zeros(y_ref.shape, U32)


def _ffn_call(blk_e, nused, next_e, xs, w1, b1p, w2, b2):
    cap, half = xs.shape
    nb = cap // FFN_BLOCK
    row_blk = lambda i, be, nu, ne: (jnp.minimum(i, nu[0] - 1), 0)
    grid_spec = pltpu.PrefetchScalarGridSpec(
        num_scalar_prefetch=3,
        grid=(nb,),
        in_specs=[
            pl.BlockSpec((FFN_BLOCK, half), row_blk),
            pl.BlockSpec(memory_space=pl.ANY),
            pl.BlockSpec((None, 1, 2 * D_EXPERT), lambda i, be, nu, ne: (be[i], 0, 0)),
            pl.BlockSpec(memory_space=pl.ANY),
            pl.BlockSpec((None, 1, D_MODEL), lambda i, be, nu, ne: (be[i], 0, 0)),
        ],
        out_specs=pl.BlockSpec((FFN_BLOCK, half), lambda i, be, nu, ne: (i, 0)),
        scratch_shapes=[
            pltpu.VMEM((D_MODEL, 2 * D_EXPERT), F32),
            pltpu.VMEM((D_EXPERT, D_MODEL), F32),
            pltpu.VMEM((D_MODEL, 2 * D_EXPERT), BF16),
            pltpu.VMEM((D_EXPERT, D_MODEL), BF16),
            pltpu.SemaphoreType.DMA((2,)),
        ],
    )
    return pl.pallas_call(
        _ffn_kernel,
        grid_spec=grid_spec,
        out_shape=jax.ShapeDtypeStruct((cap, half), U32),
        compiler_params=pltpu.CompilerParams(
            dimension_semantics=("arbitrary",), vmem_limit_bytes=VMEM_LIMIT),
        name="ffn",
    )(blk_e, nused, next_e, xs, w1, b1p, w2, b2)


def _combine_kernel(run_dst_ref, run_len_ref, tile_rows_ref, y_ref, pos_ref, gate_ref, x1_ref, mod_ref,
                    nrm_ref, out_ref, ybuf, w_hi_ref, w_lo_ref, y_a_ref, y_b_ref, sem, *, nt):
    j = pl.program_id(0)
    slot = j % 2

    def fetch_tile(t, s, unrolled):
        def run_body(e, dst_off):
            k = _plan_index(t, e)
            _, dst_off = _run_copies(run_len_ref[k], y_ref, run_dst_ref[k], ybuf.at[s], dst_off,
                                     sem.at[s])
            return dst_off

        if unrolled:
            dst_off = jnp.int32(0)
            for e in range(N_EXPERTS):
                dst_off = run_body(e, dst_off)
        else:
            lax.fori_loop(0, N_EXPERTS, run_body, 0)

    @pl.when(j == 0)
    def _():
        ybuf[...] = jnp.zeros(ybuf.shape, U32)
        fetch_tile(j, slot, unrolled=False)

    fetch_tile(j + 1, 1 - slot, unrolled=True)

    pos_t = jnp.transpose(pos_ref[...].astype(F32))
    gate_t = jnp.transpose(gate_ref[...])
    for c0 in range(0, SORT_ROWS, LANES):
        c_iota = (lax.broadcasted_iota(I32, (TOKEN_TILE, LANES), 1) + c0).astype(F32)
        wsel = jnp.zeros((TOKEN_TILE, LANES), F32)
        for k in range(TOP_K):
            wsel = jnp.where(c_iota == pos_t[:, k:k + 1], gate_t[:, k:k + 1], wsel)
        hi, lo = _split_bf16(wsel)
        w_hi_ref[:, c0:c0 + LANES] = hi
        w_lo_ref[:, c0:c0 + LANES] = lo

    _wait_rows(tile_rows_ref[j + 1], y_ref, ybuf.at[slot], sem.at[slot])
    for r0 in range(0, SORT_ROWS, UNPACK_ROWS):
        a, b = _unpack_pair(ybuf[slot, r0:r0 + UNPACK_ROWS, :])
        y_a_ref[r0:r0 + UNPACK_ROWS, :] = a
        y_b_ref[r0:r0 + UNPACK_ROWS, :] = b
    w_hi, w_lo = w_hi_ref[...], w_lo_ref[...]
    y_a, y_b = y_a_ref[...], y_b_ref[...]
    f = jnp.concatenate([_dot(w_hi, y_a) + _dot(w_lo, y_a), _dot(w_hi, y_b) + _dot(w_lo, y_b)],
                        axis=-1)
    gate_f = mod_ref[5:6, :]
    out_ref[...] = x1_ref[...] + gate_f * _rms(f, nrm_ref[...])


def _combine_call(run_dst, run_len, tile_rows, y, pos, gate, x1, mod3, nrm_post):
    batch, seq, _ = x1.shape
    ns = seq // TOKEN_TILE
    nt = batch * ns
    half = D_MODEL // 2
    grid_spec = pltpu.PrefetchScalarGridSpec(
        num_scalar_prefetch=3,
        grid=(nt,),
        in_specs=[
            pl.BlockSpec(memory_space=pl.ANY),
            pl.BlockSpec((None, SUBLANES, TOKEN_TILE), lambda j, *_: (j // ns, 0, j % ns)),
            pl.BlockSpec((None, SUBLANES, TOKEN_TILE), lambda j, *_: (j // ns, 0, j % ns)),
            pl.BlockSpec((None, TOKEN_TILE, D_MODEL), lambda j, *_: (j // ns, j % ns, 0)),
            pl.BlockSpec((None, 6, D_MODEL), lambda j, *_: (j // ns, 0, 0)),
            pl.BlockSpec((1, D_MODEL), lambda j, *_: (0, 0)),
        ],
        out_specs=pl.BlockSpec((None, TOKEN_TILE, D_MODEL), lambda j, *_: (j // ns, j % ns, 0)),
        scratch_shapes=[
            pltpu.VMEM((2, SORT_ROWS, half), U32),
            pltpu.VMEM((TOKEN_TILE, SORT_ROWS), BF16),
            pltpu.VMEM((TOKEN_TILE, SORT_ROWS), BF16),
            pltpu.VMEM((SORT_ROWS, half), BF16),
            pltpu.VMEM((SORT_ROWS, half), BF16),
            pltpu.SemaphoreType.DMA((2,)),
        ],
    )
    return pl.pallas_call(
        functools.partial(_combine_kernel, nt=nt),
        grid_spec=grid_spec,
        out_shape=jax.ShapeDtypeStruct((batch, seq, D_MODEL), F32),
        compiler_params=pltpu.CompilerParams(
            dimension_semantics=("arbitrary",), vmem_limit_bytes=VMEM_LIMIT),
        name="combine",
    )(run_dst, run_len, tile_rows, y, pos, gate, x1, mod3, nrm_post)


def _block_diag(w):
    heads, d, _ = w.shape
    n = heads * d
    tiled = jnp.tile(w.reshape(n, d), (1, heads))
    same_head = (jnp.arange(n)[:, None] // d) == (jnp.arange(n)[None, :] // d)
    return jnp.where(same_head, tiled, 0.0)


def _route_plan(cnt):
    nt = cnt.shape[0]
    run_len = (cnt + (SUBLANES - 1)) // SUBLANES * SUBLANES
    region = jnp.sum(run_len, axis=0)
    region_blk = (region + FFN_BLOCK - 1) // FFN_BLOCK * FFN_BLOCK
    earlier_e = jnp.arange(N_EXPERTS)[None, :] < jnp.arange(N_EXPERTS)[:, None]
    region_start = jnp.sum(jnp.where(earlier_e, region_blk[None, :], 0), axis=1)
    region_end = region_start + region_blk
    earlier_t = jnp.arange(nt)[None, :] < jnp.arange(nt)[:, None]
    run_before = jnp.sum(jnp.where(earlier_t[:, :, None], run_len[None, :, :], 0), axis=1)
    run_dst = region_start[None, :] + run_before
    tail_dst = region_start + region
    tail_len = region_blk - region
    total_rows = jnp.sum(region_blk)
    nused = total_rows // FFN_BLOCK
    max_rows = nt * TOKEN_TILE * TOP_K + nt * N_EXPERTS * (SUBLANES - 1) + N_EXPERTS * (FFN_BLOCK - SUBLANES)
    nb = -(-max_rows // FFN_BLOCK)
    blk_start = jnp.arange(nb, dtype=I32) * FFN_BLOCK
    blk_row = jnp.minimum(blk_start, total_rows - 1)
    blk_e = jnp.sum((blk_row[:, None] >= region_end[None, :]).astype(I32), axis=1)
    blk_e = jnp.minimum(blk_e, N_EXPERTS - 1).astype(I32)
    later_used = jnp.logical_and(jnp.arange(N_EXPERTS)[None, :] > jnp.arange(N_EXPERTS)[:, None],
                                 region_blk[None, :] > 0)
    next_e = jnp.min(jnp.where(later_used, jnp.arange(N_EXPERTS)[None, :], N_EXPERTS), axis=1)
    next_e = jnp.where(next_e == N_EXPERTS, -1, next_e)
    tile_rows = jnp.sum(run_len, axis=1)
    edge = ((1, 1), (0, 0))
    run_dst, run_len, tile_rows = jnp.pad(run_dst, edge), jnp.pad(run_len, edge), jnp.pad(tile_rows, 1)
    return dict(run_dst=run_dst.reshape(-1).astype(I32), run_len=run_len.reshape(-1).astype(I32),
                tile_rows=tile_rows.astype(I32), tail_dst=tail_dst.astype(I32),
                tail_len=tail_len.astype(I32), blk_e=blk_e, next_e=next_e.astype(I32),
                nused=nused.reshape(1).astype(I32)), nb * FFN_BLOCK


def kernel(x, c, w_ada, b_ada, norm_mix_pre, norm_mix_post, w_in, conv_w, conv_b, gate_a_w, gate_a_b, gate_x_w, gate_x_b, lru_lambda, pool_w, pool_b, pool_scale, w_out, norm_ffn_pre, norm_ffn_post, router_w, router_b, expert_w1, expert_b1, expert_w2, expert_b2):
    depth = w_ada.shape[0]
    batch = x.shape[0]
    for l in range(depth):
        mod3 = _ada_call(c, w_ada[l], b_ada[l]).reshape(batch, 6, D_MODEL)

        nrm = jnp.stack([norm_mix_pre[l], norm_mix_post[l], norm_ffn_pre[l]])
        lruv = jnp.stack([conv_b[l], gate_a_b[l], gate_x_b[l], lru_lambda[l]])
        wg = jnp.concatenate([_block_diag(gate_a_w[l]), _block_diag(gate_x_w[l])], axis=1).astype(BF16)
        poolv = jnp.stack([pool_b[l], pool_scale[l]])
        wp = _block_diag(pool_w[l]).astype(BF16)
        rw_pad = jnp.pad(router_w[l], ((0, 0), (0, LANES - N_EXPERTS)))
        rw_hi = rw_pad.astype(BF16)
        rw_lo = (rw_pad - rw_hi.astype(F32)).astype(BF16)
        rwh = jnp.concatenate([rw_hi, rw_lo], axis=1)
        rb = router_b[l].reshape(N_EXPERTS, 1)

        x1, h2, gate, pos, cnt = _mixer_call(
            x, mod3, nrm, w_in[l].astype(BF16), conv_w[l], lruv, wg, poolv, wp,
            w_out[l].astype(BF16), rwh, rb)

        plan, cap = _route_plan(cnt[:, :, :, 0].reshape(-1, N_EXPERTS))
        xs = _dispatch_call(plan["run_dst"], plan["run_len"], plan["tile_rows"], plan["tail_dst"],
                            plan["tail_len"], plan["nused"], h2.reshape(-1, D_MODEL), pos, cap)

        b1p = jnp.concatenate([expert_b1[l][:, 0::2], expert_b1[l][:, 1::2]], axis=-1)
        y = _ffn_call(plan["blk_e"], plan["nused"], plan["next_e"], xs, expert_w1[l],
                      b1p.reshape(N_EXPERTS, 1, 2 * D_EXPERT),
                      expert_w2[l], expert_b2[l].reshape(N_EXPERTS, 1, D_MODEL))

        x = _combine_call(plan["run_dst"], plan["run_len"], plan["tile_rows"], y, pos, gate, x1, mod3,
                          norm_ffn_post[l].reshape(1, D_MODEL))
    return x
```

```python
import functools

import jax
import jax.numpy as jnp
from jax import lax
from jax.experimental import pallas as pl
from jax.experimental.pallas import tpu as pltpu

F32 = jnp.float32
BF16 = jnp.bfloat16
I32 = jnp.int32
U32 = jnp.uint32

D_MODEL = 1024
LRU_WIDTH = 512
LRU_HEADS = 8
POOL_WIDTH = 512
POOL_WINDOWS = (2, 4, 8, 16)
POOL_GROUP_DIM = 128
CONV_WIDTH = 4
IN_WIDTH = 2 * LRU_WIDTH + POOL_WIDTH
LRU_C = 8.0
N_EXPERTS = 32
TOP_K = 4
D_EXPERT = 1024
SWIGLU_LIMIT = 7.0
SWIGLU_ALPHA = 1.702
NORM_EPS = 1e-6

SUBLANES = 8
LANES = 128
TOKEN_TILE = 256
FFN_BLOCK = 256
MIXER_BATCH_GROUP = 4
MIXER_PHASES = 6
CONV_CARRY = SUBLANES
POOL_CARRY = 16
SORT_ROWS = 1280
RUN_SIZES = (256, 128, 64, 32, 16, 8)
UNPACK_ROWS = 64
VMEM_LIMIT = 52 * 1024 * 1024


def _rms(v, g):
    return v * lax.rsqrt(jnp.mean(v * v, axis=-1, keepdims=True) + NORM_EPS) * g


def _dot(a, b):
    return jnp.dot(a, b, preferred_element_type=F32)


def _dot_nt(a, b):
    return lax.dot_general(a, b, (((1,), (1,)), ((), ())), preferred_element_type=F32)


def _split_bf16(v):
    hi = v.astype(BF16)
    lo = (v - hi.astype(F32)).astype(BF16)
    return hi, lo


def _pack_pair(a, b):
    ua = lax.bitcast_convert_type(a, U32)
    ub = lax.bitcast_convert_type(b, U32)
    return (ua >> 16) | (ub & jnp.uint32(0xFFFF0000))


def _unpack_pair(w):
    a = lax.bitcast_convert_type(w << 16, F32).astype(BF16)
    b = lax.bitcast_convert_type(w & jnp.uint32(0xFFFF0000), F32).astype(BF16)
    return a, b


def _ada_kernel(c_ref, w_ref, b_ref, o_ref):
    c = c_ref[...]
    ca = c * jax.nn.sigmoid(c)
    o_ref[...] = jnp.dot(ca, w_ref[...], preferred_element_type=F32,
                         precision=lax.Precision.HIGHEST) + b_ref[...]


def _ada_call(c, w_ada, b_ada):
    batch = c.shape[0]
    n_out = w_ada.shape[1]
    bn = D_MODEL
    return pl.pallas_call(
        _ada_kernel,
        grid=(n_out // bn,),
        in_specs=[
            pl.BlockSpec((batch, D_MODEL), lambda j: (0, 0)),
            pl.BlockSpec((D_MODEL, bn), lambda j: (0, j)),
            pl.BlockSpec((1, bn), lambda j: (0, j)),
        ],
        out_specs=pl.BlockSpec((batch, bn), lambda j: (0, j)),
        out_shape=jax.ShapeDtypeStruct((batch, n_out), F32),
        name="ada",
    )(c, w_ada, b_ada.reshape(1, n_out))


def _lru_scan(a, b, h0, out_ref):
    ts, c = a.shape
    groups = ts // SUBLANES
    a3 = a.reshape(groups, SUBLANES, c)
    b3 = b.reshape(groups, SUBLANES, c)
    row = lax.broadcasted_iota(I32, (groups, SUBLANES, c), 1)
    d = 1
    while d < SUBLANES:
        a_sh = pltpu.roll(a3, d, axis=1)
        b_sh = pltpu.roll(b3, d, axis=1)
        m = row >= d
        b3 = jnp.where(m, a3 * b_sh + b3, b3)
        a3 = jnp.where(m, a3 * a_sh, a3)
        d *= 2
    h = h0
    for g in range(groups):
        hg = a3[g] * h + b3[g]
        out_ref[pl.ds(g * SUBLANES, SUBLANES), :] = hg
        h = hg[SUBLANES - 1:SUBLANES, :]
    return h


def _mixer_kernel(x_ref, mod_ref, nrm_ref, w_in_ref, convw_ref, lruv_ref, wg_ref,
                  poolv_ref, wp_ref, w_out_ref, rwh_ref, rb_ref, rankw_ref,
                  x1_ref, h2_ref, gate_ref, pos_ref, cnt_ref,
                  xa_ext, xb_ext, hcar, hbuf):
    s = pl.program_id(1)
    shared = (nrm_ref, w_in_ref, convw_ref, lruv_ref, wg_ref, poolv_ref, wp_ref, w_out_ref,
              rwh_ref, rb_ref, rankw_ref)
    per_row = (x_ref, mod_ref, x1_ref, h2_ref, gate_ref, pos_ref, cnt_ref, xa_ext, xb_ext, hcar, hbuf)

    @pl.when(s == 0)
    def _():
        xa_ext[:, 0:CONV_CARRY, :] = jnp.zeros((xa_ext.shape[0], CONV_CARRY, LRU_WIDTH), F32)
        xb_ext[:, 0:POOL_CARRY, :] = jnp.zeros((xb_ext.shape[0], POOL_CARRY, POOL_WIDTH), F32)
        hcar[...] = jnp.zeros(hcar.shape, F32)

    tiles = [_mixer_tile(s, shared, *(ref.at[p] for ref in per_row)) for p in range(x_ref.shape[0])]
    for step in range(MIXER_PHASES - 1 + len(tiles) - 1):
        for p, tile in enumerate(tiles):
            if 0 <= step - p < MIXER_PHASES - 1:
                next(tile)
    for tile in tiles:
        assert next(tile, "done") == "done"


def _mixer_tile(s, shared, x_ref, mod_ref, x1_ref, h2_ref, gate_ref, pos_ref, cnt_ref,
                xa_ext, xb_ext, hcar, hbuf):
    (nrm_ref, w_in_ref, convw_ref, lruv_ref, wg_ref, poolv_ref, wp_ref, w_out_ref,
     rwh_ref, rb_ref, rankw_ref) = shared
    ts = TOKEN_TILE
    x = x_ref[...]
    mod = mod_ref[...]
    shift_m, scale_m, gate_m = mod[0:1], mod[1:2], mod[2:3]
    shift_f, scale_f = mod[3:4], mod[4:5]
    nrm = nrm_ref[...]

    h = _rms(x, nrm[0:1] * (1.0 + scale_m)) + shift_m
    u = _dot(h.astype(BF16), w_in_ref[...])
    yield
    xa_raw = u[:, :LRU_WIDTH]
    ga = u[:, LRU_WIDTH:2 * LRU_WIDTH]
    xb = u[:, 2 * LRU_WIDTH:]

    lruv = lruv_ref[...]
    conv_b, gate_a_b, gate_x_b, lam = lruv[0:1], lruv[1:2], lruv[2:3], lruv[3:4]
    convw = convw_ref[...]
    xa_ext[CONV_CARRY:, :] = xa_raw
    ext = xa_ext[...]
    acc = ext * convw[0:1]
    for k in range(1, CONV_WIDTH):
        acc = acc + pltpu.roll(ext, k, axis=0) * convw[k:k + 1]
    xa = acc[CONV_CARRY:, :] + conv_b
    xa_ext[0:CONV_CARRY, :] = xa_raw[ts - CONV_CARRY:, :]

    g = _dot(xa.astype(BF16), wg_ref[...])
    yield
    r = jax.nn.sigmoid(g[:, :LRU_WIDTH] + gate_a_b)
    i = jax.nn.sigmoid(g[:, LRU_WIDTH:] + gate_x_b)
    softplus_neg_lam = jnp.maximum(-lam, 0.0) + jnp.log(1.0 + jnp.exp(-jnp.abs(lam)))
    log_a = -LRU_C * r * softplus_neg_lam
    a = jnp.exp(log_a)
    one_minus_a2 = 1.0 - a * a
    mult = jnp.where(one_minus_a2 > 0.0, one_minus_a2 * lax.rsqrt(one_minus_a2), 0.0)
    bterm = mult * (i * xa)
    h_last = _lru_scan(a, bterm, hcar[0:1, :], hbuf)
    hcar[0:1, :] = h_last
    ya = hbuf[...] * jax.nn.gelu(ga)
    yield

    xb_ext[POOL_CARRY:, :] = xb
    t_glob = s * ts + lax.broadcasted_iota(I32, (ts, POOL_GROUP_DIM), 0)
    pooled = []
    for gi, w in enumerate(POOL_WINDOWS):
        lo = gi * POOL_GROUP_DIM
        e = xb_ext[:, lo:lo + POOL_GROUP_DIM]
        acc_w = e
        span = 1
        while span < w:
            acc_w = acc_w + pltpu.roll(acc_w, span, axis=0)
            span *= 2
        count = jnp.minimum(t_glob + 1, w).astype(F32)
        pooled.append(acc_w[POOL_CARRY:, :] / count - xb[:, lo:lo + POOL_GROUP_DIM])
    xb_ext[0:POOL_CARRY, :] = xb[ts - POOL_CARRY:, :]
    p = jnp.concatenate(pooled, axis=-1)
    poolv = poolv_ref[...]
    yb = (_dot(p.astype(BF16), wp_ref[...]) + poolv[0:1]) * poolv[1:2]
    yield

    mix =_dot(jnp.concatenate([ya, yb], axis=-1).astype(BF16), w_out_ref[...])
    x1 = x + _rms(mix, nrm[1:2] * gate_m)
    x1_ref[...] = x1

    h2 = _rms(x1, nrm[2:3] * (1.0 + scale_f)) + shift_f
    h2_hi, h2_lo = _split_bf16(h2)
    h2_ref[...] = h2_hi
    yield

    hi_terms = _dot(h2_hi, rwh_ref[...])
    lo_term = _dot(h2_lo, rwh_ref[:, 0:LANES])
    logits_t = hi_terms[:, :LANES] + hi_terms[:, LANES:] + lo_term
    logits = jnp.transpose(logits_t)[0:N_EXPERTS, :] + rb_ref[...]

    e_iota = lax.broadcasted_iota(I32, (N_EXPERTS, ts), 0)
    beaten = jnp.zeros((N_EXPERTS, ts), F32)
    for e2 in range(N_EXPERTS):
        other = logits[e2:e2 + 1, :]
        ahead_strict = jnp.where(other > logits, 1.0, 0.0)
        ahead_or_tie = jnp.where(other >= logits, 1.0, 0.0)
        beaten = beaten + jnp.where(e_iota > e2, ahead_or_tie, ahead_strict)
    sels = [beaten == float(k) for k in range(TOP_K)]
    vals = [jnp.sum(jnp.where(sel, logits, 0.0), axis=0, keepdims=True) for sel in sels]
    exps = [jnp.exp(v - vals[0]) for v in vals]
    denom = exps[0] + exps[1] + exps[2] + exps[3]
    gates = [ex / denom for ex in exps]

    member = jnp.where(beaten < float(TOP_K), 1.0, 0.0)
    rank_cnt = _dot(member.astype(BF16), rankw_ref[...])
    rank, n_e = rank_cnt[:, :ts], rank_cnt[:, ts:].astype(I32)
    run_len = ((n_e + (SUBLANES - 1)) >> 3) << 3
    ltri = (lax.broadcasted_iota(I32, (N_EXPERTS, N_EXPERTS), 1)
            < lax.broadcasted_iota(I32, (N_EXPERTS, N_EXPERTS), 0))
    run_off = _dot(jnp.where(ltri, 1.0, 0.0).astype(BF16), run_len.astype(F32).astype(BF16))
    slot = rank + run_off
    pos_rows = [jnp.sum(jnp.where(sel, slot, 0.0), axis=0, keepdims=True) for sel in sels]

    zeros_f = jnp.zeros((SUBLANES - TOP_K, ts), F32)
    gate_ref[...] = jnp.concatenate(gates + [zeros_f], axis=0)
    pos_ref[...] = jnp.concatenate(pos_rows + [zeros_f], axis=0).astype(I32)
    cnt_ref[...] = n_e[:, :LANES]


def _mixer_call(x, mod3, nrm, w_in_b, convw, lruv, wg, poolv, wp, w_out_b, rwh, rb):
    batch, seq, _ = x.shape
    ns = seq // TOKEN_TILE
    t_row = jnp.arange(TOKEN_TILE)[:, None]
    t_col = jnp.arange(2 * TOKEN_TILE)[None, :]
    rankw = jnp.where(t_col >= TOKEN_TILE, 1.0, jnp.where(t_row < t_col, 1.0, 0.0)).astype(BF16)
    grp = MIXER_BATCH_GROUP if batch % MIXER_BATCH_GROUP == 0 else 1
    const = lambda shape: pl.BlockSpec(shape, lambda b, s: (0,) * len(shape))
    return pl.pallas_call(
        _mixer_kernel,
        grid=(batch // grp, ns),
        in_specs=[
            pl.BlockSpec((grp, TOKEN_TILE, D_MODEL), lambda b, s: (b, s, 0)),
            pl.BlockSpec((grp, 6, D_MODEL), lambda b, s: (b, 0, 0)),
            const(nrm.shape), const(w_in_b.shape), const(convw.shape), const(lruv.shape),
            const(wg.shape), const(poolv.shape), const(wp.shape), const(w_out_b.shape),
            const(rwh.shape), const(rb.shape), const(rankw.shape),
        ],
        out_specs=[
            pl.BlockSpec((grp, TOKEN_TILE, D_MODEL), lambda b, s: (b, s, 0)),
            pl.BlockSpec((grp, TOKEN_TILE, D_MODEL), lambda b, s: (b, s, 0)),
            pl.BlockSpec((grp, SUBLANES, TOKEN_TILE), lambda b, s: (b, 0, s)),
            pl.BlockSpec((grp, SUBLANES, TOKEN_TILE), lambda b, s: (b, 0, s)),
            pl.BlockSpec((grp, None, N_EXPERTS, LANES), lambda b, s: (b, s, 0, 0)),
        ],
        out_shape=[
            jax.ShapeDtypeStruct((batch, seq, D_MODEL), F32),
            jax.ShapeDtypeStruct((batch, seq, D_MODEL), BF16),
            jax.ShapeDtypeStruct((batch, SUBLANES, seq), F32),
            jax.ShapeDtypeStruct((batch, SUBLANES, seq), I32),
            jax.ShapeDtypeStruct((batch, ns, N_EXPERTS, LANES), I32),
        ],
        scratch_shapes=[
            pltpu.VMEM((grp, TOKEN_TILE + CONV_CARRY, LRU_WIDTH), F32),
            pltpu.VMEM((grp, TOKEN_TILE + POOL_CARRY, POOL_WIDTH), F32),
            pltpu.VMEM((grp, SUBLANES, LRU_WIDTH), F32),
            pltpu.VMEM((grp, TOKEN_TILE, LRU_WIDTH), F32),
        ],
        compiler_params=pltpu.CompilerParams(
            dimension_semantics=("arbitrary", "arbitrary"), vmem_limit_bytes=VMEM_LIMIT),
        name="mixer",
    )(x, mod3, nrm, w_in_b, convw, lruv, wg, poolv, wp, w_out_b, rwh, rb, rankw)


def _run_copies(length, src_ref, src_off, dst_ref, dst_off, sem, wait=False):
    for size in RUN_SIZES:
        hit = (length & size) != 0

        @pl.when(hit)
        def _(src_off=src_off, dst_off=dst_off, size=size):
            copy = pltpu.make_async_copy(
                src_ref.at[pl.ds(pl.multiple_of(src_off, SUBLANES), size)],
                dst_ref.at[pl.ds(pl.multiple_of(dst_off, SUBLANES), size)],
                sem)
            if wait:
                copy.wait()
            else:
                copy.start()

        step = jnp.where(hit, size, 0)
        src_off = src_off + step
        dst_off = dst_off + step
    return src_off, dst_off


def _plan_index(tile, expert):
    return (tile + 1) * N_EXPERTS + expert


def _wait_rows(rows, hbm_ref, vmem_ref, sem):
    @pl.when(rows > 0)
    def _():
        n = pl.multiple_of(rows, SUBLANES)
        pltpu.make_async_copy(hbm_ref.at[pl.ds(0, n)], vmem_ref.at[pl.ds(0, n)], sem).wait()


def _dispatch_kernel(run_dst_ref, run_len_ref, tile_rows_ref, tail_dst_ref, tail_len_ref, nused_ref,
                     h2_ref, pos_ref, xs_ref, sorted_ref, zero_ref, perm_ref, sem, zsem, *, nt):
    j = pl.program_id(0)
    last = nt - 1
    half = D_MODEL // 2

    @pl.when(j == 0)
    def _():
        zero_ref[...] = jnp.zeros(zero_ref.shape, U32)

        for wait in (False, True):
            def tail_body(e, carry, wait=wait):
                _run_copies(tail_len_ref[e], zero_ref, jnp.int32(0), xs_ref, tail_dst_ref[e], zsem,
                            wait=wait)
                return carry

            lax.fori_loop(0, N_EXPERTS, tail_body, 0)

            def spare_body(b, carry, wait=wait):
                copy = pltpu.make_async_copy(
                    zero_ref, xs_ref.at[pl.ds(pl.multiple_of(b * FFN_BLOCK, FFN_BLOCK), FFN_BLOCK)], zsem)
                if wait:
                    copy.wait()
                else:
                    copy.start()
                return carry

            lax.fori_loop(nused_ref[0], xs_ref.shape[0] // FFN_BLOCK, spare_body, 0)

    slot = j % 2
    stage = sorted_ref.at[slot]

    def send_tile(t, s, unrolled):
        def run_body(e, src_off):
            k = _plan_index(t, e)
            src_off, _ = _run_copies(run_len_ref[k], sorted_ref.at[s], src_off, xs_ref, run_dst_ref[k],
                                     sem.at[s])
            return src_off

        if unrolled:
            src_off = jnp.int32(0)
            for e in range(N_EXPERTS):
                src_off = run_body(e, src_off)
        else:
            lax.fori_loop(0, N_EXPERTS, run_body, 0)

    send_tile(j - 1, 1 - slot, unrolled=True)

    pos = pos_ref[...]
    for r0 in range(0, SORT_ROWS, LANES):
        r_iota = lax.broadcasted_iota(I32, (LANES, TOKEN_TILE), 0) + r0
        perm = jnp.zeros((LANES, TOKEN_TILE), F32)
        for k in range(TOP_K):
            perm = jnp.where(r_iota == pos[k:k + 1, :], 1.0, perm)
        perm_ref[r0:r0 + LANES, :] = perm.astype(BF16)
    rows = _dot(perm_ref[...], h2_ref[...])

    @pl.when(j >= 2)
    def _():
        _wait_rows(tile_rows_ref[j - 2 + 1], xs_ref, stage, sem.at[slot])

    stage[...] = _pack_pair(rows[:, :half], rows[:, half:])

    @pl.when(j == last)
    def _():
        send_tile(j, slot, unrolled=False)
        _wait_rows(tile_rows_ref[j + 1], xs_ref, stage, sem.at[slot])
        if last >= 1:
            _wait_rows(tile_rows_ref[j - 1 + 1], xs_ref, sorted_ref.at[1 - slot], sem.at[1 - slot])


def _dispatch_call(run_dst, run_len, tile_rows, tail_dst, tail_len, nused, h2, pos, cap):
    tokens = h2.shape[0]
    nt = tokens // TOKEN_TILE
    ns = pos.shape[2] // TOKEN_TILE
    half = D_MODEL // 2
    grid_spec = pltpu.PrefetchScalarGridSpec(
        num_scalar_prefetch=6,
        grid=(nt,),
        in_specs=[
            pl.BlockSpec((TOKEN_TILE, D_MODEL), lambda j, *_: (j, 0)),
            pl.BlockSpec((None, SUBLANES, TOKEN_TILE), lambda j, *_: (j // ns, 0, j % ns)),
        ],
        out_specs=pl.BlockSpec(memory_space=pl.ANY),
        scratch_shapes=[
            pltpu.VMEM((2, SORT_ROWS, half), U32),
            pltpu.VMEM((FFN_BLOCK, half), U32),
            pltpu.VMEM((SORT_ROWS, TOKEN_TILE), BF16),
            pltpu.SemaphoreType.DMA((2,)),
            pltpu.SemaphoreType.DMA(()),
        ],
    )
    return pl.pallas_call(
        functools.partial(_dispatch_kernel, nt=nt),
        grid_spec=grid_spec,
        out_shape=jax.ShapeDtypeStruct((cap, half), U32),
        compiler_params=pltpu.CompilerParams(
            dimension_semantics=("arbitrary",), vmem_limit_bytes=VMEM_LIMIT),
        name="dispatch",
    )(run_dst, run_len, tile_rows, tail_dst, tail_len, nused, h2, pos)


W1_CHUNK = 256


def _ffn_kernel(blk_e_ref, nused_ref, next_e_ref, xs_ref, w1_hbm, b1_ref, w2_hbm, b2_ref, y_ref,
                w1_ref, w2_ref, w1s, w2s, wsem):
    i = pl.program_id(0)
    half = D_MODEL // 2
    prev = jnp.maximum(i - 1, 0)
    expert = blk_e_ref[i]
    new_expert = jnp.logical_or(i == 0, expert != blk_e_ref[prev])

    def fetch(e):
        return (pltpu.make_async_copy(w1_hbm.at[e], w1_ref, wsem.at[0]),
                pltpu.make_async_copy(w2_hbm.at[e], w2_ref, wsem.at[1]))

    @pl.when(i == 0)
    def _():
        for copy in fetch(expert):
            copy.start()

    @pl.when(jnp.logical_and(i < nused_ref[0], new_expert))
    def _():
        for copy in fetch(expert):
            copy.wait()
        hc = W1_CHUNK // 2
        r = lax.broadcasted_iota(I32, (W1_CHUNK, W1_CHUNK), 0)
        q = lax.broadcasted_iota(I32, (W1_CHUNK, W1_CHUNK), 1)
        src_col = jnp.where(q < hc, 2 * q, 2 * (q - hc) + 1)
        sel = jnp.where(r == src_col, 1.0, 0.0).astype(BF16)
        for c in range(2 * D_EXPERT // W1_CHUNK):
            chunk = w1_ref[:, c * W1_CHUNK:(c + 1) * W1_CHUNK].astype(BF16)
            picked = _dot(chunk, sel).astype(BF16)
            w1s[:, c * hc:(c + 1) * hc] = picked[:, :hc]
            w1s[:, D_EXPERT + c * hc:D_EXPERT + (c + 1) * hc] = picked[:, hc:]
        w2s[...] = w2_ref[...].astype(BF16)

        nxt = next_e_ref[expert]

        @pl.when(nxt >= 0)
        def _():
            for copy in fetch(nxt):
                copy.start()

    @pl.when(i < nused_ref[0])
    def _():
        x_a, x_b = _unpack_pair(xs_ref[...])
        u = _dot(jnp.concatenate([x_a, x_b], axis=-1), w1s[...]) + b1_ref[...]
        glu = jnp.minimum(u[:, :D_EXPERT], SWIGLU_LIMIT)
        lin = jnp.clip(u[:, D_EXPERT:], -SWIGLU_LIMIT, SWIGLU_LIMIT)
        act = glu * jax.nn.sigmoid(SWIGLU_ALPHA * glu) * (lin + 1.0)
        y = _dot(act.astype(BF16), w2s[...]) + b2_ref[...]
        yb = y.astype(BF16).astype(F32)
        y_ref[...] = _pack_pair(yb[:, :half], yb[:, half:])

    @pl.when(i >= nused_ref[0])
    def _():
        y_ref[...] = jnp.zeros(y_ref.shape, U32)


def _ffn_call(blk_e, nused, next_e, xs, w1, b1p, w2, b2):
    cap, half = xs.shape
    nb = cap // FFN_BLOCK
    row_blk = lambda i, be, nu, ne: (jnp.minimum(i, nu[0] - 1), 0)
    grid_spec = pltpu.PrefetchScalarGridSpec(
        num_scalar_prefetch=3,
        grid=(nb,),
        in_specs=[
            pl.BlockSpec((FFN_BLOCK, half), row_blk),
            pl.BlockSpec(memory_space=pl.ANY),
            pl.BlockSpec((None, 1, 2 * D_EXPERT), lambda i, be, nu, ne: (be[i], 0, 0)),
            pl.BlockSpec(memory_space=pl.ANY),
            pl.BlockSpec((None, 1, D_MODEL), lambda i, be, nu, ne: (be[i], 0, 0)),
        ],
        out_specs=pl.BlockSpec((FFN_BLOCK, half), lambda i, be, nu, ne: (i, 0)),
        scratch_shapes=[
            pltpu.VMEM((D_MODEL, 2 * D_EXPERT), F32),
            pltpu.VMEM((D_EXPERT, D_MODEL), F32),
            pltpu.VMEM((D_MODEL, 2 * D_EXPERT), BF16),
            pltpu.VMEM((D_EXPERT, D_MODEL), BF16),
            pltpu.SemaphoreType.DMA((2,)),
        ],
    )
    return pl.pallas_call(
        _ffn_kernel,
        grid_spec=grid_spec,
        out_shape=jax.ShapeDtypeStruct((cap, half), U32),
        compiler_params=pltpu.CompilerParams(
            dimension_semantics=("arbitrary",), vmem_limit_bytes=VMEM_LIMIT),
        name="ffn",
    )(blk_e, nused, next_e, xs, w1, b1p, w2, b2)


def _combine_kernel(run_dst_ref, run_len_ref, tile_rows_ref, y_ref, pos_ref, gate_ref, x1_ref, mod_ref,
                    nrm_ref, out_ref, ybuf, w_hi_ref, w_lo_ref, y_a_ref, y_b_ref, sem, *, nt):
    j = pl.program_id(0)
    slot = j % 2

    def fetch_tile(t, s, unrolled):
        def run_body(e, dst_off):
            k = _plan_index(t, e)
            _, dst_off = _run_copies(run_len_ref[k], y_ref, run_dst_ref[k], ybuf.at[s], dst_off,
                                     sem.at[s])
            return dst_off

        if unrolled:
            dst_off = jnp.int32(0)
            for e in range(N_EXPERTS):
                dst_off = run_body(e, dst_off)
        else:
            lax.fori_loop(0, N_EXPERTS, run_body, 0)

    @pl.when(j == 0)
    def _():
        ybuf[...] = jnp.zeros(ybuf.shape, U32)
        fetch_tile(j, slot, unrolled=False)

    fetch_tile(j + 1, 1 - slot, unrolled=True)

    pos_t = jnp.transpose(pos_ref[...].astype(F32))
    gate_t = jnp.transpose(gate_ref[...])
    for c0 in range(0, SORT_ROWS, LANES):
        c_iota = (lax.broadcasted_iota(I32, (TOKEN_TILE, LANES), 1) + c0).astype(F32)
        wsel = jnp.zeros((TOKEN_TILE, LANES), F32)
        for k in range(TOP_K):
            wsel = jnp.where(c_iota == pos_t[:, k:k + 1], gate_t[:, k:k + 1], wsel)
        hi, lo = _split_bf16(wsel)
        w_hi_ref[:, c0:c0 + LANES] = hi
        w_lo_ref[:, c0:c0 + LANES] = lo

    _wait_rows(tile_rows_ref[j + 1], y_ref, ybuf.at[slot], sem.at[slot])
    for r0 in range(0, SORT_ROWS, UNPACK_ROWS):
        a, b = _unpack_pair(ybuf[slot, r0:r0 + UNPACK_ROWS, :])
        y_a_ref[r0:r0 + UNPACK_ROWS, :] = a
        y_b_ref[r0:r0 + UNPACK_ROWS, :] = b
    w_hi, w_lo = w_hi_ref[...], w_lo_ref[...]
    y_a, y_b = y_a_ref[...], y_b_ref[...]
    f = jnp.concatenate([_dot(w_hi, y_a) + _dot(w_lo, y_a), _dot(w_hi, y_b) + _dot(w_lo, y_b)],
                        axis=-1)
    gate_f = mod_ref[5:6, :]
    out_ref[...] = x1_ref[...] + _rms(f, nrm_ref[...] * gate_f)


def _combine_call(run_dst, run_len, tile_rows, y, pos, gate, x1, mod3, nrm_post):
    batch, seq, _ = x1.shape
    ns = seq // TOKEN_TILE
    nt = batch * ns
    half = D_MODEL // 2
    grid_spec = pltpu.PrefetchScalarGridSpec(
        num_scalar_prefetch=3,
        grid=(nt,),
        in_specs=[
            pl.BlockSpec(memory_space=pl.ANY),
            pl.BlockSpec((None, SUBLANES, TOKEN_TILE), lambda j, *_: (j // ns, 0, j % ns)),
            pl.BlockSpec((None, SUBLANES, TOKEN_TILE), lambda j, *_: (j // ns, 0, j % ns)),
            pl.BlockSpec((None, TOKEN_TILE, D_MODEL), lambda j, *_: (j // ns, j % ns, 0)),
            pl.BlockSpec((None, 6, D_MODEL), lambda j, *_: (j // ns, 0, 0)),
            pl.BlockSpec((1, D_MODEL), lambda j, *_: (0, 0)),
        ],
        out_specs=pl.BlockSpec((None, TOKEN_TILE, D_MODEL), lambda j, *_: (j // ns, j % ns, 0)),
        scratch_shapes=[
            pltpu.VMEM((2, SORT_ROWS, half), U32),
            pltpu.VMEM((TOKEN_TILE, SORT_ROWS), BF16),
            pltpu.VMEM((TOKEN_TILE, SORT_ROWS), BF16),
            pltpu.VMEM((SORT_ROWS, half), BF16),
            pltpu.VMEM((SORT_ROWS, half), BF16),
            pltpu.SemaphoreType.DMA((2,)),
        ],
    )
    return pl.pallas_call(
        functools.partial(_combine_kernel, nt=nt),
        grid_spec=grid_spec,
        out_shape=jax.ShapeDtypeStruct((batch, seq, D_MODEL), F32),
        compiler_params=pltpu.CompilerParams(
            dimension_semantics=("arbitrary",), vmem_limit_bytes=VMEM_LIMIT),
        name="combine",
    )(run_dst, run_len, tile_rows, y, pos, gate, x1, mod3, nrm_post)


def _block_diag(w):
    heads, d, _ = w.shape
    n = heads * d
    tiled = jnp.tile(w.reshape(n, d), (1, heads))
    same_head = (jnp.arange(n)[:, None] // d) == (jnp.arange(n)[None, :] // d)
    return jnp.where(same_head, tiled, 0.0)


def _route_plan(cnt):
    nt = cnt.shape[0]
    run_len = (cnt + (SUBLANES - 1)) // SUBLANES * SUBLANES
    region = jnp.sum(run_len, axis=0)
    region_blk = (region + FFN_BLOCK - 1) // FFN_BLOCK * FFN_BLOCK
    earlier_e = jnp.arange(N_EXPERTS)[None, :] < jnp.arange(N_EXPERTS)[:, None]
    region_start = jnp.sum(jnp.where(earlier_e, region_blk[None, :], 0), axis=1)
    region_end = region_start + region_blk
    earlier_t = jnp.arange(nt)[None, :] < jnp.arange(nt)[:, None]
    run_before = jnp.sum(jnp.where(earlier_t[:, :, None], run_len[None, :, :], 0), axis=1)
    run_dst = region_start[None, :] + run_before
    tail_dst = region_start + region
    tail_len = region_blk - region
    total_rows = jnp.sum(region_blk)
    nused = total_rows // FFN_BLOCK
    max_rows = nt * TOKEN_TILE * TOP_K + nt * N_EXPERTS * (SUBLANES - 1) + N_EXPERTS * (FFN_BLOCK - SUBLANES)
    nb = -(-max_rows // FFN_BLOCK)
    blk_start = jnp.arange(nb, dtype=I32) * FFN_BLOCK
    blk_row = jnp.minimum(blk_start, total_rows - 1)
    blk_e = jnp.sum((blk_row[:, None] >= region_end[None, :]).astype(I32), axis=1)
    blk_e = jnp.minimum(blk_e, N_EXPERTS - 1).astype(I32)
    later_used = jnp.logical_and(jnp.arange(N_EXPERTS)[None, :] > jnp.arange(N_EXPERTS)[:, None],
                                 region_blk[None, :] > 0)
    next_e = jnp.min(jnp.where(later_used, jnp.arange(N_EXPERTS)[None, :], N_EXPERTS), axis=1)
    next_e = jnp.where(next_e == N_EXPERTS, -1, next_e)
    tile_rows = jnp.sum(run_len, axis=1)
    edge = ((1, 1), (0, 0))
    run_dst, run_len, tile_rows = jnp.pad(run_dst, edge), jnp.pad(run_len, edge), jnp.pad(tile_rows, 1)
    return dict(run_dst=run_dst.reshape(-1).astype(I32), run_len=run_len.reshape(-1).astype(I32),
                tile_rows=tile_rows.astype(I32), tail_dst=tail_dst.astype(I32),
                tail_len=tail_len.astype(I32), blk_e=blk_e, next_e=next_e.astype(I32),
                nused=nused.reshape(1).astype(I32)), nb * FFN_BLOCK


def kernel(x, c, w_ada, b_ada, norm_mix_pre, norm_mix_post, w_in, conv_w, conv_b, gate_a_w, gate_a_b, gate_x_w, gate_x_b, lru_lambda, pool_w, pool_b, pool_scale, w_out, norm_ffn_pre, norm_ffn_post, router_w, router_b, expert_w1, expert_b1, expert_w2, expert_b2):
    depth = w_ada.shape[0]
    batch = x.shape[0]
    for l in range(depth):
        mod3 = _ada_call(c, w_ada[l], b_ada[l]).reshape(batch, 6, D_MODEL)

        nrm = jnp.stack([norm_mix_pre[l], norm_mix_post[l], norm_ffn_pre[l]])
        lruv = jnp.stack([conv_b[l], gate_a_b[l], gate_x_b[l], lru_lambda[l]])
        wg = jnp.concatenate([_block_diag(gate_a_w[l]), _block_diag(gate_x_w[l])], axis=1).astype(BF16)
        poolv = jnp.stack([pool_b[l], pool_scale[l]])
        wp = _block_diag(pool_w[l]).astype(BF16)
        rw_pad = jnp.pad(router_w[l], ((0, 0), (0, LANES - N_EXPERTS)))
        rw_hi = rw_pad.astype(BF16)
        rw_lo = (rw_pad - rw_hi.astype(F32)).astype(BF16)
        rwh = jnp.concatenate([rw_hi, rw_lo], axis=1)
        rb = router_b[l].reshape(N_EXPERTS, 1)

        x1, h2, gate, pos, cnt = _mixer_call(
            x, mod3, nrm, w_in[l].astype(BF16), conv_w[l], lruv, wg, poolv, wp,
            w_out[l].astype(BF16), rwh, rb)

        plan, cap = _route_plan(cnt[:, :, :, 0].reshape(-1, N_EXPERTS))
        xs = _dispatch_call(plan["run_dst"], plan["run_len"], plan["tile_rows"], plan["tail_dst"],
                            plan["tail_len"], plan["nused"], h2.reshape(-1, D_MODEL), pos, cap)

        b1p = jnp.concatenate([expert_b1[l][:, 0::2], expert_b1[l][:, 1::2]], axis=-1)
        y = _ffn_call(plan["blk_e"], plan["nused"], plan["next_e"], xs, expert_w1[l],
                      b1p.reshape(N_EXPERTS, 1, 2 * D_EXPERT),
                      expert_w2[l], expert_b2[l].reshape(N_EXPERTS, 1, D_MODEL))

        x = _combine_call(plan["run_dst"], plan["run_len"], plan["tile_rows"], y, pos, gate, x1, mod3,
                          norm_ffn_post[l].reshape(1, D_MODEL))
    return x
```

```python
import functools

import jax
import jax.numpy as jnp
from jax import lax
from jax.experimental import pallas as pl
from jax.experimental.pallas import tpu as pltpu

F32 = jnp.float32
BF16 = jnp.bfloat16
I32 = jnp.int32
U32 = jnp.uint32

D_MODEL = 1024
LRU_WIDTH = 512
LRU_HEADS = 8
POOL_WIDTH = 512
POOL_WINDOWS = (2, 4, 8, 16)
POOL_GROUP_DIM = 128
CONV_WIDTH = 4
IN_WIDTH = 2 * LRU_WIDTH + POOL_WIDTH
LRU_C = 8.0
N_EXPERTS = 32
TOP_K = 4
D_EXPERT = 1024
SWIGLU_LIMIT = 7.0
SWIGLU_ALPHA = 1.702
NORM_EPS = 1e-6

SUBLANES = 8
LANES = 128
TOKEN_TILE = 256
FFN_BLOCK = 256
MIXER_BATCH_GROUP = 4
MIXER_PHASES = 6
CONV_CARRY = SUBLANES
POOL_CARRY = 16
SORT_ROWS = 1280
RUN_SIZES = (256, 128, 64, 32, 16, 8)
UNPACK_ROWS = 64
VMEM_LIMIT = 52 * 1024 * 1024


def _rms(v, g):
    return v * lax.rsqrt(jnp.mean(v * v, axis=-1, keepdims=True) + NORM_EPS) * g


def _dot(a, b):
    return jnp.dot(a, b, preferred_element_type=F32)


def _dot_nt(a, b):
    return lax.dot_general(a, b, (((1,), (1,)), ((), ())), preferred_element_type=F32)


def _split_bf16(v):
    hi = v.astype(BF16)
    lo = (v - hi.astype(F32)).astype(BF16)
    return hi, lo


def _pack_pair(a, b):
    ua = lax.bitcast_convert_type(a, U32)
    ub = lax.bitcast_convert_type(b, U32)
    return (ua >> 16) | (ub & jnp.uint32(0xFFFF0000))


def _unpack_pair(w):
    a = lax.bitcast_convert_type(w << 16, F32).astype(BF16)
    b = lax.bitcast_convert_type(w & jnp.uint32(0xFFFF0000), F32).astype(BF16)
    return a, b


def _ada_kernel(c_ref, w_ref, b_ref, o_ref):
    c = c_ref[...]
    ca = c * jax.nn.sigmoid(c)
    o_ref[...] = jnp.dot(ca, w_ref[...], preferred_element_type=F32,
                         precision=lax.Precision.HIGHEST) + b_ref[...]


def _ada_call(c, w_ada, b_ada):
    batch = c.shape[0]
    n_out = w_ada.shape[1]
    bn = D_MODEL
    return pl.pallas_call(
        _ada_kernel,
        grid=(n_out // bn,),
        in_specs=[
            pl.BlockSpec((batch, D_MODEL), lambda j: (0, 0)),
            pl.BlockSpec((D_MODEL, bn), lambda j: (0, j)),
            pl.BlockSpec((1, bn), lambda j: (0, j)),
        ],
        out_specs=pl.BlockSpec((batch, bn), lambda j: (0, j)),
        out_shape=jax.ShapeDtypeStruct((batch, n_out), F32),
        name="ada",
    )(c, w_ada, b_ada.reshape(1, n_out))


def _lru_scan(a, b, h0, out_ref):
    ts, c = a.shape
    groups = ts // SUBLANES
    a3 = a.reshape(groups, SUBLANES, c)
    b3 = b.reshape(groups, SUBLANES, c)
    row = lax.broadcasted_iota(I32, (groups, SUBLANES, c), 1)
    d = 1
    while d < SUBLANES:
        a_sh = pltpu.roll(a3, d, axis=1)
        b_sh = pltpu.roll(b3, d, axis=1)
        m = row >= d
        b3 = jnp.where(m, a3 * b_sh + b3, b3)
        a3 = jnp.where(m, a3 * a_sh, a3)
        d *= 2
    h = h0
    for g in range(groups):
        hg = a3[g] * h + b3[g]
        out_ref[pl.ds(g * SUBLANES, SUBLANES), :] = hg
        h = hg[SUBLANES - 1:SUBLANES, :]
    return h


def _mixer_kernel(x_ref, mod_ref, nrm_ref, w_in_ref, convw_ref, lruv_ref, wg_ref,
                  poolv_ref, wp_ref, w_out_ref, rwh_ref, rb_ref, rankw_ref,
                  x1_ref, h2_ref, gate_ref, pos_ref, cnt_ref,
                  xa_ext, xb_ext, hcar, hbuf):
    s = pl.program_id(1)
    shared = (nrm_ref, w_in_ref, convw_ref, lruv_ref, wg_ref, poolv_ref, wp_ref, w_out_ref,
              rwh_ref, rb_ref, rankw_ref)
    per_row = (x_ref, mod_ref, x1_ref, h2_ref, gate_ref, pos_ref, cnt_ref, xa_ext, xb_ext, hcar, hbuf)

    @pl.when(s == 0)
    def _():
        xa_ext[:, 0:CONV_CARRY, :] = jnp.zeros((xa_ext.shape[0], CONV_CARRY, LRU_WIDTH), F32)
        xb_ext[:, 0:POOL_CARRY, :] = jnp.zeros((xb_ext.shape[0], POOL_CARRY, POOL_WIDTH), F32)
        hcar[...] = jnp.zeros(hcar.shape, F32)

    tiles = [_mixer_tile(s, shared, *(ref.at[p] for ref in per_row)) for p in range(x_ref.shape[0])]
    _skewed(tiles, MIXER_PHASES - 1, finish=False)
    for tile in tiles:
        assert next(tile, "done") == "done"


def _mixer_tile(s, shared, x_ref, mod_ref, x1_ref, h2_ref, gate_ref, pos_ref, cnt_ref,
                xa_ext, xb_ext, hcar, hbuf):
    (nrm_ref, w_in_ref, convw_ref, lruv_ref, wg_ref, poolv_ref, wp_ref, w_out_ref,
     rwh_ref, rb_ref, rankw_ref) = shared
    ts = TOKEN_TILE
    x = x_ref[...]
    mod = mod_ref[...]
    shift_m, scale_m, gate_m = mod[0:1], mod[1:2], mod[2:3]
    shift_f, scale_f = mod[3:4], mod[4:5]
    nrm = nrm_ref[...]

    h = _rms(x, nrm[0:1] * (1.0 + scale_m)) + shift_m
    u = _dot(h.astype(BF16), w_in_ref[...])
    yield
    xa_raw = u[:, :LRU_WIDTH]
    ga = u[:, LRU_WIDTH:2 * LRU_WIDTH]
    xb = u[:, 2 * LRU_WIDTH:]

    lruv = lruv_ref[...]
    conv_b, gate_a_b, gate_x_b, lam = lruv[0:1], lruv[1:2], lruv[2:3], lruv[3:4]
    convw = convw_ref[...]
    xa_ext[CONV_CARRY:, :] = xa_raw
    ext = xa_ext[...]
    acc = ext * convw[0:1]
    for k in range(1, CONV_WIDTH):
        acc = acc + pltpu.roll(ext, k, axis=0) * convw[k:k + 1]
    xa = acc[CONV_CARRY:, :] + conv_b
    xa_ext[0:CONV_CARRY, :] = xa_raw[ts - CONV_CARRY:, :]

    g = _dot(xa.astype(BF16), wg_ref[...])
    yield
    r = jax.nn.sigmoid(g[:, :LRU_WIDTH] + gate_a_b)
    i = jax.nn.sigmoid(g[:, LRU_WIDTH:] + gate_x_b)
    softplus_neg_lam = jnp.maximum(-lam, 0.0) + jnp.log(1.0 + jnp.exp(-jnp.abs(lam)))
    log_a = -LRU_C * r * softplus_neg_lam
    a = jnp.exp(log_a)
    one_minus_a2 = 1.0 - a * a
    mult = jnp.where(one_minus_a2 > 0.0, one_minus_a2 * lax.rsqrt(one_minus_a2), 0.0)
    bterm = mult * (i * xa)
    h_last = _lru_scan(a, bterm, hcar[0:1, :], hbuf)
    hcar[0:1, :] = h_last
    ya = hbuf[...] * jax.nn.gelu(ga)
    yield

    xb_ext[POOL_CARRY:, :] = xb
    t_glob = s * ts + lax.broadcasted_iota(I32, (ts, POOL_GROUP_DIM), 0)
    pooled = []
    for gi, w in enumerate(POOL_WINDOWS):
        lo = gi * POOL_GROUP_DIM
        e = xb_ext[:, lo:lo + POOL_GROUP_DIM]
        acc_w = e
        span = 1
        while span < w:
            acc_w = acc_w + pltpu.roll(acc_w, span, axis=0)
            span *= 2
        count = jnp.minimum(t_glob + 1, w).astype(F32)
        pooled.append(acc_w[POOL_CARRY:, :] / count - xb[:, lo:lo + POOL_GROUP_DIM])
    xb_ext[0:POOL_CARRY, :] = xb[ts - POOL_CARRY:, :]
    p = jnp.concatenate(pooled, axis=-1)
    poolv = poolv_ref[...]
    yb = (_dot(p.astype(BF16), wp_ref[...]) + poolv[0:1]) * poolv[1:2]
    yield

    mix =_dot(jnp.concatenate([ya, yb], axis=-1).astype(BF16), w_out_ref[...])
    x1 = x + _rms(mix, nrm[1:2] * gate_m)
    x1_ref[...] = x1

    h2 = _rms(x1, nrm[2:3] * (1.0 + scale_f)) + shift_f
    h2_hi, h2_lo = _split_bf16(h2)
    h2_ref[...] = h2_hi
    yield

    hi_terms = _dot(h2_hi, rwh_ref[...])
    lo_term = _dot(h2_lo, rwh_ref[:, 0:LANES])
    logits_t = hi_terms[:, :LANES] + hi_terms[:, LANES:] + lo_term
    logits = jnp.transpose(logits_t)[0:N_EXPERTS, :] + rb_ref[...]

    e_iota = lax.broadcasted_iota(I32, (N_EXPERTS, ts), 0)
    beaten = jnp.zeros((N_EXPERTS, ts), F32)
    for e2 in range(N_EXPERTS):
        other = logits[e2:e2 + 1, :]
        ahead_strict = jnp.where(other > logits, 1.0, 0.0)
        ahead_or_tie = jnp.where(other >= logits, 1.0, 0.0)
        beaten = beaten + jnp.where(e_iota > e2, ahead_or_tie, ahead_strict)
    sels = [beaten == float(k) for k in range(TOP_K)]
    vals = [jnp.sum(jnp.where(sel, logits, 0.0), axis=0, keepdims=True) for sel in sels]
    exps = [jnp.exp(v - vals[0]) for v in vals]
    denom = exps[0] + exps[1] + exps[2] + exps[3]
    gates = [ex / denom for ex in exps]

    member = jnp.where(beaten < float(TOP_K), 1.0, 0.0)
    rank_cnt = _dot(member.astype(BF16), rankw_ref[...])
    rank, n_e = rank_cnt[:, :ts], rank_cnt[:, ts:].astype(I32)
    run_len = ((n_e + (SUBLANES - 1)) >> 3) << 3
    ltri = (lax.broadcasted_iota(I32, (N_EXPERTS, N_EXPERTS), 1)
            < lax.broadcasted_iota(I32, (N_EXPERTS, N_EXPERTS), 0))
    run_off = _dot(jnp.where(ltri, 1.0, 0.0).astype(BF16), run_len.astype(F32).astype(BF16))
    slot = rank + run_off
    pos_rows = [jnp.sum(jnp.where(sel, slot, 0.0), axis=0, keepdims=True) for sel in sels]

    zeros_f = jnp.zeros((SUBLANES - TOP_K, ts), F32)
    gate_ref[...] = jnp.concatenate(gates + [zeros_f], axis=0)
    pos_ref[...] = jnp.concatenate(pos_rows + [zeros_f], axis=0).astype(I32)
    cnt_ref[...] = n_e[:, :LANES]


def _mixer_call(x, mod3, nrm, w_in_b, convw, lruv, wg, poolv, wp, w_out_b, rwh, rb):
    batch, seq, _ = x.shape
    ns = seq // TOKEN_TILE
    t_row = jnp.arange(TOKEN_TILE)[:, None]
    t_col = jnp.arange(2 * TOKEN_TILE)[None, :]
    rankw = jnp.where(t_col >= TOKEN_TILE, 1.0, jnp.where(t_row < t_col, 1.0, 0.0)).astype(BF16)
    grp = MIXER_BATCH_GROUP if batch % MIXER_BATCH_GROUP == 0 else 1
    const = lambda shape: pl.BlockSpec(shape, lambda b, s: (0,) * len(shape))
    return pl.pallas_call(
        _mixer_kernel,
        grid=(batch // grp, ns),
        in_specs=[
            pl.BlockSpec((grp, TOKEN_TILE, D_MODEL), lambda b, s: (b, s, 0)),
            pl.BlockSpec((grp, 6, D_MODEL), lambda b, s: (b, 0, 0)),
            const(nrm.shape), const(w_in_b.shape), const(convw.shape), const(lruv.shape),
            const(wg.shape), const(poolv.shape), const(wp.shape), const(w_out_b.shape),
            const(rwh.shape), const(rb.shape), const(rankw.shape),
        ],
        out_specs=[
            pl.BlockSpec((grp, TOKEN_TILE, D_MODEL), lambda b, s: (b, s, 0)),
            pl.BlockSpec((grp, TOKEN_TILE, D_MODEL), lambda b, s: (b, s, 0)),
            pl.BlockSpec((grp, SUBLANES, TOKEN_TILE), lambda b, s: (b, 0, s)),
            pl.BlockSpec((grp, SUBLANES, TOKEN_TILE), lambda b, s: (b, 0, s)),
            pl.BlockSpec((grp, None, N_EXPERTS, LANES), lambda b, s: (b, s, 0, 0)),
        ],
        out_shape=[
            jax.ShapeDtypeStruct((batch, seq, D_MODEL), F32),
            jax.ShapeDtypeStruct((batch, seq, D_MODEL), BF16),
            jax.ShapeDtypeStruct((batch, SUBLANES, seq), F32),
            jax.ShapeDtypeStruct((batch, SUBLANES, seq), I32),
            jax.ShapeDtypeStruct((batch, ns, N_EXPERTS, LANES), I32),
        ],
        scratch_shapes=[
            pltpu.VMEM((grp, TOKEN_TILE + CONV_CARRY, LRU_WIDTH), F32),
            pltpu.VMEM((grp, TOKEN_TILE + POOL_CARRY, POOL_WIDTH), F32),
            pltpu.VMEM((grp, SUBLANES, LRU_WIDTH), F32),
            pltpu.VMEM((grp, TOKEN_TILE, LRU_WIDTH), F32),
        ],
        compiler_params=pltpu.CompilerParams(
            dimension_semantics=("arbitrary", "arbitrary"), vmem_limit_bytes=VMEM_LIMIT),
        name="mixer",
    )(x, mod3, nrm, w_in_b, convw, lruv, wg, poolv, wp, w_out_b, rwh, rb, rankw)


def _run_copies(length, src_ref, src_off, dst_ref, dst_off, sem, wait=False):
    for size in RUN_SIZES:
        hit = (length & size) != 0

        @pl.when(hit)
        def _(src_off=src_off, dst_off=dst_off, size=size):
            copy = pltpu.make_async_copy(
                src_ref.at[pl.ds(pl.multiple_of(src_off, SUBLANES), size)],
                dst_ref.at[pl.ds(pl.multiple_of(dst_off, SUBLANES), size)],
                sem)
            if wait:
                copy.wait()
            else:
                copy.start()

        step = jnp.where(hit, size, 0)
        src_off = src_off + step
        dst_off = dst_off + step
    return src_off, dst_off


DISPATCH_TILES = 2
COMBINE_TILES = 2
PLAN_PAD = 2 * max(DISPATCH_TILES, COMBINE_TILES)


def _plan_index(tile, expert):
    return (tile + PLAN_PAD) * N_EXPERTS + expert


def _wait_rows(rows, hbm_ref, vmem_ref, sem):
    @pl.when(rows > 0)
    def _():
        n = pl.multiple_of(rows, SUBLANES)
        pltpu.make_async_copy(hbm_ref.at[pl.ds(0, n)], vmem_ref.at[pl.ds(0, n)], sem).wait()


def _dispatch_kernel(run_dst_ref, run_len_ref, tile_rows_ref, tail_dst_ref, tail_len_ref, nused_ref,
                     h2_ref, pos_ref, xs_ref, sorted_ref, zero_ref, perm_ref, sem, zsem, *, steps):
    j = pl.program_id(0)
    last = steps - 1
    half = D_MODEL // 2

    @pl.when(j == 0)
    def _():
        zero_ref[...] = jnp.zeros(zero_ref.shape, U32)

        for wait in (False, True):
            def tail_body(e, carry, wait=wait):
                _run_copies(tail_len_ref[e], zero_ref, jnp.int32(0), xs_ref, tail_dst_ref[e], zsem,
                            wait=wait)
                return carry

            lax.fori_loop(0, N_EXPERTS, tail_body, 0)

            def spare_body(b, carry, wait=wait):
                copy = pltpu.make_async_copy(
                    zero_ref, xs_ref.at[pl.ds(pl.multiple_of(b * FFN_BLOCK, FFN_BLOCK), FFN_BLOCK)], zsem)
                if wait:
                    copy.wait()
                else:
                    copy.start()
                return carry

            lax.fori_loop(nused_ref[0], xs_ref.shape[0] // FFN_BLOCK, spare_body, 0)

    cur = j % 2
    tps = DISPATCH_TILES

    def send_expert(t, s, i, e, src_off):
        k = _plan_index(t, e)
        src_off, _ = _run_copies(run_len_ref[k], sorted_ref.at[s, i], src_off, xs_ref,
                                 run_dst_ref[k], sem.at[s, i])
        return src_off

    def send_tile_stepwise(t, s, i):
        src_off = jnp.int32(0)
        for e in range(N_EXPERTS):
            src_off = send_expert(t, s, i, e, src_off)
            yield

    def wait_tile(t, s, i):
        _wait_rows(tile_rows_ref[t + PLAN_PAD], xs_ref, sorted_ref.at[s, i], sem.at[s, i])

    for i in range(tps):
        wait_tile((j - 2) * tps + i, cur, i)
    def sort_tile(i):
        cols = slice(i * TOKEN_TILE, (i + 1) * TOKEN_TILE)
        send = send_tile_stepwise((j - 1) * tps + i, 1 - cur, i)
        pos = pos_ref[:, cols]
        for r0 in range(0, SORT_ROWS, LANES):
            r_iota = lax.broadcasted_iota(I32, (LANES, TOKEN_TILE), 0) + r0
            perm = jnp.zeros((LANES, TOKEN_TILE), F32)
            for k in range(TOP_K):
                perm = jnp.where(r_iota == pos[k:k + 1, :], 1.0, perm)
            perm_ref[i, r0:r0 + LANES, :] = perm.astype(BF16)
            for _ in range(4):
                next(send, None)
        yield
        rows = _dot(perm_ref[i], h2_ref[cols, :])
        yield
        sorted_ref[cur, i] = _pack_pair(rows[:, :half], rows[:, half:])

    _skewed([sort_tile(i) for i in range(tps)], phases=3)

    @pl.when(j == last)
    def _():
        for i in range(tps):
            lax.fori_loop(0, N_EXPERTS, functools.partial(send_expert, j * tps + i, cur, i), 0)
        for i in range(tps):
            wait_tile(j * tps + i, cur, i)
            if last >= 1:
                wait_tile((j - 1) * tps + i, 1 - cur, i)


def _dispatch_call(run_dst, run_len, tile_rows, tail_dst, tail_len, nused, h2, pos, cap):
    tokens = h2.shape[0]
    tps = DISPATCH_TILES
    span = tps * TOKEN_TILE
    steps = tokens // span
    per_row = pos.shape[2] // span
    half = D_MODEL // 2
    grid_spec = pltpu.PrefetchScalarGridSpec(
        num_scalar_prefetch=6,
        grid=(steps,),
        in_specs=[
            pl.BlockSpec((span, D_MODEL), lambda j, *_: (j, 0)),
            pl.BlockSpec((None, SUBLANES, span), lambda j, *_: (j // per_row, 0, j % per_row)),
        ],
        out_specs=pl.BlockSpec(memory_space=pl.ANY),
        scratch_shapes=[
            pltpu.VMEM((2, tps, SORT_ROWS, half), U32),
            pltpu.VMEM((FFN_BLOCK, half), U32),
            pltpu.VMEM((tps, SORT_ROWS, TOKEN_TILE), BF16),
            pltpu.SemaphoreType.DMA((2, tps)),
            pltpu.SemaphoreType.DMA(()),
        ],
    )
    return pl.pallas_call(
        functools.partial(_dispatch_kernel, steps=steps),
        grid_spec=grid_spec,
        out_shape=jax.ShapeDtypeStruct((cap, half), U32),
        compiler_params=pltpu.CompilerParams(
            dimension_semantics=("arbitrary",), vmem_limit_bytes=VMEM_LIMIT),
        name="dispatch",
    )(run_dst, run_len, tile_rows, tail_dst, tail_len, nused, h2, pos)


W1_CHUNK = 256


def _ffn_kernel(blk_e_ref, nused_ref, next_e_ref, xs_ref, w1_hbm, b1_ref, w2_hbm, b2_ref, y_ref,
                w1_ref, w2_ref, w1s, w2s, wsem):
    i = pl.program_id(0)
    half = D_MODEL // 2
    prev = jnp.maximum(i - 1, 0)
    expert = blk_e_ref[i]
    new_expert = jnp.logical_or(i == 0, expert != blk_e_ref[prev])

    def fetch(e):
        return (pltpu.make_async_copy(w1_hbm.at[e], w1_ref, wsem.at[0]),
                pltpu.make_async_copy(w2_hbm.at[e], w2_ref, wsem.at[1]))

    @pl.when(i == 0)
    def _():
        for copy in fetch(expert):
            copy.start()

    @pl.when(jnp.logical_and(i < nused_ref[0], new_expert))
    def _():
        for copy in fetch(expert):
            copy.wait()
        hc = W1_CHUNK // 2
        r = lax.broadcasted_iota(I32, (W1_CHUNK, W1_CHUNK), 0)
        q = lax.broadcasted_iota(I32, (W1_CHUNK, W1_CHUNK), 1)
        src_col = jnp.where(q < hc, 2 * q, 2 * (q - hc) + 1)
        sel = jnp.where(r == src_col, 1.0, 0.0).astype(BF16)
        for c in range(2 * D_EXPERT // W1_CHUNK):
            chunk = w1_ref[:, c * W1_CHUNK:(c + 1) * W1_CHUNK].astype(BF16)
            picked = _dot(chunk, sel).astype(BF16)
            w1s[:, c * hc:(c + 1) * hc] = picked[:, :hc]
            w1s[:, D_EXPERT + c * hc:D_EXPERT + (c + 1) * hc] = picked[:, hc:]
        w2s[...] = w2_ref[...].astype(BF16)

        nxt = next_e_ref[expert]

        @pl.when(nxt >= 0)
        def _():
            for copy in fetch(nxt):
                copy.start()

    @pl.when(i < nused_ref[0])
    def _():
        x_a, x_b = _unpack_pair(xs_ref[...])
        u = _dot(jnp.concatenate([x_a, x_b], axis=-1), w1s[...]) + b1_ref[...]
        glu = jnp.minimum(u[:, :D_EXPERT], SWIGLU_LIMIT)
        lin = jnp.clip(u[:, D_EXPERT:], -SWIGLU_LIMIT, SWIGLU_LIMIT)
        act = glu * jax.nn.sigmoid(SWIGLU_ALPHA * glu) * (lin + 1.0)
        y = _dot(act.astype(BF16), w2s[...]) + b2_ref[...]
        yb = y.astype(BF16).astype(F32)
        y_ref[...] = _pack_pair(yb[:, :half], yb[:, half:])

    @pl.when(i >= nused_ref[0])
    def _():
        y_ref[...] = jnp.zeros(y_ref.shape, U32)


def _ffn_call(blk_e, nused, next_e, xs, w1, b1p, w2, b2):
    cap, half = xs.shape
    nb = cap // FFN_BLOCK
    row_blk = lambda i, be, nu, ne: (jnp.minimum(i, nu[0] - 1), 0)
    grid_spec = pltpu.PrefetchScalarGridSpec(
        num_scalar_prefetch=3,
        grid=(nb,),
        in_specs=[
            pl.BlockSpec((FFN_BLOCK, half), row_blk),
            pl.BlockSpec(memory_space=pl.ANY),
            pl.BlockSpec((None, 1, 2 * D_EXPERT), lambda i, be, nu, ne: (be[i], 0, 0)),
            pl.BlockSpec(memory_space=pl.ANY),
            pl.BlockSpec((None, 1, D_MODEL), lambda i, be, nu, ne: (be[i], 0, 0)),
        ],
        out_specs=pl.BlockSpec((FFN_BLOCK, half), lambda i, be, nu, ne: (i, 0)),
        scratch_shapes=[
            pltpu.VMEM((D_MODEL, 2 * D_EXPERT), F32),
            pltpu.VMEM((D_EXPERT, D_MODEL), F32),
            pltpu.VMEM((D_MODEL, 2 * D_EXPERT), BF16),
            pltpu.VMEM((D_EXPERT, D_MODEL), BF16),
            pltpu.SemaphoreType.DMA((2,)),
        ],
    )
    return pl.pallas_call(
        _ffn_kernel,
        grid_spec=grid_spec,
        out_shape=jax.ShapeDtypeStruct((cap, half), U32),
        compiler_params=pltpu.CompilerParams(
            dimension_semantics=("arbitrary",), vmem_limit_bytes=VMEM_LIMIT),
        name="ffn",
    )(blk_e, nused, next_e, xs, w1, b1p, w2, b2)


def _skewed(tiles, phases, finish=True):
    for step in range(phases + len(tiles) - 1):
        for i, tile in enumerate(tiles):
            if finish and step - i == phases - 1:
                assert next(tile, "done") == "done"
            elif 0 <= step - i < phases:
                next(tile)


def _combine_kernel(run_dst_ref, run_len_ref, tile_rows_ref, y_ref, pos_ref, gate_ref, x1_ref, mod_ref,
                    nrm_ref, out_ref, ybuf, w_hi_ref, w_lo_ref, y_a_ref, y_b_ref, sem):
    g = pl.program_id(0)
    cur = g % 2
    tps = COMBINE_TILES

    def fetch_expert(t, s, i, e, dst_off):
        k = _plan_index(t, e)
        _, dst_off = _run_copies(run_len_ref[k], y_ref, run_dst_ref[k], ybuf.at[s, i], dst_off,
                                 sem.at[s, i])
        return dst_off

    def fetch_tile_stepwise(t, s, i):
        dst_off = jnp.int32(0)
        for e in range(N_EXPERTS):
            dst_off = fetch_expert(t, s, i, e, dst_off)
            yield

    @pl.when(g == 0)
    def _():
        ybuf[...] = jnp.zeros(ybuf.shape, U32)
        for i in range(tps):
            lax.fori_loop(0, N_EXPERTS, functools.partial(fetch_expert, i, cur, i), 0)

    for i in range(tps):
        _wait_rows(tile_rows_ref[g * tps + i + PLAN_PAD], y_ref, ybuf.at[cur, i], sem.at[cur, i])
    def combine_tile(i):
        rows = slice(i * TOKEN_TILE, (i + 1) * TOKEN_TILE)
        fetch = fetch_tile_stepwise((g + 1) * tps + i, 1 - cur, i)

        def issue(n_experts):
            for _ in range(n_experts):
                next(fetch, None)

        pos_t = jnp.transpose(pos_ref[:, rows].astype(F32))
        gate_t = jnp.transpose(gate_ref[:, rows])
        for c0 in range(0, SORT_ROWS, LANES):
            c_iota = (lax.broadcasted_iota(I32, (TOKEN_TILE, LANES), 1) + c0).astype(F32)
            wsel = jnp.zeros((TOKEN_TILE, LANES), F32)
            for k in range(TOP_K):
                wsel = jnp.where(c_iota == pos_t[:, k:k + 1], gate_t[:, k:k + 1], wsel)
            hi, lo = _split_bf16(wsel)
            w_hi_ref[i, :, c0:c0 + LANES] = hi
            w_lo_ref[i, :, c0:c0 + LANES] = lo
            issue(2)

        for r0 in range(0, SORT_ROWS, UNPACK_ROWS):
            a, b = _unpack_pair(ybuf[cur, i, r0:r0 + UNPACK_ROWS, :])
            y_a_ref[i, r0:r0 + UNPACK_ROWS, :] = a
            y_b_ref[i, r0:r0 + UNPACK_ROWS, :] = b
            issue(1)
        issue(N_EXPERTS)
        yield

        w_hi, w_lo = w_hi_ref[i], w_lo_ref[i]
        y_a, y_b = y_a_ref[i], y_b_ref[i]
        f = jnp.concatenate([_dot(w_hi, y_a) + _dot(w_lo, y_a), _dot(w_hi, y_b) + _dot(w_lo, y_b)],
                            axis=-1)
        yield

        gate_f = mod_ref[5:6, :]
        out_ref[rows, :] = x1_ref[rows, :] + _rms(f, nrm_ref[...] * gate_f)

    _skewed([combine_tile(i) for i in range(tps)], phases=3)


def _combine_call(run_dst, run_len, tile_rows, y, pos, gate, x1, mod3, nrm_post):
    batch, seq, _ = x1.shape
    tps = COMBINE_TILES
    span = tps * TOKEN_TILE
    per_row = seq // span
    half = D_MODEL // 2
    grid_spec = pltpu.PrefetchScalarGridSpec(
        num_scalar_prefetch=3,
        grid=(batch * per_row,),
        in_specs=[
            pl.BlockSpec(memory_space=pl.ANY),
            pl.BlockSpec((None, SUBLANES, span), lambda g, *_: (g // per_row, 0, g % per_row)),
            pl.BlockSpec((None, SUBLANES, span), lambda g, *_: (g // per_row, 0, g % per_row)),
            pl.BlockSpec((None, span, D_MODEL), lambda g, *_: (g // per_row, g % per_row, 0)),
            pl.BlockSpec((None, 6, D_MODEL), lambda g, *_: (g // per_row, 0, 0)),
            pl.BlockSpec((1, D_MODEL), lambda g, *_: (0, 0)),
        ],
        out_specs=pl.BlockSpec((None, span, D_MODEL), lambda g, *_: (g // per_row, g % per_row, 0)),
        scratch_shapes=[
            pltpu.VMEM((2, tps, SORT_ROWS, half), U32),
            pltpu.VMEM((tps, TOKEN_TILE, SORT_ROWS), BF16),
            pltpu.VMEM((tps, TOKEN_TILE, SORT_ROWS), BF16),
            pltpu.VMEM((tps, SORT_ROWS, half), BF16),
            pltpu.VMEM((tps, SORT_ROWS, half), BF16),
            pltpu.SemaphoreType.DMA((2, tps)),
        ],
    )
    return pl.pallas_call(
        _combine_kernel,
        grid_spec=grid_spec,
        out_shape=jax.ShapeDtypeStruct((batch, seq, D_MODEL), F32),
        compiler_params=pltpu.CompilerParams(
            dimension_semantics=("arbitrary",), vmem_limit_bytes=VMEM_LIMIT),
        name="combine",
    )(run_dst, run_len, tile_rows, y, pos, gate, x1, mod3, nrm_post)


def _block_diag(w):
    heads, d, _ = w.shape
    n = heads * d
    tiled = jnp.tile(w.reshape(n, d), (1, heads))
    same_head = (jnp.arange(n)[:, None] // d) == (jnp.arange(n)[None, :] // d)
    return jnp.where(same_head, tiled, 0.0)


def _route_plan(cnt):
    nt = cnt.shape[0]
    run_len = (cnt + (SUBLANES - 1)) // SUBLANES * SUBLANES
    region = jnp.sum(run_len, axis=0)
    region_blk = (region + FFN_BLOCK - 1) // FFN_BLOCK * FFN_BLOCK
    earlier_e = jnp.arange(N_EXPERTS)[None, :] < jnp.arange(N_EXPERTS)[:, None]
    region_start = jnp.sum(jnp.where(earlier_e, region_blk[None, :], 0), axis=1)
    region_end = region_start + region_blk
    earlier_t = jnp.arange(nt)[None, :] < jnp.arange(nt)[:, None]
    run_before = jnp.sum(jnp.where(earlier_t[:, :, None], run_len[None, :, :], 0), axis=1)
    run_dst = region_start[None, :] + run_before
    tail_dst = region_start + region
    tail_len = region_blk - region
    total_rows = jnp.sum(region_blk)
    nused = total_rows // FFN_BLOCK
    max_rows = nt * TOKEN_TILE * TOP_K + nt * N_EXPERTS * (SUBLANES - 1) + N_EXPERTS * (FFN_BLOCK - SUBLANES)
    nb = -(-max_rows // FFN_BLOCK)
    blk_start = jnp.arange(nb, dtype=I32) * FFN_BLOCK
    blk_row = jnp.minimum(blk_start, total_rows - 1)
    blk_e = jnp.sum((blk_row[:, None] >= region_end[None, :]).astype(I32), axis=1)
    blk_e = jnp.minimum(blk_e, N_EXPERTS - 1).astype(I32)
    later_used = jnp.logical_and(jnp.arange(N_EXPERTS)[None, :] > jnp.arange(N_EXPERTS)[:, None],
                                 region_blk[None, :] > 0)
    next_e = jnp.min(jnp.where(later_used, jnp.arange(N_EXPERTS)[None, :], N_EXPERTS), axis=1)
    next_e = jnp.where(next_e == N_EXPERTS, -1, next_e)
    tile_rows = jnp.sum(run_len, axis=1)
    edge = ((PLAN_PAD, PLAN_PAD), (0, 0))
    run_dst, run_len = jnp.pad(run_dst, edge), jnp.pad(run_len, edge)
    tile_rows = jnp.pad(tile_rows, PLAN_PAD)
    return dict(run_dst=run_dst.reshape(-1).astype(I32), run_len=run_len.reshape(-1).astype(I32),
                tile_rows=tile_rows.astype(I32), tail_dst=tail_dst.astype(I32),
                tail_len=tail_len.astype(I32), blk_e=blk_e, next_e=next_e.astype(I32),
                nused=nused.reshape(1).astype(I32)), nb * FFN_BLOCK


def kernel(x, c, w_ada, b_ada, norm_mix_pre, norm_mix_post, w_in, conv_w, conv_b, gate_a_w, gate_a_b, gate_x_w, gate_x_b, lru_lambda, pool_w, pool_b, pool_scale, w_out, norm_ffn_pre, norm_ffn_post, router_w, router_b, expert_w1, expert_b1, expert_w2, expert_b2):
    depth = w_ada.shape[0]
    batch = x.shape[0]
    for l in range(depth):
        mod3 = _ada_call(c, w_ada[l], b_ada[l]).reshape(batch, 6, D_MODEL)

        nrm = jnp.stack([norm_mix_pre[l], norm_mix_post[l], norm_ffn_pre[l]])
        lruv = jnp.stack([conv_b[l], gate_a_b[l], gate_x_b[l], lru_lambda[l]])
        wg = jnp.concatenate([_block_diag(gate_a_w[l]), _block_diag(gate_x_w[l])], axis=1).astype(BF16)
        poolv = jnp.stack([pool_b[l], pool_scale[l]])
        wp = _block_diag(pool_w[l]).astype(BF16)
        rw_pad = jnp.pad(router_w[l], ((0, 0), (0, LANES - N_EXPERTS)))
        rw_hi = rw_pad.astype(BF16)
        rw_lo = (rw_pad - rw_hi.astype(F32)).astype(BF16)
        rwh = jnp.concatenate([rw_hi, rw_lo], axis=1)
        rb = router_b[l].reshape(N_EXPERTS, 1)

        x1, h2, gate, pos, cnt = _mixer_call(
            x, mod3, nrm, w_in[l].astype(BF16), conv_w[l], lruv, wg, poolv, wp,
            w_out[l].astype(BF16), rwh, rb)

        plan, cap = _route_plan(cnt[:, :, :, 0].reshape(-1, N_EXPERTS))
        xs = _dispatch_call(plan["run_dst"], plan["run_len"], plan["tile_rows"], plan["tail_dst"],
                            plan["tail_len"], plan["nused"], h2.reshape(-1, D_MODEL), pos, cap)

        b1p = jnp.concatenate([expert_b1[l][:, 0::2], expert_b1[l][:, 1::2]], axis=-1)
        y = _ffn_call(plan["blk_e"], plan["nused"], plan["next_e"], xs, expert_w1[l],
                      b1p.reshape(N_EXPERTS, 1, 2 * D_EXPERT),
                      expert_w2[l], expert_b2[l].reshape(N_EXPERTS, 1, D_MODEL))

        x = _combine_call(plan["run_dst"], plan["run_len"], plan["tile_rows"], y, pos, gate, x1, mod3,
                          norm_ffn_post[l].reshape(1, D_MODEL))
    return x
```

```python
import functools

import jax
import jax.numpy as jnp
from jax import lax
from jax.experimental import pallas as pl
from jax.experimental.pallas import tpu as pltpu

F32 = jnp.float32
BF16 = jnp.bfloat16
I32 = jnp.int32
U32 = jnp.uint32

D_MODEL = 1024
LRU_WIDTH = 512
LRU_HEADS = 8
POOL_WIDTH = 512
POOL_WINDOWS = (2, 4, 8, 16)
POOL_GROUP_DIM = 128
CONV_WIDTH = 4
IN_WIDTH = 2 * LRU_WIDTH + POOL_WIDTH
LRU_C = 8.0
N_EXPERTS = 32
TOP_K = 4
D_EXPERT = 1024
SWIGLU_LIMIT = 7.0
SWIGLU_ALPHA = 1.702
NORM_EPS = 1e-6

SUBLANES = 8
LANES = 128
TOKEN_TILE = 256
FFN_BLOCK = 256
MIXER_BATCH_GROUP = 4
MIXER_PHASES = 6
CONV_CARRY = SUBLANES
POOL_CARRY = 16
SORT_ROWS = 1280
RUN_SIZES = (256, 128, 64, 32, 16, 8)
UNPACK_ROWS = 64
VMEM_LIMIT = 52 * 1024 * 1024


def _rms(v, g):
    return v * lax.rsqrt(jnp.mean(v * v, axis=-1, keepdims=True) + NORM_EPS) * g


def _dot(a, b):
    return jnp.dot(a, b, preferred_element_type=F32)


def _dot_nt(a, b):
    return lax.dot_general(a, b, (((1,), (1,)), ((), ())), preferred_element_type=F32)


def _split_bf16(v):
    hi = v.astype(BF16)
    lo = (v - hi.astype(F32)).astype(BF16)
    return hi, lo


def _pack_pair(a, b):
    ua = lax.bitcast_convert_type(a, U32)
    ub = lax.bitcast_convert_type(b, U32)
    return (ua >> 16) | (ub & jnp.uint32(0xFFFF0000))


def _unpack_pair(w):
    a = lax.bitcast_convert_type(w << 16, F32).astype(BF16)
    b = lax.bitcast_convert_type(w & jnp.uint32(0xFFFF0000), F32).astype(BF16)
    return a, b


def _ada_kernel(c_ref, w_ref, b_ref, o_ref):
    c = c_ref[...]
    ca = c * jax.nn.sigmoid(c)
    o_ref[...] = jnp.dot(ca, w_ref[...], preferred_element_type=F32,
                         precision=lax.Precision.HIGHEST) + b_ref[...]


def _ada_call(c, w_ada, b_ada):
    batch = c.shape[0]
    n_out = w_ada.shape[1]
    bn = D_MODEL
    return pl.pallas_call(
        _ada_kernel,
        grid=(n_out // bn,),
        in_specs=[
            pl.BlockSpec((batch, D_MODEL), lambda j: (0, 0)),
            pl.BlockSpec((D_MODEL, bn), lambda j: (0, j)),
            pl.BlockSpec((1, bn), lambda j: (0, j)),
        ],
        out_specs=pl.BlockSpec((batch, bn), lambda j: (0, j)),
        out_shape=jax.ShapeDtypeStruct((batch, n_out), F32),
        name="ada",
    )(c, w_ada, b_ada.reshape(1, n_out))


def _lru_scan(a, b, h0, out_ref):
    ts, c = a.shape
    groups = ts // SUBLANES
    a3 = a.reshape(groups, SUBLANES, c)
    b3 = b.reshape(groups, SUBLANES, c)
    row = lax.broadcasted_iota(I32, (groups, SUBLANES, c), 1)
    d = 1
    while d < SUBLANES:
        a_sh = pltpu.roll(a3, d, axis=1)
        b_sh = pltpu.roll(b3, d, axis=1)
        m = row >= d
        b3 = jnp.where(m, a3 * b_sh + b3, b3)
        a3 = jnp.where(m, a3 * a_sh, a3)
        d *= 2
    h = h0
    for g in range(groups):
        hg = a3[g] * h + b3[g]
        out_ref[pl.ds(g * SUBLANES, SUBLANES), :] = hg
        h = hg[SUBLANES - 1:SUBLANES, :]
    return h


def _mixer_kernel(x_ref, mod_ref, nrm_ref, w_in_ref, convw_ref, lruv_ref, wg_ref,
                  poolv_ref, wp_ref, w_out_ref, rwh_ref, rb_ref, rankw_ref,
                  x1_ref, h2_ref, gate_ref, pos_ref, cnt_ref,
                  xa_ext, xb_ext, hcar, hbuf):
    s = pl.program_id(1)
    shared = (nrm_ref, w_in_ref, convw_ref, lruv_ref, wg_ref, poolv_ref, wp_ref, w_out_ref,
              rwh_ref, rb_ref, rankw_ref)
    per_row = (x_ref, mod_ref, x1_ref, h2_ref, gate_ref, pos_ref, cnt_ref, xa_ext, xb_ext, hcar, hbuf)

    @pl.when(s == 0)
    def _():
        xa_ext[:, 0:CONV_CARRY, :] = jnp.zeros((xa_ext.shape[0], CONV_CARRY, LRU_WIDTH), F32)
        xb_ext[:, 0:POOL_CARRY, :] = jnp.zeros((xb_ext.shape[0], POOL_CARRY, POOL_WIDTH), F32)
        hcar[...] = jnp.zeros(hcar.shape, F32)

    tiles = [_mixer_tile(s, shared, *(ref.at[p] for ref in per_row)) for p in range(x_ref.shape[0])]
    _skewed(tiles, MIXER_PHASES - 1, finish=False)
    for tile in tiles:
        assert next(tile, "done") == "done"


def _mixer_tile(s, shared, x_ref, mod_ref, x1_ref, h2_ref, gate_ref, pos_ref, cnt_ref,
                xa_ext, xb_ext, hcar, hbuf):
    (nrm_ref, w_in_ref, convw_ref, lruv_ref, wg_ref, poolv_ref, wp_ref, w_out_ref,
     rwh_ref, rb_ref, rankw_ref) = shared
    ts = TOKEN_TILE
    x = x_ref[...]
    mod = mod_ref[...]
    shift_m, scale_m, gate_m = mod[0:1], mod[1:2], mod[2:3]
    shift_f, scale_f = mod[3:4], mod[4:5]
    nrm = nrm_ref[...]

    h = _rms(x, nrm[0:1] * (1.0 + scale_m)) + shift_m
    u = _dot(h.astype(BF16), w_in_ref[...])
    yield
    xa_raw = u[:, :LRU_WIDTH]
    ga = u[:, LRU_WIDTH:2 * LRU_WIDTH]
    xb = u[:, 2 * LRU_WIDTH:]

    lruv = lruv_ref[...]
    conv_b, gate_a_b, gate_x_b, lam = lruv[0:1], lruv[1:2], lruv[2:3], lruv[3:4]
    convw = convw_ref[...]
    xa_ext[CONV_CARRY:, :] = xa_raw
    ext = xa_ext[...]
    acc = ext * convw[0:1]
    for k in range(1, CONV_WIDTH):
        acc = acc + pltpu.roll(ext, k, axis=0) * convw[k:k + 1]
    xa = acc[CONV_CARRY:, :] + conv_b
    xa_ext[0:CONV_CARRY, :] = xa_raw[ts - CONV_CARRY:, :]

    g = _dot(xa.astype(BF16), wg_ref[...])
    yield
    r = jax.nn.sigmoid(g[:, :LRU_WIDTH] + gate_a_b)
    i = jax.nn.sigmoid(g[:, LRU_WIDTH:] + gate_x_b)
    softplus_neg_lam = jnp.maximum(-lam, 0.0) + jnp.log(1.0 + jnp.exp(-jnp.abs(lam)))
    log_a = -LRU_C * r * softplus_neg_lam
    a = jnp.exp(log_a)
    one_minus_a2 = 1.0 - a * a
    mult = jnp.where(one_minus_a2 > 0.0, one_minus_a2 * lax.rsqrt(one_minus_a2), 0.0)
    bterm = mult * (i * xa)
    h_last = _lru_scan(a, bterm, hcar[0:1, :], hbuf)
    hcar[0:1, :] = h_last
    ya = hbuf[...] * jax.nn.gelu(ga)
    yield

    xb_ext[POOL_CARRY:, :] = xb
    t_glob = s * ts + lax.broadcasted_iota(I32, (ts, POOL_GROUP_DIM), 0)
    pooled = []
    for gi, w in enumerate(POOL_WINDOWS):
        lo = gi * POOL_GROUP_DIM
        e = xb_ext[:, lo:lo + POOL_GROUP_DIM]
        acc_w = e
        span = 1
        while span < w:
            acc_w = acc_w + pltpu.roll(acc_w, span, axis=0)
            span *= 2
        count = jnp.minimum(t_glob + 1, w).astype(F32)
        pooled.append(acc_w[POOL_CARRY:, :] / count - xb[:, lo:lo + POOL_GROUP_DIM])
    xb_ext[0:POOL_CARRY, :] = xb[ts - POOL_CARRY:, :]
    p = jnp.concatenate(pooled, axis=-1)
    poolv = poolv_ref[...]
    yb = (_dot(p.astype(BF16), wp_ref[...]) + poolv[0:1]) * poolv[1:2]
    yield

    mix =_dot(jnp.concatenate([ya, yb], axis=-1).astype(BF16), w_out_ref[...])
    x1 = x + _rms(mix, nrm[1:2] * gate_m)
    x1_ref[...] = x1

    h2 = _rms(x1, nrm[2:3] * (1.0 + scale_f)) + shift_f
    h2_hi, h2_lo = _split_bf16(h2)
    h2_ref[...] = h2_hi
    yield

    hi_terms = _dot(h2_hi, rwh_ref[...])
    lo_term = _dot(h2_lo, rwh_ref[:, 0:LANES])
    logits_t = hi_terms[:, :LANES] + hi_terms[:, LANES:] + lo_term
    logits = jnp.transpose(logits_t)[0:N_EXPERTS, :] + rb_ref[...]

    e_iota = lax.broadcasted_iota(I32, (N_EXPERTS, ts), 0)
    beaten = jnp.zeros((N_EXPERTS, ts), F32)
    for e2 in range(N_EXPERTS):
        other = logits[e2:e2 + 1, :]
        ahead_strict = jnp.where(other > logits, 1.0, 0.0)
        ahead_or_tie = jnp.where(other >= logits, 1.0, 0.0)
        beaten = beaten + jnp.where(e_iota > e2, ahead_or_tie, ahead_strict)
    sels = [beaten == float(k) for k in range(TOP_K)]
    vals = [jnp.sum(jnp.where(sel, logits, 0.0), axis=0, keepdims=True) for sel in sels]
    exps = [jnp.exp(v - vals[0]) for v in vals]
    denom = exps[0] + exps[1] + exps[2] + exps[3]
    gates = [ex / denom for ex in exps]

    member = jnp.where(beaten < float(TOP_K), 1.0, 0.0)
    rank_cnt = _dot(member.astype(BF16), rankw_ref[...])
    rank, n_e = rank_cnt[:, :ts], rank_cnt[:, ts:].astype(I32)
    run_len = ((n_e + (SUBLANES - 1)) >> 3) << 3
    ltri = (lax.broadcasted_iota(I32, (N_EXPERTS, N_EXPERTS), 1)
            < lax.broadcasted_iota(I32, (N_EXPERTS, N_EXPERTS), 0))
    run_off = _dot(jnp.where(ltri, 1.0, 0.0).astype(BF16), run_len.astype(F32).astype(BF16))
    slot = rank + run_off
    pos_rows = [jnp.sum(jnp.where(sel, slot, 0.0), axis=0, keepdims=True) for sel in sels]

    zeros_f = jnp.zeros((SUBLANES - TOP_K, ts), F32)
    gate_ref[...] = jnp.concatenate(gates + [zeros_f], axis=0)
    pos_ref[...] = jnp.concatenate(pos_rows + [zeros_f], axis=0).astype(I32)
    cnt_ref[...] = n_e[:, :LANES]


def _mixer_call(x, mod3, nrm, w_in_b, convw, lruv, wg, poolv, wp, w_out_b, rwh, rb):
    batch, seq, _ = x.shape
    ns = seq // TOKEN_TILE
    t_row = jnp.arange(TOKEN_TILE)[:, None]
    t_col = jnp.arange(2 * TOKEN_TILE)[None, :]
    rankw = jnp.where(t_col >= TOKEN_TILE, 1.0, jnp.where(t_row < t_col, 1.0, 0.0)).astype(BF16)
    grp = MIXER_BATCH_GROUP if batch % MIXER_BATCH_GROUP == 0 else 1
    const = lambda shape: pl.BlockSpec(shape, lambda b, s: (0,) * len(shape))
    return pl.pallas_call(
        _mixer_kernel,
        grid=(batch // grp, ns),
        in_specs=[
            pl.BlockSpec((grp, TOKEN_TILE, D_MODEL), lambda b, s: (b, s, 0)),
            pl.BlockSpec((grp, 6, D_MODEL), lambda b, s: (b, 0, 0)),
            const(nrm.shape), const(w_in_b.shape), const(convw.shape), const(lruv.shape),
            const(wg.shape), const(poolv.shape), const(wp.shape), const(w_out_b.shape),
            const(rwh.shape), const(rb.shape), const(rankw.shape),
        ],
        out_specs=[
            pl.BlockSpec((grp, TOKEN_TILE, D_MODEL), lambda b, s: (b, s, 0)),
            pl.BlockSpec((grp, TOKEN_TILE, D_MODEL), lambda b, s: (b, s, 0)),
            pl.BlockSpec((grp, SUBLANES, TOKEN_TILE), lambda b, s: (b, 0, s)),
            pl.BlockSpec((grp, SUBLANES, TOKEN_TILE), lambda b, s: (b, 0, s)),
            pl.BlockSpec((grp, None, N_EXPERTS, LANES), lambda b, s: (b, s, 0, 0)),
        ],
        out_shape=[
            jax.ShapeDtypeStruct((batch, seq, D_MODEL), F32),
            jax.ShapeDtypeStruct((batch, seq, D_MODEL), BF16),
            jax.ShapeDtypeStruct((batch, SUBLANES, seq), F32),
            jax.ShapeDtypeStruct((batch, SUBLANES, seq), I32),
            jax.ShapeDtypeStruct((batch, ns, N_EXPERTS, LANES), I32),
        ],
        scratch_shapes=[
            pltpu.VMEM((grp, TOKEN_TILE + CONV_CARRY, LRU_WIDTH), F32),
            pltpu.VMEM((grp, TOKEN_TILE + POOL_CARRY, POOL_WIDTH), F32),
            pltpu.VMEM((grp, SUBLANES, LRU_WIDTH), F32),
            pltpu.VMEM((grp, TOKEN_TILE, LRU_WIDTH), F32),
        ],
        compiler_params=pltpu.CompilerParams(
            dimension_semantics=("arbitrary", "arbitrary"), vmem_limit_bytes=VMEM_LIMIT),
        name="mixer",
    )(x, mod3, nrm, w_in_b, convw, lruv, wg, poolv, wp, w_out_b, rwh, rb, rankw)


def _run_copies(length, src_ref, src_off, dst_ref, dst_off, sem, wait=False):
    for size in RUN_SIZES:
        hit = (length & size) != 0

        @pl.when(hit)
        def _(src_off=src_off, dst_off=dst_off, size=size):
            copy = pltpu.make_async_copy(
                src_ref.at[pl.ds(pl.multiple_of(src_off, SUBLANES), size)],
                dst_ref.at[pl.ds(pl.multiple_of(dst_off, SUBLANES), size)],
                sem)
            if wait:
                copy.wait()
            else:
                copy.start()

        step = jnp.where(hit, size, 0)
        src_off = src_off + step
        dst_off = dst_off + step
    return src_off, dst_off


DISPATCH_TILES = 4
COMBINE_TILES = 2
PLAN_PAD = 2 * max(DISPATCH_TILES, COMBINE_TILES)


def _plan_index(tile, expert):
    return (tile + PLAN_PAD) * N_EXPERTS + expert


def _wait_rows(rows, hbm_ref, vmem_ref, sem):
    @pl.when(rows > 0)
    def _():
        n = pl.multiple_of(rows, SUBLANES)
        pltpu.make_async_copy(hbm_ref.at[pl.ds(0, n)], vmem_ref.at[pl.ds(0, n)], sem).wait()


def _dispatch_kernel(run_dst_ref, run_len_ref, tile_rows_ref, tail_dst_ref, tail_len_ref, nused_ref,
                     h2_ref, pos_ref, xs_ref, sorted_ref, zero_ref, perm_ref, sem, zsem, *, steps):
    j = pl.program_id(0)
    last = steps - 1
    half = D_MODEL // 2

    @pl.when(j == 0)
    def _():
        zero_ref[...] = jnp.zeros(zero_ref.shape, U32)

        for wait in (False, True):
            def tail_body(e, carry, wait=wait):
                _run_copies(tail_len_ref[e], zero_ref, jnp.int32(0), xs_ref, tail_dst_ref[e], zsem,
                            wait=wait)
                return carry

            lax.fori_loop(0, N_EXPERTS, tail_body, 0)

            def spare_body(b, carry, wait=wait):
                copy = pltpu.make_async_copy(
                    zero_ref, xs_ref.at[pl.ds(pl.multiple_of(b * FFN_BLOCK, FFN_BLOCK), FFN_BLOCK)], zsem)
                if wait:
                    copy.wait()
                else:
                    copy.start()
                return carry

            lax.fori_loop(nused_ref[0], xs_ref.shape[0] // FFN_BLOCK, spare_body, 0)

    cur = j % 2
    tps = DISPATCH_TILES

    def send_expert(t, s, i, e, src_off):
        k = _plan_index(t, e)
        src_off, _ = _run_copies(run_len_ref[k], sorted_ref.at[s, i], src_off, xs_ref,
                                 run_dst_ref[k], sem.at[s, i])
        return src_off

    def send_tile_stepwise(t, s, i):
        src_off = jnp.int32(0)
        for e in range(N_EXPERTS):
            src_off = send_expert(t, s, i, e, src_off)
            yield

    def wait_tile(t, s, i):
        _wait_rows(tile_rows_ref[t + PLAN_PAD], xs_ref, sorted_ref.at[s, i], sem.at[s, i])

    for i in range(tps):
        wait_tile((j - 2) * tps + i, cur, i)
    def sort_tile(i):
        cols = slice(i * TOKEN_TILE, (i + 1) * TOKEN_TILE)
        send = send_tile_stepwise((j - 1) * tps + i, 1 - cur, i)
        pos = pos_ref[:, cols]
        for r0 in range(0, SORT_ROWS, LANES):
            r_iota = lax.broadcasted_iota(I32, (LANES, TOKEN_TILE), 0) + r0
            perm = jnp.zeros((LANES, TOKEN_TILE), F32)
            for k in range(TOP_K):
                perm = jnp.where(r_iota == pos[k:k + 1, :], 1.0, perm)
            perm_ref[i, r0:r0 + LANES, :] = perm.astype(BF16)
            for _ in range(4):
                next(send, None)
        yield
        rows = _dot(perm_ref[i], h2_ref[cols, :])
        yield
        sorted_ref[cur, i] = _pack_pair(rows[:, :half], rows[:, half:])

    _skewed([sort_tile(i) for i in range(tps)], phases=3)

    @pl.when(j == last)
    def _():
        for i in range(tps):
            lax.fori_loop(0, N_EXPERTS, functools.partial(send_expert, j * tps + i, cur, i), 0)
        for i in range(tps):
            wait_tile(j * tps + i, cur, i)
            if last >= 1:
                wait_tile((j - 1) * tps + i, 1 - cur, i)


def _dispatch_call(run_dst, run_len, tile_rows, tail_dst, tail_len, nused, h2, pos, cap):
    tokens = h2.shape[0]
    tps = DISPATCH_TILES
    span = tps * TOKEN_TILE
    steps = tokens // span
    per_row = pos.shape[2] // span
    half = D_MODEL // 2
    grid_spec = pltpu.PrefetchScalarGridSpec(
        num_scalar_prefetch=6,
        grid=(steps,),
        in_specs=[
            pl.BlockSpec((span, D_MODEL), lambda j, *_: (j, 0)),
            pl.BlockSpec((None, SUBLANES, span), lambda j, *_: (j // per_row, 0, j % per_row)),
        ],
        out_specs=pl.BlockSpec(memory_space=pl.ANY),
        scratch_shapes=[
            pltpu.VMEM((2, tps, SORT_ROWS, half), U32),
            pltpu.VMEM((FFN_BLOCK, half), U32),
            pltpu.VMEM((tps, SORT_ROWS, TOKEN_TILE), BF16),
            pltpu.SemaphoreType.DMA((2, tps)),
            pltpu.SemaphoreType.DMA(()),
        ],
    )
    return pl.pallas_call(
        functools.partial(_dispatch_kernel, steps=steps),
        grid_spec=grid_spec,
        out_shape=jax.ShapeDtypeStruct((cap, half), U32),
        compiler_params=pltpu.CompilerParams(
            dimension_semantics=("arbitrary",), vmem_limit_bytes=VMEM_LIMIT),
        name="dispatch",
    )(run_dst, run_len, tile_rows, tail_dst, tail_len, nused, h2, pos)


W1_CHUNK = 256


def _ffn_kernel(blk_e_ref, nused_ref, next_e_ref, xs_ref, w1_hbm, b1_ref, w2_hbm, b2_ref, y_ref,
                w1_ref, w2_ref, w1s, w2s, wsem):
    i = pl.program_id(0)
    half = D_MODEL // 2
    prev = jnp.maximum(i - 1, 0)
    expert = blk_e_ref[i]
    new_expert = jnp.logical_or(i == 0, expert != blk_e_ref[prev])

    def fetch(e):
        return (pltpu.make_async_copy(w1_hbm.at[e], w1_ref, wsem.at[0]),
                pltpu.make_async_copy(w2_hbm.at[e], w2_ref, wsem.at[1]))

    @pl.when(i == 0)
    def _():
        for copy in fetch(expert):
            copy.start()

    @pl.when(jnp.logical_and(i < nused_ref[0], new_expert))
    def _():
        for copy in fetch(expert):
            copy.wait()
        hc = W1_CHUNK // 2
        r = lax.broadcasted_iota(I32, (W1_CHUNK, W1_CHUNK), 0)
        q = lax.broadcasted_iota(I32, (W1_CHUNK, W1_CHUNK), 1)
        src_col = jnp.where(q < hc, 2 * q, 2 * (q - hc) + 1)
        sel = jnp.where(r == src_col, 1.0, 0.0).astype(BF16)
        for c in range(2 * D_EXPERT // W1_CHUNK):
            chunk = w1_ref[:, c * W1_CHUNK:(c + 1) * W1_CHUNK].astype(BF16)
            picked = _dot(chunk, sel).astype(BF16)
            w1s[:, c * hc:(c + 1) * hc] = picked[:, :hc]
            w1s[:, D_EXPERT + c * hc:D_EXPERT + (c + 1) * hc] = picked[:, hc:]
        w2s[...] = w2_ref[...].astype(BF16)

        nxt = next_e_ref[expert]

        @pl.when(nxt >= 0)
        def _():
            for copy in fetch(nxt):
                copy.start()

    @pl.when(i < nused_ref[0])
    def _():
        x_a, x_b = _unpack_pair(xs_ref[...])
        u = _dot(jnp.concatenate([x_a, x_b], axis=-1), w1s[...]) + b1_ref[...]
        glu = jnp.minimum(u[:, :D_EXPERT], SWIGLU_LIMIT)
        lin = jnp.clip(u[:, D_EXPERT:], -SWIGLU_LIMIT, SWIGLU_LIMIT)
        act = glu * jax.nn.sigmoid(SWIGLU_ALPHA * glu) * (lin + 1.0)
        y = _dot(act.astype(BF16), w2s[...]) + b2_ref[...]
        yb = y.astype(BF16).astype(F32)
        y_ref[...] = _pack_pair(yb[:, :half], yb[:, half:])

    @pl.when(i >= nused_ref[0])
    def _():
        y_ref[...] = jnp.zeros(y_ref.shape, U32)


def _ffn_call(blk_e, nused, next_e, xs, w1, b1p, w2, b2):
    cap, half = xs.shape
    nb = cap // FFN_BLOCK
    row_blk = lambda i, be, nu, ne: (jnp.minimum(i, nu[0] - 1), 0)
    grid_spec = pltpu.PrefetchScalarGridSpec(
        num_scalar_prefetch=3,
        grid=(nb,),
        in_specs=[
            pl.BlockSpec((FFN_BLOCK, half), row_blk),
            pl.BlockSpec(memory_space=pl.ANY),
            pl.BlockSpec((None, 1, 2 * D_EXPERT), lambda i, be, nu, ne: (be[i], 0, 0)),
            pl.BlockSpec(memory_space=pl.ANY),
            pl.BlockSpec((None, 1, D_MODEL), lambda i, be, nu, ne: (be[i], 0, 0)),
        ],
        out_specs=pl.BlockSpec((FFN_BLOCK, half), lambda i, be, nu, ne: (i, 0)),
        scratch_shapes=[
            pltpu.VMEM((D_MODEL, 2 * D_EXPERT), F32),
            pltpu.VMEM((D_EXPERT, D_MODEL), F32),
            pltpu.VMEM((D_MODEL, 2 * D_EXPERT), BF16),
            pltpu.VMEM((D_EXPERT, D_MODEL), BF16),
            pltpu.SemaphoreType.DMA((2,)),
        ],
    )
    return pl.pallas_call(
        _ffn_kernel,
        grid_spec=grid_spec,
        out_shape=jax.ShapeDtypeStruct((cap, half), U32),
        compiler_params=pltpu.CompilerParams(
            dimension_semantics=("arbitrary",), vmem_limit_bytes=VMEM_LIMIT),
        name="ffn",
    )(blk_e, nused, next_e, xs, w1, b1p, w2, b2)


def _skewed(tiles, phases, finish=True):
    for step in range(phases + len(tiles) - 1):
        for i, tile in enumerate(tiles):
            if finish and step - i == phases - 1:
                assert next(tile, "done") == "done"
            elif 0 <= step - i < phases:
                next(tile)


def _combine_kernel(run_dst_ref, run_len_ref, tile_rows_ref, y_ref, pos_ref, gate_ref, x1_ref, mod_ref,
                    nrm_ref, out_ref, ybuf, w_ref, y_a_ref, y_b_ref, sem):
    g = pl.program_id(0)
    cur = g % 2
    tps = COMBINE_TILES

    def fetch_expert(t, s, i, e, dst_off):
        k = _plan_index(t, e)
        _, dst_off = _run_copies(run_len_ref[k], y_ref, run_dst_ref[k], ybuf.at[s, i], dst_off,
                                 sem.at[s, i])
        return dst_off

    def fetch_tile_stepwise(t, s, i):
        dst_off = jnp.int32(0)
        for e in range(N_EXPERTS):
            dst_off = fetch_expert(t, s, i, e, dst_off)
            yield

    @pl.when(g == 0)
    def _():
        ybuf[...] = jnp.zeros(ybuf.shape, U32)
        for i in range(tps):
            lax.fori_loop(0, N_EXPERTS, functools.partial(fetch_expert, i, cur, i), 0)

    for i in range(tps):
        _wait_rows(tile_rows_ref[g * tps + i + PLAN_PAD], y_ref, ybuf.at[cur, i], sem.at[cur, i])
    def combine_tile(i):
        rows = slice(i * TOKEN_TILE, (i + 1) * TOKEN_TILE)
        fetch = fetch_tile_stepwise((g + 1) * tps + i, 1 - cur, i)

        def issue(n_experts):
            for _ in range(n_experts):
                next(fetch, None)

        pos_t = jnp.transpose(pos_ref[:, rows].astype(F32))
        gate_t = jnp.transpose(gate_ref[:, rows])
        for c0 in range(0, SORT_ROWS, LANES):
            c_iota = (lax.broadcasted_iota(I32, (TOKEN_TILE, LANES), 1) + c0).astype(F32)
            wsel = jnp.zeros((TOKEN_TILE, LANES), F32)
            for k in range(TOP_K):
                wsel = jnp.where(c_iota == pos_t[:, k:k + 1], gate_t[:, k:k + 1], wsel)
            w_ref[i, :, c0:c0 + LANES] = wsel.astype(BF16)
            issue(2)

        for r0 in range(0, SORT_ROWS, UNPACK_ROWS):
            a, b = _unpack_pair(ybuf[cur, i, r0:r0 + UNPACK_ROWS, :])
            y_a_ref[i, r0:r0 + UNPACK_ROWS, :] = a
            y_b_ref[i, r0:r0 + UNPACK_ROWS, :] = b
            issue(1)
        issue(N_EXPERTS)
        yield

        w = w_ref[i]
        f = jnp.concatenate([_dot(w, y_a_ref[i]), _dot(w, y_b_ref[i])], axis=-1)
        yield

        gate_f = mod_ref[5:6, :]
        out_ref[rows, :] = x1_ref[rows, :] + _rms(f, nrm_ref[...] * gate_f)

    _skewed([combine_tile(i) for i in range(tps)], phases=3)


def _combine_call(run_dst, run_len, tile_rows, y, pos, gate, x1, mod3, nrm_post):
    batch, seq, _ = x1.shape
    tps = COMBINE_TILES
    span = tps * TOKEN_TILE
    per_row = seq // span
    half = D_MODEL // 2
    grid_spec = pltpu.PrefetchScalarGridSpec(
        num_scalar_prefetch=3,
        grid=(batch * per_row,),
        in_specs=[
            pl.BlockSpec(memory_space=pl.ANY),
            pl.BlockSpec((None, SUBLANES, span), lambda g, *_: (g // per_row, 0, g % per_row)),
            pl.BlockSpec((None, SUBLANES, span), lambda g, *_: (g // per_row, 0, g % per_row)),
            pl.BlockSpec((None, span, D_MODEL), lambda g, *_: (g // per_row, g % per_row, 0)),
            pl.BlockSpec((None, 6, D_MODEL), lambda g, *_: (g // per_row, 0, 0)),
            pl.BlockSpec((1, D_MODEL), lambda g, *_: (0, 0)),
        ],
        out_specs=pl.BlockSpec((None, span, D_MODEL), lambda g, *_: (g // per_row, g % per_row, 0)),
        scratch_shapes=[
            pltpu.VMEM((2, tps, SORT_ROWS, half), U32),
            pltpu.VMEM((tps, TOKEN_TILE, SORT_ROWS), BF16),
            pltpu.VMEM((tps, SORT_ROWS, half), BF16),
            pltpu.VMEM((tps, SORT_ROWS, half), BF16),
            pltpu.SemaphoreType.DMA((2, tps)),
        ],
    )
    return pl.pallas_call(
        _combine_kernel,
        grid_spec=grid_spec,
        out_shape=jax.ShapeDtypeStruct((batch, seq, D_MODEL), F32),
        compiler_params=pltpu.CompilerParams(
            dimension_semantics=("arbitrary",), vmem_limit_bytes=VMEM_LIMIT),
        name="combine",
    )(run_dst, run_len, tile_rows, y, pos, gate, x1, mod3, nrm_post)


def _block_diag(w):
    heads, d, _ = w.shape
    n = heads * d
    tiled = jnp.tile(w.reshape(n, d), (1, heads))
    same_head = (jnp.arange(n)[:, None] // d) == (jnp.arange(n)[None, :] // d)
    return jnp.where(same_head, tiled, 0.0)


def _route_plan(cnt):
    nt = cnt.shape[0]
    run_len = (cnt + (SUBLANES - 1)) // SUBLANES * SUBLANES
    region = jnp.sum(run_len, axis=0)
    region_blk = (region + FFN_BLOCK - 1) // FFN_BLOCK * FFN_BLOCK
    earlier_e = jnp.arange(N_EXPERTS)[None, :] < jnp.arange(N_EXPERTS)[:, None]
    region_start = jnp.sum(jnp.where(earlier_e, region_blk[None, :], 0), axis=1)
    region_end = region_start + region_blk
    earlier_t = jnp.arange(nt)[None, :] < jnp.arange(nt)[:, None]
    run_before = jnp.sum(jnp.where(earlier_t[:, :, None], run_len[None, :, :], 0), axis=1)
    run_dst = region_start[None, :] + run_before
    tail_dst = region_start + region
    tail_len = region_blk - region
    total_rows = jnp.sum(region_blk)
    nused = total_rows // FFN_BLOCK
    max_rows = nt * TOKEN_TILE * TOP_K + nt * N_EXPERTS * (SUBLANES - 1) + N_EXPERTS * (FFN_BLOCK - SUBLANES)
    nb = -(-max_rows // FFN_BLOCK)
    blk_start = jnp.arange(nb, dtype=I32) * FFN_BLOCK
    blk_row = jnp.minimum(blk_start, total_rows - 1)
    blk_e = jnp.sum((blk_row[:, None] >= region_end[None, :]).astype(I32), axis=1)
    blk_e = jnp.minimum(blk_e, N_EXPERTS - 1).astype(I32)
    later_used = jnp.logical_and(jnp.arange(N_EXPERTS)[None, :] > jnp.arange(N_EXPERTS)[:, None],
                                 region_blk[None, :] > 0)
    next_e = jnp.min(jnp.where(later_used, jnp.arange(N_EXPERTS)[None, :], N_EXPERTS), axis=1)
    next_e = jnp.where(next_e == N_EXPERTS, -1, next_e)
    tile_rows = jnp.sum(run_len, axis=1)
    edge = ((PLAN_PAD, PLAN_PAD), (0, 0))
    run_dst, run_len = jnp.pad(run_dst, edge), jnp.pad(run_len, edge)
    tile_rows = jnp.pad(tile_rows, PLAN_PAD)
    return dict(run_dst=run_dst.reshape(-1).astype(I32), run_len=run_len.reshape(-1).astype(I32),
                tile_rows=tile_rows.astype(I32), tail_dst=tail_dst.astype(I32),
                tail_len=tail_len.astype(I32), blk_e=blk_e, next_e=next_e.astype(I32),
                nused=nused.reshape(1).astype(I32)), nb * FFN_BLOCK


def kernel(x, c, w_ada, b_ada, norm_mix_pre, norm_mix_post, w_in, conv_w, conv_b, gate_a_w, gate_a_b, gate_x_w, gate_x_b, lru_lambda, pool_w, pool_b, pool_scale, w_out, norm_ffn_pre, norm_ffn_post, router_w, router_b, expert_w1, expert_b1, expert_w2, expert_b2):
    depth = w_ada.shape[0]
    batch = x.shape[0]
    for l in range(depth):
        mod3 = _ada_call(c, w_ada[l], b_ada[l]).reshape(batch, 6, D_MODEL)

        nrm = jnp.stack([norm_mix_pre[l], norm_mix_post[l], norm_ffn_pre[l]])
        lruv = jnp.stack([conv_b[l], gate_a_b[l], gate_x_b[l], lru_lambda[l]])
        wg = jnp.concatenate([_block_diag(gate_a_w[l]), _block_diag(gate_x_w[l])], axis=1).astype(BF16)
        poolv = jnp.stack([pool_b[l], pool_scale[l]])
        wp = _block_diag(pool_w[l]).astype(BF16)
        rw_pad = jnp.pad(router_w[l], ((0, 0), (0, LANES - N_EXPERTS)))
        rw_hi = rw_pad.astype(BF16)
        rw_lo = (rw_pad - rw_hi.astype(F32)).astype(BF16)
        rwh = jnp.concatenate([rw_hi, rw_lo], axis=1)
        rb = router_b[l].reshape(N_EXPERTS, 1)

        x1, h2, gate, pos, cnt = _mixer_call(
            x, mod3, nrm, w_in[l].astype(BF16), conv_w[l], lruv, wg, poolv, wp,
            w_out[l].astype(BF16), rwh, rb)

        plan, cap = _route_plan(cnt[:, :, :, 0].reshape(-1, N_EXPERTS))
        xs = _dispatch_call(plan["run_dst"], plan["run_len"], plan["tile_rows"], plan["tail_dst"],
                            plan["tail_len"], plan["nused"], h2.reshape(-1, D_MODEL), pos, cap)

        b1p = jnp.concatenate([expert_b1[l][:, 0::2], expert_b1[l][:, 1::2]], axis=-1)
        y = _ffn_call(plan["blk_e"], plan["nused"], plan["next_e"], xs, expert_w1[l],
                      b1p.reshape(N_EXPERTS, 1, 2 * D_EXPERT),
                      expert_w2[l], expert_b2[l].reshape(N_EXPERTS, 1, D_MODEL))

        x = _combine_call(plan["run_dst"], plan["run_len"], plan["tile_rows"], y, pos, gate, x1, mod3,
                          norm_ffn_post[l].reshape(1, D_MODEL))
    return x
```

```python
import functools

import jax
import jax.numpy as jnp
from jax import lax
from jax.experimental import pallas as pl
from jax.experimental.pallas import tpu as pltpu

F32 = jnp.float32
BF16 = jnp.bfloat16
I32 = jnp.int32
U32 = jnp.uint32

D_MODEL = 1024
LRU_WIDTH = 512
LRU_HEADS = 8
POOL_WIDTH = 512
POOL_WINDOWS = (2, 4, 8, 16)
POOL_GROUP_DIM = 128
CONV_WIDTH = 4
IN_WIDTH = 2 * LRU_WIDTH + POOL_WIDTH
LRU_C = 8.0
N_EXPERTS = 32
TOP_K = 4
D_EXPERT = 1024
SWIGLU_LIMIT = 7.0
SWIGLU_ALPHA = 1.702
NORM_EPS = 1e-6

SUBLANES = 8
LANES = 128
TOKEN_TILE = 256
FFN_BLOCK = 256
MIXER_BATCH_GROUP = 4
MIXER_PHASES = 6
CONV_CARRY = SUBLANES
POOL_CARRY = 16
SORT_ROWS = 1280
RUN_SIZES = (256, 128, 64, 32, 16, 8)
UNPACK_ROWS = 64
VMEM_LIMIT = 52 * 1024 * 1024


def _rms(v, g):
    return v * lax.rsqrt(jnp.mean(v * v, axis=-1, keepdims=True) + NORM_EPS) * g


def _dot(a, b):
    return jnp.dot(a, b, preferred_element_type=F32)


def _dot_nt(a, b):
    return lax.dot_general(a, b, (((1,), (1,)), ((), ())), preferred_element_type=F32)


def _split_bf16(v):
    hi = v.astype(BF16)
    lo = (v - hi.astype(F32)).astype(BF16)
    return hi, lo


def _split_bf16_exact(v):
    hi = v.astype(BF16)
    return hi, v - hi.astype(F32)


def _pack_pair(a, b):
    ua = lax.bitcast_convert_type(a, U32)
    ub = lax.bitcast_convert_type(b, U32)
    return (ua >> 16) | (ub & jnp.uint32(0xFFFF0000))


def _unpack_pair(w):
    a = lax.bitcast_convert_type(w << 16, F32).astype(BF16)
    b = lax.bitcast_convert_type(w & jnp.uint32(0xFFFF0000), F32).astype(BF16)
    return a, b


def _ada_kernel(c_ref, w_ref, b_ref, o_ref):
    c = c_ref[...]
    batch = c.shape[0]
    ca = c * jax.nn.sigmoid(c)
    ca_pad = jnp.concatenate([ca, jnp.zeros((SUBLANES - batch, ca.shape[1]), F32)], axis=0)
    ca_t = jnp.transpose(ca_pad)
    w = w_ref[...]
    for b in range(batch):
        o_ref[b:b + 1, :] = jnp.sum(w * ca_t[:, b:b + 1], axis=0, keepdims=True) + b_ref[...]


def _ada_call(c, w_ada, b_ada):
    batch = c.shape[0]
    n_out = w_ada.shape[1]
    bn = D_MODEL
    return pl.pallas_call(
        _ada_kernel,
        grid=(n_out // bn,),
        in_specs=[
            pl.BlockSpec((batch, D_MODEL), lambda j: (0, 0)),
            pl.BlockSpec((D_MODEL, bn), lambda j: (0, j)),
            pl.BlockSpec((1, bn), lambda j: (0, j)),
        ],
        out_specs=pl.BlockSpec((batch, bn), lambda j: (0, j)),
        out_shape=jax.ShapeDtypeStruct((batch, n_out), F32),
        name="ada",
    )(c, w_ada, b_ada.reshape(1, n_out))


def _lru_scan(a, b, h0, out_ref):
    ts, c = a.shape
    groups = ts // SUBLANES
    a3 = a.reshape(groups, SUBLANES, c)
    b3 = b.reshape(groups, SUBLANES, c)
    row = lax.broadcasted_iota(I32, (groups, SUBLANES, c), 1)
    d = 1
    while d < SUBLANES:
        a_sh = pltpu.roll(a3, d, axis=1)
        b_sh = pltpu.roll(b3, d, axis=1)
        m = row >= d
        b3 = jnp.where(m, a3 * b_sh + b3, b3)
        a3 = jnp.where(m, a3 * a_sh, a3)
        d *= 2
    h = h0
    for g in range(groups):
        hg = a3[g] * h + b3[g]
        out_ref[pl.ds(g * SUBLANES, SUBLANES), :] = hg
        h = hg[SUBLANES - 1:SUBLANES, :]
    return h


def _mixer_kernel(x_ref, mod_ref, nrm_ref, w_in_ref, convw_ref, lruv_ref, wg_ref,
                  poolv_ref, wp_ref, w_out_ref, rwh_ref, rb_ref, rankw_ref,
                  x1_ref, h2_ref, gate_ref, pos_ref, cnt_ref,
                  xa_ext, xb_ext, hcar, hbuf):
    s = pl.program_id(1)
    shared = (nrm_ref, w_in_ref, convw_ref, lruv_ref, wg_ref, poolv_ref, wp_ref, w_out_ref,
              rwh_ref, rb_ref, rankw_ref)
    per_row = (x_ref, mod_ref, x1_ref, h2_ref, gate_ref, pos_ref, cnt_ref, xa_ext, xb_ext, hcar, hbuf)

    @pl.when(s == 0)
    def _():
        xa_ext[:, 0:CONV_CARRY, :] = jnp.zeros((xa_ext.shape[0], CONV_CARRY, LRU_WIDTH), F32)
        xb_ext[:, 0:POOL_CARRY, :] = jnp.zeros((xb_ext.shape[0], POOL_CARRY, POOL_WIDTH), F32)
        hcar[...] = jnp.zeros(hcar.shape, F32)

    tiles = [_mixer_tile(s, shared, *(ref.at[p] for ref in per_row)) for p in range(x_ref.shape[0])]
    _skewed(tiles, MIXER_PHASES - 1, finish=False)
    for tile in tiles:
        assert next(tile, "done") == "done"


def _mixer_tile(s, shared, x_ref, mod_ref, x1_ref, h2_ref, gate_ref, pos_ref, cnt_ref,
                xa_ext, xb_ext, hcar, hbuf):
    (nrm_ref, w_in_ref, convw_ref, lruv_ref, wg_ref, poolv_ref, wp_ref, w_out_ref,
     rwh_ref, rb_ref, rankw_ref) = shared
    ts = TOKEN_TILE
    x = x_ref[...]
    mod = mod_ref[...]
    shift_m, scale_m, gate_m = mod[0:1], mod[1:2], mod[2:3]
    shift_f, scale_f = mod[3:4], mod[4:5]
    nrm = nrm_ref[...]

    h = _rms(x, nrm[0:1] * (1.0 + scale_m)) + shift_m
    u = _dot(h.astype(BF16), w_in_ref[...])
    yield
    xa_raw = u[:, :LRU_WIDTH]
    ga = u[:, LRU_WIDTH:2 * LRU_WIDTH]
    xb = u[:, 2 * LRU_WIDTH:]

    lruv = lruv_ref[...]
    conv_b, gate_a_b, gate_x_b, lam = lruv[0:1], lruv[1:2], lruv[2:3], lruv[3:4]
    convw = convw_ref[...]
    xa_ext[CONV_CARRY:, :] = xa_raw
    ext = xa_ext[...]
    acc = ext * convw[0:1]
    for k in range(1, CONV_WIDTH):
        acc = acc + pltpu.roll(ext, k, axis=0) * convw[k:k + 1]
    xa = acc[CONV_CARRY:, :] + conv_b
    xa_ext[0:CONV_CARRY, :] = xa_raw[ts - CONV_CARRY:, :]

    g = _dot(xa.astype(BF16), wg_ref[...])
    yield
    r = jax.nn.sigmoid(g[:, :LRU_WIDTH] + gate_a_b)
    i = jax.nn.sigmoid(g[:, LRU_WIDTH:] + gate_x_b)
    softplus_neg_lam = jnp.maximum(-lam, 0.0) + jnp.log(1.0 + jnp.exp(-jnp.abs(lam)))
    log_a = -LRU_C * r * softplus_neg_lam
    a = jnp.exp(log_a)
    one_minus_a2 = 1.0 - a * a
    mult = jnp.where(one_minus_a2 > 0.0, one_minus_a2 * lax.rsqrt(one_minus_a2), 0.0)
    bterm = mult * (i * xa)
    h_last = _lru_scan(a, bterm, hcar[0:1, :], hbuf)
    hcar[0:1, :] = h_last
    ya = hbuf[...] * jax.nn.gelu(ga)
    yield

    xb_ext[POOL_CARRY:, :] = xb
    t_glob = s * ts + lax.broadcasted_iota(I32, (ts, POOL_GROUP_DIM), 0)
    pooled = []
    for gi, w in enumerate(POOL_WINDOWS):
        lo = gi * POOL_GROUP_DIM
        e = xb_ext[:, lo:lo + POOL_GROUP_DIM]
        acc_w = e
        span = 1
        while span < w:
            acc_w = acc_w + pltpu.roll(acc_w, span, axis=0)
            span *= 2
        count = jnp.minimum(t_glob + 1, w).astype(F32)
        pooled.append(acc_w[POOL_CARRY:, :] / count - xb[:, lo:lo + POOL_GROUP_DIM])
    xb_ext[0:POOL_CARRY, :] = xb[ts - POOL_CARRY:, :]
    p = jnp.concatenate(pooled, axis=-1)
    poolv = poolv_ref[...]
    yb = (_dot(p.astype(BF16), wp_ref[...]) + poolv[0:1]) * poolv[1:2]
    yield

    mix =_dot(jnp.concatenate([ya, yb], axis=-1).astype(BF16), w_out_ref[...])
    x1 = x + _rms(mix, nrm[1:2] * gate_m)
    x1_ref[...] = x1

    h2 = _rms(x1, nrm[2:3] * (1.0 + scale_f)) + shift_f
    h2_hi, h2_lo = _split_bf16(h2)
    h2_ref[...] = h2_hi
    yield

    hi_terms = _dot(h2_hi, rwh_ref[...])
    lo_term = _dot(h2_lo, rwh_ref[:, 0:LANES])
    logits_t = hi_terms[:, :LANES] + hi_terms[:, LANES:] + lo_term
    logits = jnp.transpose(logits_t)[0:N_EXPERTS, :] + rb_ref[...]

    e_iota = lax.broadcasted_iota(I32, (N_EXPERTS, ts), 0)
    beaten = jnp.zeros((N_EXPERTS, ts), F32)
    for e2 in range(N_EXPERTS):
        other = logits[e2:e2 + 1, :]
        ahead_strict = jnp.where(other > logits, 1.0, 0.0)
        ahead_or_tie = jnp.where(other >= logits, 1.0, 0.0)
        beaten = beaten + jnp.where(e_iota > e2, ahead_or_tie, ahead_strict)
    sels = [beaten == float(k) for k in range(TOP_K)]
    vals = [jnp.sum(jnp.where(sel, logits, 0.0), axis=0, keepdims=True) for sel in sels]
    exps = [jnp.exp(v - vals[0]) for v in vals]
    denom = exps[0] + exps[1] + exps[2] + exps[3]
    gates = [ex / denom for ex in exps]

    member = jnp.where(beaten < float(TOP_K), 1.0, 0.0)
    rank_cnt = _dot(member.astype(BF16), rankw_ref[...])
    rank, n_e = rank_cnt[:, :ts], rank_cnt[:, ts:].astype(I32)
    run_len = ((n_e + (SUBLANES - 1)) >> 3) << 3
    ltri = (lax.broadcasted_iota(I32, (N_EXPERTS, N_EXPERTS), 1)
            < lax.broadcasted_iota(I32, (N_EXPERTS, N_EXPERTS), 0))
    run_off = _dot(jnp.where(ltri, 1.0, 0.0).astype(BF16), run_len.astype(F32).astype(BF16))
    slot = rank + run_off
    pos_rows = [jnp.sum(jnp.where(sel, slot, 0.0), axis=0, keepdims=True) for sel in sels]

    zeros_f = jnp.zeros((SUBLANES - TOP_K, ts), F32)
    gate_ref[...] = jnp.concatenate(gates + [zeros_f], axis=0)
    pos_ref[...] = jnp.concatenate(pos_rows + [zeros_f], axis=0).astype(I32)
    cnt_ref[...] = n_e[:, :LANES]


def _mixer_call(x, mod3, nrm, w_in_b, convw, lruv, wg, poolv, wp, w_out_b, rwh, rb):
    batch, seq, _ = x.shape
    ns = seq // TOKEN_TILE
    t_row = jnp.arange(TOKEN_TILE)[:, None]
    t_col = jnp.arange(2 * TOKEN_TILE)[None, :]
    rankw = jnp.where(t_col >= TOKEN_TILE, 1.0, jnp.where(t_row < t_col, 1.0, 0.0)).astype(BF16)
    grp = MIXER_BATCH_GROUP if batch % MIXER_BATCH_GROUP == 0 else 1
    const = lambda shape: pl.BlockSpec(shape, lambda b, s: (0,) * len(shape))
    return pl.pallas_call(
        _mixer_kernel,
        grid=(batch // grp, ns),
        in_specs=[
            pl.BlockSpec((grp, TOKEN_TILE, D_MODEL), lambda b, s: (b, s, 0)),
            pl.BlockSpec((grp, 6, D_MODEL), lambda b, s: (b, 0, 0)),
            const(nrm.shape), const(w_in_b.shape), const(convw.shape), const(lruv.shape),
            const(wg.shape), const(poolv.shape), const(wp.shape), const(w_out_b.shape),
            const(rwh.shape), const(rb.shape), const(rankw.shape),
        ],
        out_specs=[
            pl.BlockSpec((grp, TOKEN_TILE, D_MODEL), lambda b, s: (b, s, 0)),
            pl.BlockSpec((grp, TOKEN_TILE, D_MODEL), lambda b, s: (b, s, 0)),
            pl.BlockSpec((grp, SUBLANES, TOKEN_TILE), lambda b, s: (b, 0, s)),
            pl.BlockSpec((grp, SUBLANES, TOKEN_TILE), lambda b, s: (b, 0, s)),
            pl.BlockSpec((grp, None, N_EXPERTS, LANES), lambda b, s: (b, s, 0, 0)),
        ],
        out_shape=[
            jax.ShapeDtypeStruct((batch, seq, D_MODEL), F32),
            jax.ShapeDtypeStruct((batch, seq, D_MODEL), BF16),
            jax.ShapeDtypeStruct((batch, SUBLANES, seq), F32),
            jax.ShapeDtypeStruct((batch, SUBLANES, seq), I32),
            jax.ShapeDtypeStruct((batch, ns, N_EXPERTS, LANES), I32),
        ],
        scratch_shapes=[
            pltpu.VMEM((grp, TOKEN_TILE + CONV_CARRY, LRU_WIDTH), F32),
            pltpu.VMEM((grp, TOKEN_TILE + POOL_CARRY, POOL_WIDTH), F32),
            pltpu.VMEM((grp, SUBLANES, LRU_WIDTH), F32),
            pltpu.VMEM((grp, TOKEN_TILE, LRU_WIDTH), F32),
        ],
        compiler_params=pltpu.CompilerParams(
            dimension_semantics=("arbitrary", "arbitrary"), vmem_limit_bytes=VMEM_LIMIT),
        name="mixer",
    )(x, mod3, nrm, w_in_b, convw, lruv, wg, poolv, wp, w_out_b, rwh, rb, rankw)


def _run_copies(length, src_ref, src_off, dst_ref, dst_off, sem, wait=False):
    for size in RUN_SIZES:
        hit = (length & size) != 0

        @pl.when(hit)
        def _(src_off=src_off, dst_off=dst_off, size=size):
            copy = pltpu.make_async_copy(
                src_ref.at[pl.ds(pl.multiple_of(src_off, SUBLANES), size)],
                dst_ref.at[pl.ds(pl.multiple_of(dst_off, SUBLANES), size)],
                sem)
            if wait:
                copy.wait()
            else:
                copy.start()

        step = jnp.where(hit, size, 0)
        src_off = src_off + step
        dst_off = dst_off + step
    return src_off, dst_off


DISPATCH_TILES = 4
COMBINE_TILES = 2
PLAN_PAD = 2 * max(DISPATCH_TILES, COMBINE_TILES)


def _plan_index(tile, expert):
    return (tile + PLAN_PAD) * N_EXPERTS + expert


def _wait_rows(rows, hbm_ref, vmem_ref, sem):
    @pl.when(rows > 0)
    def _():
        n = pl.multiple_of(rows, SUBLANES)
        pltpu.make_async_copy(hbm_ref.at[pl.ds(0, n)], vmem_ref.at[pl.ds(0, n)], sem).wait()


def _dispatch_kernel(run_dst_ref, run_len_ref, tile_rows_ref, tail_dst_ref, tail_len_ref, nused_ref,
                     h2_ref, pos_ref, xs_ref, sorted_ref, zero_ref, perm_ref, sem, zsem, *, steps):
    j = pl.program_id(0)
    last = steps - 1
    half = D_MODEL // 2

    @pl.when(j == 0)
    def _():
        zero_ref[...] = jnp.zeros(zero_ref.shape, U32)

        for wait in (False, True):
            def tail_body(e, carry, wait=wait):
                _run_copies(tail_len_ref[e], zero_ref, jnp.int32(0), xs_ref, tail_dst_ref[e], zsem,
                            wait=wait)
                return carry

            lax.fori_loop(0, N_EXPERTS, tail_body, 0)

            def spare_body(b, carry, wait=wait):
                copy = pltpu.make_async_copy(
                    zero_ref, xs_ref.at[pl.ds(pl.multiple_of(b * FFN_BLOCK, FFN_BLOCK), FFN_BLOCK)], zsem)
                if wait:
                    copy.wait()
                else:
                    copy.start()
                return carry

            lax.fori_loop(nused_ref[0], xs_ref.shape[0] // FFN_BLOCK, spare_body, 0)

    cur = j % 2
    tps = DISPATCH_TILES

    def send_expert(t, s, i, e, src_off):
        k = _plan_index(t, e)
        src_off, _ = _run_copies(run_len_ref[k], sorted_ref.at[s, i], src_off, xs_ref,
                                 run_dst_ref[k], sem.at[s, i])
        return src_off

    def send_tile_stepwise(t, s, i):
        src_off = jnp.int32(0)
        for e in range(N_EXPERTS):
            src_off = send_expert(t, s, i, e, src_off)
            yield

    def wait_tile(t, s, i):
        _wait_rows(tile_rows_ref[t + PLAN_PAD], xs_ref, sorted_ref.at[s, i], sem.at[s, i])

    for i in range(tps):
        wait_tile((j - 2) * tps + i, cur, i)
    def sort_tile(i):
        cols = slice(i * TOKEN_TILE, (i + 1) * TOKEN_TILE)
        send = send_tile_stepwise((j - 1) * tps + i, 1 - cur, i)
        pos = pos_ref[:, cols]
        r_iota = lax.broadcasted_iota(I32, (LANES, TOKEN_TILE), 0)
        for r0 in range(0, SORT_ROWS, LANES):
            pos_in_chunk = pos - r0
            perm = jnp.zeros((LANES, TOKEN_TILE), F32)
            for k in range(TOP_K):
                perm = jnp.where(r_iota == pos_in_chunk[k:k + 1, :], 1.0, perm)
            perm_ref[i, r0:r0 + LANES, :] = perm.astype(BF16)
            for _ in range(4):
                next(send, None)
        yield
        rows = _dot(perm_ref[i], h2_ref[cols, :])
        yield
        sorted_ref[cur, i] = _pack_pair(rows[:, :half], rows[:, half:])

    _skewed([sort_tile(i) for i in range(tps)], phases=3)

    @pl.when(j == last)
    def _():
        for i in range(tps):
            lax.fori_loop(0, N_EXPERTS, functools.partial(send_expert, j * tps + i, cur, i), 0)
        for i in range(tps):
            wait_tile(j * tps + i, cur, i)
            if last >= 1:
                wait_tile((j - 1) * tps + i, 1 - cur, i)


def _dispatch_call(run_dst, run_len, tile_rows, tail_dst, tail_len, nused, h2, pos, cap):
    tokens = h2.shape[0]
    tps = DISPATCH_TILES
    span = tps * TOKEN_TILE
    steps = tokens // span
    per_row = pos.shape[2] // span
    half = D_MODEL // 2
    grid_spec = pltpu.PrefetchScalarGridSpec(
        num_scalar_prefetch=6,
        grid=(steps,),
        in_specs=[
            pl.BlockSpec((span, D_MODEL), lambda j, *_: (j, 0)),
            pl.BlockSpec((None, SUBLANES, span), lambda j, *_: (j // per_row, 0, j % per_row)),
        ],
        out_specs=pl.BlockSpec(memory_space=pl.ANY),
        scratch_shapes=[
            pltpu.VMEM((2, tps, SORT_ROWS, half), U32),
            pltpu.VMEM((FFN_BLOCK, half), U32),
            pltpu.VMEM((tps, SORT_ROWS, TOKEN_TILE), BF16),
            pltpu.SemaphoreType.DMA((2, tps)),
            pltpu.SemaphoreType.DMA(()),
        ],
    )
    return pl.pallas_call(
        functools.partial(_dispatch_kernel, steps=steps),
        grid_spec=grid_spec,
        out_shape=jax.ShapeDtypeStruct((cap, half), U32),
        compiler_params=pltpu.CompilerParams(
            dimension_semantics=("arbitrary",), vmem_limit_bytes=VMEM_LIMIT),
        name="dispatch",
    )(run_dst, run_len, tile_rows, tail_dst, tail_len, nused, h2, pos)


W1_CHUNK = 256


def _ffn_kernel(blk_e_ref, nused_ref, next_e_ref, xs_ref, w1_hbm, b1_ref, w2_hbm, b2_ref, y_ref,
                w1_ref, w2_ref, w1s, w2s, b1s, wsem):
    i = pl.program_id(0)
    half = D_MODEL // 2
    prev = jnp.maximum(i - 1, 0)
    expert = blk_e_ref[i]
    new_expert = jnp.logical_or(i == 0, expert != blk_e_ref[prev])

    def fetch(e):
        return (pltpu.make_async_copy(w1_hbm.at[e], w1_ref, wsem.at[0]),
                pltpu.make_async_copy(w2_hbm.at[e], w2_ref, wsem.at[1]))

    @pl.when(i == 0)
    def _():
        for copy in fetch(expert):
            copy.start()

    @pl.when(jnp.logical_and(i < nused_ref[0], new_expert))
    def _():
        for copy in fetch(expert):
            copy.wait()
        hc = W1_CHUNK // 2
        r = lax.broadcasted_iota(I32, (W1_CHUNK, W1_CHUNK), 0)
        q = lax.broadcasted_iota(I32, (W1_CHUNK, W1_CHUNK), 1)
        src_col = jnp.where(q < hc, 2 * q, 2 * (q - hc) + 1)
        sel = jnp.where(r == src_col, 1.0, 0.0).astype(BF16)
        pad_rows = jnp.zeros((2 * SUBLANES - 3, W1_CHUNK), F32)
        for c in range(2 * D_EXPERT // W1_CHUNK):
            cols = slice(c * W1_CHUNK, (c + 1) * W1_CHUNK)
            chunk = w1_ref[:, cols].astype(BF16)
            b_hi, b_rest = _split_bf16_exact(b1_ref[:, cols])
            b_mid, b_lo = _split_bf16_exact(b_rest)
            b_rows = jnp.concatenate([b_hi.astype(F32), b_mid.astype(F32), b_lo, pad_rows], axis=0)
            lhs = jnp.concatenate([chunk, b_rows.astype(BF16)], axis=0)
            picked = _dot(lhs, sel)
            w_part = picked[:D_MODEL, :].astype(BF16)
            w1s[:, c * hc:(c + 1) * hc] = w_part[:, :hc]
            w1s[:, D_EXPERT + c * hc:D_EXPERT + (c + 1) * hc] = w_part[:, hc:]
            b_part = (picked[D_MODEL:D_MODEL + 1, :] + picked[D_MODEL + 1:D_MODEL + 2, :]
                      + picked[D_MODEL + 2:D_MODEL + 3, :])
            b1s[:, c * hc:(c + 1) * hc] = b_part[:, :hc]
            b1s[:, D_EXPERT + c * hc:D_EXPERT + (c + 1) * hc] = b_part[:, hc:]
        w2s[...] = w2_ref[...].astype(BF16)

        nxt = next_e_ref[expert]

        @pl.when(nxt >= 0)
        def _():
            for copy in fetch(nxt):
                copy.start()

    @pl.when(i < nused_ref[0])
    def _():
        x_a, x_b = _unpack_pair(xs_ref[...])
        u = _dot(jnp.concatenate([x_a, x_b], axis=-1), w1s[...]) + b1s[...]
        glu = jnp.minimum(u[:, :D_EXPERT], SWIGLU_LIMIT)
        lin = jnp.clip(u[:, D_EXPERT:], -SWIGLU_LIMIT, SWIGLU_LIMIT)
        act = glu * jax.nn.sigmoid(SWIGLU_ALPHA * glu) * (lin + 1.0)
        y = _dot(act.astype(BF16), w2s[...]) + b2_ref[...]
        yb = y.astype(BF16).astype(F32)
        y_ref[...] = _pack_pair(yb[:, :half], yb[:, half:])

    @pl.when(i >= nused_ref[0])
    def _():
        y_ref[...] = jnp.zeros(y_ref.shape, U32)


def _ffn_call(blk_e, nused, next_e, xs, w1, b1, w2, b2):
    cap, half = xs.shape
    nb = cap // FFN_BLOCK
    row_blk = lambda i, be, nu, ne: (jnp.minimum(i, nu[0] - 1), 0)
    grid_spec = pltpu.PrefetchScalarGridSpec(
        num_scalar_prefetch=3,
        grid=(nb,),
        in_specs=[
            pl.BlockSpec((FFN_BLOCK, half), row_blk),
            pl.BlockSpec(memory_space=pl.ANY),
            pl.BlockSpec((None, 1, 2 * D_EXPERT), lambda i, be, nu, ne: (be[i], 0, 0)),
            pl.BlockSpec(memory_space=pl.ANY),
            pl.BlockSpec((None, 1, D_MODEL), lambda i, be, nu, ne: (be[i], 0, 0)),
        ],
        out_specs=pl.BlockSpec((FFN_BLOCK, half), lambda i, be, nu, ne: (i, 0)),
        scratch_shapes=[
            pltpu.VMEM((D_MODEL, 2 * D_EXPERT), F32),
            pltpu.VMEM((D_EXPERT, D_MODEL), F32),
            pltpu.VMEM((D_MODEL, 2 * D_EXPERT), BF16),
            pltpu.VMEM((D_EXPERT, D_MODEL), BF16),
            pltpu.VMEM((1, 2 * D_EXPERT), F32),
            pltpu.SemaphoreType.DMA((2,)),
        ],
    )
    return pl.pallas_call(
        _ffn_kernel,
        grid_spec=grid_spec,
        out_shape=jax.ShapeDtypeStruct((cap, half), U32),
        compiler_params=pltpu.CompilerParams(
            dimension_semantics=("arbitrary",), vmem_limit_bytes=VMEM_LIMIT),
        name="ffn",
    )(blk_e, nused, next_e, xs, w1, b1, w2, b2)


def _skewed(tiles, phases, finish=True):
    for step in range(phases + len(tiles) - 1):
        for i, tile in enumerate(tiles):
            if finish and step - i == phases - 1:
                assert next(tile, "done") == "done"
            elif 0 <= step - i < phases:
                next(tile)


def _combine_kernel(run_dst_ref, run_len_ref, tile_rows_ref, y_ref, pos_ref, gate_ref, x1_ref, mod_ref,
                    nrm_ref, out_ref, ybuf, w_ref, y_a_ref, y_b_ref, sem):
    g = pl.program_id(0)
    cur = g % 2
    tps = COMBINE_TILES

    def fetch_expert(t, s, i, e, dst_off):
        k = _plan_index(t, e)
        _, dst_off = _run_copies(run_len_ref[k], y_ref, run_dst_ref[k], ybuf.at[s, i], dst_off,
                                 sem.at[s, i])
        return dst_off

    def fetch_tile_stepwise(t, s, i):
        dst_off = jnp.int32(0)
        for e in range(N_EXPERTS):
            dst_off = fetch_expert(t, s, i, e, dst_off)
            yield

    @pl.when(g == 0)
    def _():
        ybuf[...] = jnp.zeros(ybuf.shape, U32)
        for i in range(tps):
            lax.fori_loop(0, N_EXPERTS, functools.partial(fetch_expert, i, cur, i), 0)

    for i in range(tps):
        _wait_rows(tile_rows_ref[g * tps + i + PLAN_PAD], y_ref, ybuf.at[cur, i], sem.at[cur, i])
    def combine_tile(i):
        rows = slice(i * TOKEN_TILE, (i + 1) * TOKEN_TILE)
        slot = i % 2
        fetch = fetch_tile_stepwise((g + 1) * tps + i, 1 - cur, i)

        def issue(n_experts):
            for _ in range(n_experts):
                next(fetch, None)

        pos_t = jnp.transpose(pos_ref[:, rows].astype(F32))
        gate_t = jnp.transpose(gate_ref[:, rows])
        for c0 in range(0, SORT_ROWS, LANES):
            c_iota = (lax.broadcasted_iota(I32, (TOKEN_TILE, LANES), 1) + c0).astype(F32)
            wsel = jnp.zeros((TOKEN_TILE, LANES), F32)
            for k in range(TOP_K):
                wsel = jnp.where(c_iota == pos_t[:, k:k + 1], gate_t[:, k:k + 1], wsel)
            w_ref[slot, :, c0:c0 + LANES] = wsel.astype(BF16)
            issue(2)

        for r0 in range(0, SORT_ROWS, UNPACK_ROWS):
            a, b = _unpack_pair(ybuf[cur, i, r0:r0 + UNPACK_ROWS, :])
            y_a_ref[slot, r0:r0 + UNPACK_ROWS, :] = a
            y_b_ref[slot, r0:r0 + UNPACK_ROWS, :] = b
            issue(1)
        issue(N_EXPERTS)
        yield

        w = w_ref[slot]
        f = jnp.concatenate([_dot(w, y_a_ref[slot]), _dot(w, y_b_ref[slot])], axis=-1)
        yield

        gate_f = mod_ref[5:6, :]
        out_ref[rows, :] = x1_ref[rows, :] + _rms(f, nrm_ref[...] * gate_f)

    _skewed([combine_tile(i) for i in range(tps)], phases=3)


def _combine_call(run_dst, run_len, tile_rows, y, pos, gate, x1, mod3, nrm_post):
    batch, seq, _ = x1.shape
    tps = COMBINE_TILES
    span = tps * TOKEN_TILE
    per_row = seq // span
    half = D_MODEL // 2
    grid_spec = pltpu.PrefetchScalarGridSpec(
        num_scalar_prefetch=3,
        grid=(batch * per_row,),
        in_specs=[
            pl.BlockSpec(memory_space=pl.ANY),
            pl.BlockSpec((None, SUBLANES, span), lambda g, *_: (g // per_row, 0, g % per_row)),
            pl.BlockSpec((None, SUBLANES, span), lambda g, *_: (g // per_row, 0, g % per_row)),
            pl.BlockSpec((None, span, D_MODEL), lambda g, *_: (g // per_row, g % per_row, 0)),
            pl.BlockSpec((None, 6, D_MODEL), lambda g, *_: (g // per_row, 0, 0)),
            pl.BlockSpec((1, D_MODEL), lambda g, *_: (0, 0)),
        ],
        out_specs=pl.BlockSpec((None, span, D_MODEL), lambda g, *_: (g // per_row, g % per_row, 0)),
        scratch_shapes=[
            pltpu.VMEM((2, tps, SORT_ROWS, half), U32),
            pltpu.VMEM((2, TOKEN_TILE, SORT_ROWS), BF16),
            pltpu.VMEM((2, SORT_ROWS, half), BF16),
            pltpu.VMEM((2, SORT_ROWS, half), BF16),
            pltpu.SemaphoreType.DMA((2, tps)),
        ],
    )
    return pl.pallas_call(
        _combine_kernel,
        grid_spec=grid_spec,
        out_shape=jax.ShapeDtypeStruct((batch, seq, D_MODEL), F32),
        compiler_params=pltpu.CompilerParams(
            dimension_semantics=("arbitrary",), vmem_limit_bytes=VMEM_LIMIT),
        name="combine",
    )(run_dst, run_len, tile_rows, y, pos, gate, x1, mod3, nrm_post)


def _block_diag(w):
    heads, d, _ = w.shape
    n = heads * d
    tiled = jnp.tile(w.reshape(n, d), (1, heads))
    same_head = (jnp.arange(n)[:, None] // d) == (jnp.arange(n)[None, :] // d)
    return jnp.where(same_head, tiled, 0.0)


def _route_plan(cnt):
    nt = cnt.shape[0]
    run_len = (cnt + (SUBLANES - 1)) // SUBLANES * SUBLANES
    region = jnp.sum(run_len, axis=0)
    region_blk = (region + FFN_BLOCK - 1) // FFN_BLOCK * FFN_BLOCK
    earlier_e = jnp.arange(N_EXPERTS)[None, :] < jnp.arange(N_EXPERTS)[:, None]
    region_start = jnp.sum(jnp.where(earlier_e, region_blk[None, :], 0), axis=1)
    region_end = region_start + region_blk
    earlier_t = jnp.arange(nt)[None, :] < jnp.arange(nt)[:, None]
    run_before = jnp.sum(jnp.where(earlier_t[:, :, None], run_len[None, :, :], 0), axis=1)
    run_dst = region_start[None, :] + run_before
    tail_dst = region_start + region
    tail_len = region_blk - region
    total_rows = jnp.sum(region_blk)
    nused = total_rows // FFN_BLOCK
    max_rows = nt * TOKEN_TILE * TOP_K + nt * N_EXPERTS * (SUBLANES - 1) + N_EXPERTS * (FFN_BLOCK - SUBLANES)
    nb = -(-max_rows // FFN_BLOCK)
    blk_start = jnp.arange(nb, dtype=I32) * FFN_BLOCK
    blk_row = jnp.minimum(blk_start, total_rows - 1)
    blk_e = jnp.sum((blk_row[:, None] >= region_end[None, :]).astype(I32), axis=1)
    blk_e = jnp.minimum(blk_e, N_EXPERTS - 1).astype(I32)
    later_used = jnp.logical_and(jnp.arange(N_EXPERTS)[None, :] > jnp.arange(N_EXPERTS)[:, None],
                                 region_blk[None, :] > 0)
    next_e = jnp.min(jnp.where(later_used, jnp.arange(N_EXPERTS)[None, :], N_EXPERTS), axis=1)
    next_e = jnp.where(next_e == N_EXPERTS, -1, next_e)
    tile_rows = jnp.sum(run_len, axis=1)
    edge = ((PLAN_PAD, PLAN_PAD), (0, 0))
    run_dst, run_len = jnp.pad(run_dst, edge), jnp.pad(run_len, edge)
    tile_rows = jnp.pad(tile_rows, PLAN_PAD)
    return dict(run_dst=run_dst.reshape(-1).astype(I32), run_len=run_len.reshape(-1).astype(I32),
                tile_rows=tile_rows.astype(I32), tail_dst=tail_dst.astype(I32),
                tail_len=tail_len.astype(I32), blk_e=blk_e, next_e=next_e.astype(I32),
                nused=nused.reshape(1).astype(I32)), nb * FFN_BLOCK


def kernel(x, c, w_ada, b_ada, norm_mix_pre, norm_mix_post, w_in, conv_w, conv_b, gate_a_w, gate_a_b, gate_x_w, gate_x_b, lru_lambda, pool_w, pool_b, pool_scale, w_out, norm_ffn_pre, norm_ffn_post, router_w, router_b, expert_w1, expert_b1, expert_w2, expert_b2):
    depth = w_ada.shape[0]
    batch = x.shape[0]
    for l in range(depth):
        mod3 = _ada_call(c, w_ada[l], b_ada[l]).reshape(batch, 6, D_MODEL)

        nrm = jnp.stack([norm_mix_pre[l], norm_mix_post[l], norm_ffn_pre[l]])
        lruv = jnp.stack([conv_b[l], gate_a_b[l], gate_x_b[l], lru_lambda[l]])
        wg = jnp.concatenate([_block_diag(gate_a_w[l]), _block_diag(gate_x_w[l])], axis=1).astype(BF16)
        poolv = jnp.stack([pool_b[l], pool_scale[l]])
        wp = _block_diag(pool_w[l]).astype(BF16)
        rw_pad = jnp.pad(router_w[l], ((0, 0), (0, LANES - N_EXPERTS)))
        rw_hi = rw_pad.astype(BF16)
        rw_lo = (rw_pad - rw_hi.astype(F32)).astype(BF16)
        rwh = jnp.concatenate([rw_hi, rw_lo], axis=1)
        rb = router_b[l].reshape(N_EXPERTS, 1)

        x1, h2, gate, pos, cnt = _mixer_call(
            x, mod3, nrm, w_in[l].astype(BF16), conv_w[l], lruv, wg, poolv, wp,
            w_out[l].astype(BF16), rwh, rb)

        plan, cap = _route_plan(cnt[:, :, :, 0].reshape(-1, N_EXPERTS))
        xs = _dispatch_call(plan["run_dst"], plan["run_len"], plan["tile_rows"], plan["tail_dst"],
                            plan["tail_len"], plan["nused"], h2.reshape(-1, D_MODEL), pos, cap)

        y = _ffn_call(plan["blk_e"], plan["nused"], plan["next_e"], xs, expert_w1[l],
                      expert_b1[l].reshape(N_EXPERTS, 1, 2 * D_EXPERT),
                      expert_w2[l], expert_b2[l].reshape(N_EXPERTS, 1, D_MODEL))

        x = _combine_call(plan["run_dst"], plan["run_len"], plan["tile_rows"], y, pos, gate, x1, mod3,
                          norm_ffn_post[l].reshape(1, D_MODEL))
    return x
```

```python
import functools

import jax
import jax.numpy as jnp
from jax import lax
from jax.experimental import pallas as pl
from jax.experimental.pallas import tpu as pltpu

F32 = jnp.float32
BF16 = jnp.bfloat16
I32 = jnp.int32
U32 = jnp.uint32

D_MODEL = 1024
LRU_WIDTH = 512
LRU_HEADS = 8
POOL_WIDTH = 512
POOL_WINDOWS = (2, 4, 8, 16)
POOL_GROUP_DIM = 128
CONV_WIDTH = 4
IN_WIDTH = 2 * LRU_WIDTH + POOL_WIDTH
LRU_C = 8.0
N_EXPERTS = 32
TOP_K = 4
D_EXPERT = 1024
SWIGLU_LIMIT = 7.0
SWIGLU_ALPHA = 1.702
NORM_EPS = 1e-6

SUBLANES = 8
LANES = 128
TOKEN_TILE = 256
FFN_BLOCK = 256
MIXER_BATCH_GROUP = 4
MIXER_PHASES = 6
CONV_CARRY = SUBLANES
POOL_CARRY = 16
SORT_ROWS = 1280
RUN_SIZES = (256, 128, 64, 32, 16, 8)
UNPACK_ROWS = 64
VMEM_LIMIT = 52 * 1024 * 1024


def _rms(v, g):
    return v * lax.rsqrt(jnp.mean(v * v, axis=-1, keepdims=True) + NORM_EPS) * g


def _dot(a, b):
    return jnp.dot(a, b, preferred_element_type=F32)


def _dot_nt(a, b):
    return lax.dot_general(a, b, (((1,), (1,)), ((), ())), preferred_element_type=F32)


def _split_bf16(v):
    hi = v.astype(BF16)
    lo = (v - hi.astype(F32)).astype(BF16)
    return hi, lo


def _split_bf16_exact(v):
    hi = v.astype(BF16)
    return hi, v - hi.astype(F32)


def _pack_pair(a, b):
    ua = lax.bitcast_convert_type(a, U32)
    ub = lax.bitcast_convert_type(b, U32)
    return (ua >> 16) | (ub & jnp.uint32(0xFFFF0000))


def _unpack_pair(w):
    a = lax.bitcast_convert_type(w << 16, F32).astype(BF16)
    b = lax.bitcast_convert_type(w & jnp.uint32(0xFFFF0000), F32).astype(BF16)
    return a, b


def _ada_kernel(c_ref, w_ref, b_ref, o_ref):
    c = c_ref[...]
    batch = c.shape[0]
    ca = c * jax.nn.sigmoid(c)
    ca_pad = jnp.concatenate([ca, jnp.zeros((SUBLANES - batch, ca.shape[1]), F32)], axis=0)
    ca_t = jnp.transpose(ca_pad)
    w = w_ref[...]
    for b in range(batch):
        o_ref[b:b + 1, :] = jnp.sum(w * ca_t[:, b:b + 1], axis=0, keepdims=True) + b_ref[...]


def _ada_call(c, w_ada, b_ada):
    batch = c.shape[0]
    n_out = w_ada.shape[1]
    bn = D_MODEL
    return pl.pallas_call(
        _ada_kernel,
        grid=(n_out // bn,),
        in_specs=[
            pl.BlockSpec((batch, D_MODEL), lambda j: (0, 0)),
            pl.BlockSpec((D_MODEL, bn), lambda j: (0, j)),
            pl.BlockSpec((1, bn), lambda j: (0, j)),
        ],
        out_specs=pl.BlockSpec((batch, bn), lambda j: (0, j)),
        out_shape=jax.ShapeDtypeStruct((batch, n_out), F32),
        name="ada",
    )(c, w_ada, b_ada.reshape(1, n_out))


SCAN_CHUNKS = SUBLANES
SCAN_PITCH = 40


def _lru_scan(a, b, h0, a_scr, b_scr, out_ref):
    ts, c = a.shape
    steps = ts // SCAN_CHUNKS
    tiles = c // LANES
    for s in range(SCAN_CHUNKS):
        for lt in range(tiles):
            cols = slice(lt * LANES, (lt + 1) * LANES)
            a_scr[lt, s * SCAN_PITCH:s * SCAN_PITCH + steps, :] = a[s * steps:(s + 1) * steps, cols]
            b_scr[lt, s * SCAN_PITCH:s * SCAN_PITCH + steps, :] = b[s * steps:(s + 1) * steps, cols]
    prod = [jnp.ones((SCAN_CHUNKS, LANES), F32)] * tiles
    state = [jnp.zeros((SCAN_CHUNKS, LANES), F32)] * tiles
    for g in range(steps):
        rows = pl.ds(g, SCAN_CHUNKS, stride=SCAN_PITCH)
        for lt in range(tiles):
            a_g = a_scr[lt, rows, :]
            prod[lt] = a_g * prod[lt]
            state[lt] = a_g * state[lt] + b_scr[lt, rows, :]
            a_scr[lt, rows, :] = prod[lt]
            b_scr[lt, rows, :] = state[lt]
    prod = jnp.concatenate(prod, axis=-1)
    state = jnp.concatenate(state, axis=-1)
    h_in = h0
    for s in range(SCAN_CHUNKS):
        lo = s * SCAN_PITCH
        a_cum = jnp.concatenate([a_scr[lt, lo:lo + steps, :] for lt in range(tiles)], axis=-1)
        h_loc = jnp.concatenate([b_scr[lt, lo:lo + steps, :] for lt in range(tiles)], axis=-1)
        out_ref[s * steps:(s + 1) * steps, :] = a_cum * h_in + h_loc
        h_in = prod[s:s + 1, :] * h_in + state[s:s + 1, :]
    return h_in


def _mixer_kernel(x_ref, mod_ref, nrm_ref, w_in_ref, convw_ref, lruv_ref, wg_ref,
                  poolv_ref, wp_ref, w_out_ref, rwh_ref, rb_ref, rankw_ref,
                  x1_ref, h2_ref, gate_ref, pos_ref, cnt_ref,
                  xa_ext, xb_ext, hcar, hbuf, sa_scr, sb_scr):
    s = pl.program_id(1)
    shared = (nrm_ref, w_in_ref, convw_ref, lruv_ref, wg_ref, poolv_ref, wp_ref, w_out_ref,
              rwh_ref, rb_ref, rankw_ref)
    per_row = (x_ref, mod_ref, x1_ref, h2_ref, gate_ref, pos_ref, cnt_ref, xa_ext, xb_ext, hcar, hbuf, sa_scr, sb_scr)

    @pl.when(s == 0)
    def _():
        xa_ext[:, 0:CONV_CARRY, :] = jnp.zeros((xa_ext.shape[0], CONV_CARRY, LRU_WIDTH), F32)
        xb_ext[:, 0:POOL_CARRY, :] = jnp.zeros((xb_ext.shape[0], POOL_CARRY, POOL_WIDTH), F32)
        hcar[...] = jnp.zeros(hcar.shape, F32)

    tiles = [_mixer_tile(s, shared, *(ref.at[p] for ref in per_row)) for p in range(x_ref.shape[0])]
    _skewed(tiles, MIXER_PHASES - 1, finish=False)
    for tile in tiles:
        assert next(tile, "done") == "done"


def _mixer_tile(s, shared, x_ref, mod_ref, x1_ref, h2_ref, gate_ref, pos_ref, cnt_ref,
                xa_ext, xb_ext, hcar, hbuf, sa_scr, sb_scr):
    (nrm_ref, w_in_ref, convw_ref, lruv_ref, wg_ref, poolv_ref, wp_ref, w_out_ref,
     rwh_ref, rb_ref, rankw_ref) = shared
    ts = TOKEN_TILE
    x = x_ref[...]
    mod = mod_ref[...]
    shift_m, scale_m, gate_m = mod[0:1], mod[1:2], mod[2:3]
    shift_f, scale_f = mod[3:4], mod[4:5]
    nrm = nrm_ref[...]

    h = _rms(x, nrm[0:1] * (1.0 + scale_m)) + shift_m
    u = _dot(h.astype(BF16), w_in_ref[...])
    yield
    xa_raw = u[:, :LRU_WIDTH]
    ga = u[:, LRU_WIDTH:2 * LRU_WIDTH]
    xb = u[:, 2 * LRU_WIDTH:]

    lruv = lruv_ref[...]
    conv_b, gate_a_b, gate_x_b, lam = lruv[0:1], lruv[1:2], lruv[2:3], lruv[3:4]
    convw = convw_ref[...]
    xa_ext[CONV_CARRY:, :] = xa_raw
    ext = xa_ext[...]
    acc = ext * convw[0:1]
    for k in range(1, CONV_WIDTH):
        acc = acc + pltpu.roll(ext, k, axis=0) * convw[k:k + 1]
    xa = acc[CONV_CARRY:, :] + conv_b
    xa_ext[0:CONV_CARRY, :] = xa_raw[ts - CONV_CARRY:, :]

    g = _dot(xa.astype(BF16), wg_ref[...])
    yield
    r = jax.nn.sigmoid(g[:, :LRU_WIDTH] + gate_a_b)
    i = jax.nn.sigmoid(g[:, LRU_WIDTH:] + gate_x_b)
    softplus_neg_lam = jnp.maximum(-lam, 0.0) + jnp.log(1.0 + jnp.exp(-jnp.abs(lam)))
    log_a = -LRU_C * r * softplus_neg_lam
    a = jnp.exp(log_a)
    one_minus_a2 = 1.0 - a * a
    mult = jnp.where(one_minus_a2 > 0.0, one_minus_a2 * lax.rsqrt(one_minus_a2), 0.0)
    bterm = mult * (i * xa)
    h_last = _lru_scan(a, bterm, hcar[0:1, :], sa_scr, sb_scr, hbuf)
    hcar[0:1, :] = h_last
    ya = hbuf[...] * jax.nn.gelu(ga)
    yield

    xb_ext[POOL_CARRY:, :] = xb
    t_glob = s * ts + lax.broadcasted_iota(I32, (ts, POOL_GROUP_DIM), 0)
    pooled = []
    for gi, w in enumerate(POOL_WINDOWS):
        lo = gi * POOL_GROUP_DIM
        e = xb_ext[:, lo:lo + POOL_GROUP_DIM]
        acc_w = e
        span = 1
        while span < w:
            acc_w = acc_w + pltpu.roll(acc_w, span, axis=0)
            span *= 2
        count = jnp.minimum(t_glob + 1, w).astype(F32)
        pooled.append(acc_w[POOL_CARRY:, :] / count - xb[:, lo:lo + POOL_GROUP_DIM])
    xb_ext[0:POOL_CARRY, :] = xb[ts - POOL_CARRY:, :]
    p = jnp.concatenate(pooled, axis=-1)
    poolv = poolv_ref[...]
    yb = (_dot(p.astype(BF16), wp_ref[...]) + poolv[0:1]) * poolv[1:2]
    yield

    mix =_dot(jnp.concatenate([ya, yb], axis=-1).astype(BF16), w_out_ref[...])
    x1 = x + _rms(mix, nrm[1:2] * gate_m)
    x1_ref[...] = x1

    h2 = _rms(x1, nrm[2:3] * (1.0 + scale_f)) + shift_f
    h2_hi, h2_lo = _split_bf16(h2)
    h2_ref[...] = h2_hi
    yield

    hi_terms = _dot(h2_hi, rwh_ref[...])
    lo_term = _dot(h2_lo, rwh_ref[:, 0:LANES])
    logits_t = hi_terms[:, :LANES] + hi_terms[:, LANES:] + lo_term
    logits = jnp.transpose(logits_t)[0:N_EXPERTS, :] + rb_ref[...]

    e_iota = lax.broadcasted_iota(I32, (N_EXPERTS, ts), 0)
    beaten = jnp.zeros((N_EXPERTS, ts), F32)
    for e2 in range(N_EXPERTS):
        other = logits[e2:e2 + 1, :]
        ahead_strict = jnp.where(other > logits, 1.0, 0.0)
        ahead_or_tie = jnp.where(other >= logits, 1.0, 0.0)
        beaten = beaten + jnp.where(e_iota > e2, ahead_or_tie, ahead_strict)
    sels = [beaten == float(k) for k in range(TOP_K)]
    vals = [jnp.sum(jnp.where(sel, logits, 0.0), axis=0, keepdims=True) for sel in sels]
    exps = [jnp.exp(v - vals[0]) for v in vals]
    denom = exps[0] + exps[1] + exps[2] + exps[3]
    gates = [ex / denom for ex in exps]

    member = jnp.where(beaten < float(TOP_K), 1.0, 0.0)
    rank_cnt = _dot(member.astype(BF16), rankw_ref[...])
    rank, n_e = rank_cnt[:, :ts], rank_cnt[:, ts:].astype(I32)
    run_len = ((n_e + (SUBLANES - 1)) >> 3) << 3
    ltri = (lax.broadcasted_iota(I32, (N_EXPERTS, N_EXPERTS), 1)
            < lax.broadcasted_iota(I32, (N_EXPERTS, N_EXPERTS), 0))
    run_off = _dot(jnp.where(ltri, 1.0, 0.0).astype(BF16), run_len.astype(F32).astype(BF16))
    slot = rank + run_off
    pos_rows = [jnp.sum(jnp.where(sel, slot, 0.0), axis=0, keepdims=True) for sel in sels]

    zeros_f = jnp.zeros((SUBLANES - TOP_K, ts), F32)
    gate_ref[...] = jnp.concatenate(gates + [zeros_f], axis=0)
    pos_ref[...] = jnp.concatenate(pos_rows + [zeros_f], axis=0).astype(I32)
    cnt_ref[...] = n_e[:, :LANES]


def _mixer_call(x, mod3, nrm, w_in_b, convw, lruv, wg, poolv, wp, w_out_b, rwh, rb):
    batch, seq, _ = x.shape
    ns = seq // TOKEN_TILE
    t_row = jnp.arange(TOKEN_TILE)[:, None]
    t_col = jnp.arange(2 * TOKEN_TILE)[None, :]
    rankw = jnp.where(t_col >= TOKEN_TILE, 1.0, jnp.where(t_row < t_col, 1.0, 0.0)).astype(BF16)
    grp = MIXER_BATCH_GROUP if batch % MIXER_BATCH_GROUP == 0 else 1
    const = lambda shape: pl.BlockSpec(shape, lambda b, s: (0,) * len(shape))
    return pl.pallas_call(
        _mixer_kernel,
        grid=(batch // grp, ns),
        in_specs=[
            pl.BlockSpec((grp, TOKEN_TILE, D_MODEL), lambda b, s: (b, s, 0)),
            pl.BlockSpec((grp, 6, D_MODEL), lambda b, s: (b, 0, 0)),
            const(nrm.shape), const(w_in_b.shape), const(convw.shape), const(lruv.shape),
            const(wg.shape), const(poolv.shape), const(wp.shape), const(w_out_b.shape),
            const(rwh.shape), const(rb.shape), const(rankw.shape),
        ],
        out_specs=[
            pl.BlockSpec((grp, TOKEN_TILE, D_MODEL), lambda b, s: (b, s, 0)),
            pl.BlockSpec((grp, TOKEN_TILE, D_MODEL), lambda b, s: (b, s, 0)),
            pl.BlockSpec((grp, SUBLANES, TOKEN_TILE), lambda b, s: (b, 0, s)),
            pl.BlockSpec((grp, SUBLANES, TOKEN_TILE), lambda b, s: (b, 0, s)),
            pl.BlockSpec((grp, None, N_EXPERTS, LANES), lambda b, s: (b, s, 0, 0)),
        ],
        out_shape=[
            jax.ShapeDtypeStruct((batch, seq, D_MODEL), F32),
            jax.ShapeDtypeStruct((batch, seq, D_MODEL), BF16),
            jax.ShapeDtypeStruct((batch, SUBLANES, seq), F32),
            jax.ShapeDtypeStruct((batch, SUBLANES, seq), I32),
            jax.ShapeDtypeStruct((batch, ns, N_EXPERTS, LANES), I32),
        ],
        scratch_shapes=[
            pltpu.VMEM((grp, TOKEN_TILE + CONV_CARRY, LRU_WIDTH), F32),
            pltpu.VMEM((grp, TOKEN_TILE + POOL_CARRY, POOL_WIDTH), F32),
            pltpu.VMEM((grp, SUBLANES, LRU_WIDTH), F32),
            pltpu.VMEM((grp, TOKEN_TILE, LRU_WIDTH), F32),
            pltpu.VMEM((grp, LRU_WIDTH // LANES, SCAN_CHUNKS * SCAN_PITCH, LANES), F32),
            pltpu.VMEM((grp, LRU_WIDTH // LANES, SCAN_CHUNKS * SCAN_PITCH, LANES), F32),
        ],
        compiler_params=pltpu.CompilerParams(
            dimension_semantics=("arbitrary", "arbitrary"), vmem_limit_bytes=VMEM_LIMIT),
        name="mixer",
    )(x, mod3, nrm, w_in_b, convw, lruv, wg, poolv, wp, w_out_b, rwh, rb, rankw)


def _run_copies(length, src_ref, src_off, dst_ref, dst_off, sem, wait=False):
    for size in RUN_SIZES:
        hit = (length & size) != 0

        @pl.when(hit)
        def _(src_off=src_off, dst_off=dst_off, size=size):
            copy = pltpu.make_async_copy(
                src_ref.at[pl.ds(pl.multiple_of(src_off, SUBLANES), size)],
                dst_ref.at[pl.ds(pl.multiple_of(dst_off, SUBLANES), size)],
                sem)
            if wait:
                copy.wait()
            else:
                copy.start()

        step = jnp.where(hit, size, 0)
        src_off = src_off + step
        dst_off = dst_off + step
    return src_off, dst_off


DISPATCH_TILES = 4
COMBINE_TILES = 2
PLAN_PAD = 2 * max(DISPATCH_TILES, COMBINE_TILES)


def _plan_index(tile, expert):
    return (tile + PLAN_PAD) * N_EXPERTS + expert


def _wait_rows(rows, hbm_ref, vmem_ref, sem):
    @pl.when(rows > 0)
    def _():
        n = pl.multiple_of(rows, SUBLANES)
        pltpu.make_async_copy(hbm_ref.at[pl.ds(0, n)], vmem_ref.at[pl.ds(0, n)], sem).wait()


def _dispatch_kernel(run_dst_ref, run_len_ref, tile_rows_ref, tail_dst_ref, tail_len_ref, nused_ref,
                     h2_ref, pos_ref, xs_ref, sorted_ref, zero_ref, perm_ref, sem, zsem, *, steps):
    j = pl.program_id(0)
    last = steps - 1
    half = D_MODEL // 2

    @pl.when(j == 0)
    def _():
        zero_ref[...] = jnp.zeros(zero_ref.shape, U32)

        for wait in (False, True):
            def tail_body(e, carry, wait=wait):
                _run_copies(tail_len_ref[e], zero_ref, jnp.int32(0), xs_ref, tail_dst_ref[e], zsem,
                            wait=wait)
                return carry

            lax.fori_loop(0, N_EXPERTS, tail_body, 0)

            def spare_body(b, carry, wait=wait):
                copy = pltpu.make_async_copy(
                    zero_ref, xs_ref.at[pl.ds(pl.multiple_of(b * FFN_BLOCK, FFN_BLOCK), FFN_BLOCK)], zsem)
                if wait:
                    copy.wait()
                else:
                    copy.start()
                return carry

            lax.fori_loop(nused_ref[0], xs_ref.shape[0] // FFN_BLOCK, spare_body, 0)

    cur = j % 2
    tps = DISPATCH_TILES

    def send_expert(t, s, i, e, src_off):
        k = _plan_index(t, e)
        src_off, _ = _run_copies(run_len_ref[k], sorted_ref.at[s, i], src_off, xs_ref,
                                 run_dst_ref[k], sem.at[s, i])
        return src_off

    def send_tile_stepwise(t, s, i):
        src_off = jnp.int32(0)
        for e in range(N_EXPERTS):
            src_off = send_expert(t, s, i, e, src_off)
            yield

    def wait_tile(t, s, i):
        _wait_rows(tile_rows_ref[t + PLAN_PAD], xs_ref, sorted_ref.at[s, i], sem.at[s, i])

    for i in range(tps):
        wait_tile((j - 2) * tps + i, cur, i)
    def sort_tile(i):
        cols = slice(i * TOKEN_TILE, (i + 1) * TOKEN_TILE)
        send = send_tile_stepwise((j - 1) * tps + i, 1 - cur, i)
        pos = pos_ref[:, cols]
        r_iota = lax.broadcasted_iota(I32, (LANES, TOKEN_TILE), 0)
        for r0 in range(0, SORT_ROWS, LANES):
            pos_in_chunk = pos - r0
            perm = jnp.zeros((LANES, TOKEN_TILE), F32)
            for k in range(TOP_K):
                perm = jnp.where(r_iota == pos_in_chunk[k:k + 1, :], 1.0, perm)
            perm_ref[i, r0:r0 + LANES, :] = perm.astype(BF16)
            for _ in range(4):
                next(send, None)
        yield
        rows = _dot(perm_ref[i], h2_ref[cols, :])
        yield
        sorted_ref[cur, i] = _pack_pair(rows[:, :half], rows[:, half:])

    _skewed([sort_tile(i) for i in range(tps)], phases=3)

    @pl.when(j == last)
    def _():
        for i in range(tps):
            lax.fori_loop(0, N_EXPERTS, functools.partial(send_expert, j * tps + i, cur, i), 0)
        for i in range(tps):
            wait_tile(j * tps + i, cur, i)
            if last >= 1:
                wait_tile((j - 1) * tps + i, 1 - cur, i)


def _dispatch_call(run_dst, run_len, tile_rows, tail_dst, tail_len, nused, h2, pos, cap):
    tokens = h2.shape[0]
    tps = DISPATCH_TILES
    span = tps * TOKEN_TILE
    steps = tokens // span
    per_row = pos.shape[2] // span
    half = D_MODEL // 2
    grid_spec = pltpu.PrefetchScalarGridSpec(
        num_scalar_prefetch=6,
        grid=(steps,),
        in_specs=[
            pl.BlockSpec((span, D_MODEL), lambda j, *_: (j, 0)),
            pl.BlockSpec((None, SUBLANES, span), lambda j, *_: (j // per_row, 0, j % per_row)),
        ],
        out_specs=pl.BlockSpec(memory_space=pl.ANY),
        scratch_shapes=[
            pltpu.VMEM((2, tps, SORT_ROWS, half), U32),
            pltpu.VMEM((FFN_BLOCK, half), U32),
            pltpu.VMEM((tps, SORT_ROWS, TOKEN_TILE), BF16),
            pltpu.SemaphoreType.DMA((2, tps)),
            pltpu.SemaphoreType.DMA(()),
        ],
    )
    return pl.pallas_call(
        functools.partial(_dispatch_kernel, steps=steps),
        grid_spec=grid_spec,
        out_shape=jax.ShapeDtypeStruct((cap, half), U32),
        compiler_params=pltpu.CompilerParams(
            dimension_semantics=("arbitrary",), vmem_limit_bytes=VMEM_LIMIT),
        name="dispatch",
    )(run_dst, run_len, tile_rows, tail_dst, tail_len, nused, h2, pos)


W1_CHUNK = 256


def _ffn_kernel(blk_e_ref, nused_ref, next_e_ref, xs_ref, w1_hbm, b1_ref, w2_hbm, b2_ref, y_ref,
                w1_ref, w2_ref, w1s, w2s, b1s, wsem):
    i = pl.program_id(0)
    half = D_MODEL // 2
    prev = jnp.maximum(i - 1, 0)
    expert = blk_e_ref[i]
    new_expert = jnp.logical_or(i == 0, expert != blk_e_ref[prev])

    def fetch(e):
        return (pltpu.make_async_copy(w1_hbm.at[e], w1_ref, wsem.at[0]),
                pltpu.make_async_copy(w2_hbm.at[e], w2_ref, wsem.at[1]))

    @pl.when(i == 0)
    def _():
        for copy in fetch(expert):
            copy.start()

    @pl.when(jnp.logical_and(i < nused_ref[0], new_expert))
    def _():
        for copy in fetch(expert):
            copy.wait()
        hc = W1_CHUNK // 2
        r = lax.broadcasted_iota(I32, (W1_CHUNK, W1_CHUNK), 0)
        q = lax.broadcasted_iota(I32, (W1_CHUNK, W1_CHUNK), 1)
        src_col = jnp.where(q < hc, 2 * q, 2 * (q - hc) + 1)
        sel = jnp.where(r == src_col, 1.0, 0.0).astype(BF16)
        pad_rows = jnp.zeros((2 * SUBLANES - 3, W1_CHUNK), F32)
        for c in range(2 * D_EXPERT // W1_CHUNK):
            cols = slice(c * W1_CHUNK, (c + 1) * W1_CHUNK)
            chunk = w1_ref[:, cols].astype(BF16)
            b_hi, b_rest = _split_bf16_exact(b1_ref[:, cols])
            b_mid, b_lo = _split_bf16_exact(b_rest)
            b_rows = jnp.concatenate([b_hi.astype(F32), b_mid.astype(F32), b_lo, pad_rows], axis=0)
            lhs = jnp.concatenate([chunk, b_rows.astype(BF16)], axis=0)
            picked = _dot(lhs, sel)
            w_part = picked[:D_MODEL, :].astype(BF16)
            w1s[:, c * hc:(c + 1) * hc] = w_part[:, :hc]
            w1s[:, D_EXPERT + c * hc:D_EXPERT + (c + 1) * hc] = w_part[:, hc:]
            b_part = (picked[D_MODEL:D_MODEL + 1, :] + picked[D_MODEL + 1:D_MODEL + 2, :]
                      + picked[D_MODEL + 2:D_MODEL + 3, :])
            b1s[:, c * hc:(c + 1) * hc] = b_part[:, :hc]
            b1s[:, D_EXPERT + c * hc:D_EXPERT + (c + 1) * hc] = b_part[:, hc:]
        w2s[...] = w2_ref[...].astype(BF16)

        nxt = next_e_ref[expert]

        @pl.when(nxt >= 0)
        def _():
            for copy in fetch(nxt):
                copy.start()

    @pl.when(i < nused_ref[0])
    def _():
        x_a, x_b = _unpack_pair(xs_ref[...])
        u = _dot(jnp.concatenate([x_a, x_b], axis=-1), w1s[...]) + b1s[...]
        glu = jnp.minimum(u[:, :D_EXPERT], SWIGLU_LIMIT)
        lin = jnp.clip(u[:, D_EXPERT:], -SWIGLU_LIMIT, SWIGLU_LIMIT)
        act = glu * jax.nn.sigmoid(SWIGLU_ALPHA * glu) * (lin + 1.0)
        y = _dot(act.astype(BF16), w2s[...]) + b2_ref[...]
        yb = y.astype(BF16).astype(F32)
        y_ref[...] = _pack_pair(yb[:, :half], yb[:, half:])

    @pl.when(i >= nused_ref[0])
    def _():
        y_ref[...] = jnp.zeros(y_ref.shape, U32)


def _ffn_call(blk_e, nused, next_e, xs, w1, b1, w2, b2):
    cap, half = xs.shape
    nb = cap // FFN_BLOCK
    row_blk = lambda i, be, nu, ne: (jnp.minimum(i, nu[0] - 1), 0)
    grid_spec = pltpu.PrefetchScalarGridSpec(
        num_scalar_prefetch=3,
        grid=(nb,),
        in_specs=[
            pl.BlockSpec((FFN_BLOCK, half), row_blk),
            pl.BlockSpec(memory_space=pl.ANY),
            pl.BlockSpec((None, 1, 2 * D_EXPERT), lambda i, be, nu, ne: (be[i], 0, 0)),
            pl.BlockSpec(memory_space=pl.ANY),
            pl.BlockSpec((None, 1, D_MODEL), lambda i, be, nu, ne: (be[i], 0, 0)),
        ],
        out_specs=pl.BlockSpec((FFN_BLOCK, half), lambda i, be, nu, ne: (i, 0)),
        scratch_shapes=[
            pltpu.VMEM((D_MODEL, 2 * D_EXPERT), F32),
            pltpu.VMEM((D_EXPERT, D_MODEL), F32),
            pltpu.VMEM((D_MODEL, 2 * D_EXPERT), BF16),
            pltpu.VMEM((D_EXPERT, D_MODEL), BF16),
            pltpu.VMEM((1, 2 * D_EXPERT), F32),
            pltpu.SemaphoreType.DMA((2,)),
        ],
    )
    return pl.pallas_call(
        _ffn_kernel,
        grid_spec=grid_spec,
        out_shape=jax.ShapeDtypeStruct((cap, half), U32),
        compiler_params=pltpu.CompilerParams(
            dimension_semantics=("arbitrary",), vmem_limit_bytes=VMEM_LIMIT),
        name="ffn",
    )(blk_e, nused, next_e, xs, w1, b1, w2, b2)


def _skewed(tiles, phases, finish=True):
    for step in range(phases + len(tiles) - 1):
        for i, tile in enumerate(tiles):
            if finish and step - i == phases - 1:
                assert next(tile, "done") == "done"
            elif 0 <= step - i < phases:
                next(tile)


def _combine_kernel(run_dst_ref, run_len_ref, tile_rows_ref, y_ref, pos_ref, gate_ref, x1_ref, mod_ref,
                    nrm_ref, out_ref, ybuf, w_ref, y_a_ref, y_b_ref, sem):
    g = pl.program_id(0)
    cur = g % 2
    tps = COMBINE_TILES

    def fetch_expert(t, s, i, e, dst_off):
        k = _plan_index(t, e)
        _, dst_off = _run_copies(run_len_ref[k], y_ref, run_dst_ref[k], ybuf.at[s, i], dst_off,
                                 sem.at[s, i])
        return dst_off

    def fetch_tile_stepwise(t, s, i):
        dst_off = jnp.int32(0)
        for e in range(N_EXPERTS):
            dst_off = fetch_expert(t, s, i, e, dst_off)
            yield

    @pl.when(g == 0)
    def _():
        ybuf[...] = jnp.zeros(ybuf.shape, U32)
        for i in range(tps):
            lax.fori_loop(0, N_EXPERTS, functools.partial(fetch_expert, i, cur, i), 0)

    for i in range(tps):
        _wait_rows(tile_rows_ref[g * tps + i + PLAN_PAD], y_ref, ybuf.at[cur, i], sem.at[cur, i])
    def combine_tile(i):
        rows = slice(i * TOKEN_TILE, (i + 1) * TOKEN_TILE)
        slot = i % 2
        fetch = fetch_tile_stepwise((g + 1) * tps + i, 1 - cur, i)

        def issue(n_experts):
            for _ in range(n_experts):
                next(fetch, None)

        pos_t = jnp.transpose(pos_ref[:, rows].astype(F32))
        gate_t = jnp.transpose(gate_ref[:, rows])
        for c0 in range(0, SORT_ROWS, LANES):
            c_iota = (lax.broadcasted_iota(I32, (TOKEN_TILE, LANES), 1) + c0).astype(F32)
            wsel = jnp.zeros((TOKEN_TILE, LANES), F32)
            for k in range(TOP_K):
                wsel = jnp.where(c_iota == pos_t[:, k:k + 1], gate_t[:, k:k + 1], wsel)
            w_ref[slot, :, c0:c0 + LANES] = wsel.astype(BF16)
            issue(2)

        for r0 in range(0, SORT_ROWS, UNPACK_ROWS):
            a, b = _unpack_pair(ybuf[cur, i, r0:r0 + UNPACK_ROWS, :])
            y_a_ref[slot, r0:r0 + UNPACK_ROWS, :] = a
            y_b_ref[slot, r0:r0 + UNPACK_ROWS, :] = b
            issue(1)
        issue(N_EXPERTS)
        yield

        w = w_ref[slot]
        f = jnp.concatenate([_dot(w, y_a_ref[slot]), _dot(w, y_b_ref[slot])], axis=-1)
        yield

        gate_f = mod_ref[5:6, :]
        out_ref[rows, :] = x1_ref[rows, :] + _rms(f, nrm_ref[...] * gate_f)

    _skewed([combine_tile(i) for i in range(tps)], phases=3)


def _combine_call(run_dst, run_len, tile_rows, y, pos, gate, x1, mod3, nrm_post):
    batch, seq, _ = x1.shape
    tps = COMBINE_TILES
    span = tps * TOKEN_TILE
    per_row = seq // span
    half = D_MODEL // 2
    grid_spec = pltpu.PrefetchScalarGridSpec(
        num_scalar_prefetch=3,
        grid=(batch * per_row,),
        in_specs=[
            pl.BlockSpec(memory_space=pl.ANY),
            pl.BlockSpec((None, SUBLANES, span), lambda g, *_: (g // per_row, 0, g % per_row)),
            pl.BlockSpec((None, SUBLANES, span), lambda g, *_: (g // per_row, 0, g % per_row)),
            pl.BlockSpec((None, span, D_MODEL), lambda g, *_: (g // per_row, g % per_row, 0)),
            pl.BlockSpec((None, 6, D_MODEL), lambda g, *_: (g // per_row, 0, 0)),
            pl.BlockSpec((1, D_MODEL), lambda g, *_: (0, 0)),
        ],
        out_specs=pl.BlockSpec((None, span, D_MODEL), lambda g, *_: (g // per_row, g % per_row, 0)),
        scratch_shapes=[
            pltpu.VMEM((2, tps, SORT_ROWS, half), U32),
            pltpu.VMEM((2, TOKEN_TILE, SORT_ROWS), BF16),
            pltpu.VMEM((2, SORT_ROWS, half), BF16),
            pltpu.VMEM((2, SORT_ROWS, half), BF16),
            pltpu.SemaphoreType.DMA((2, tps)),
        ],
    )
    return pl.pallas_call(
        _combine_kernel,
        grid_spec=grid_spec,
        out_shape=jax.ShapeDtypeStruct((batch, seq, D_MODEL), F32),
        compiler_params=pltpu.CompilerParams(
            dimension_semantics=("arbitrary",), vmem_limit_bytes=VMEM_LIMIT),
        name="combine",
    )(run_dst, run_len, tile_rows, y, pos, gate, x1, mod3, nrm_post)


def _block_diag(w):
    heads, d, _ = w.shape
    n = heads * d
    tiled = jnp.tile(w.reshape(n, d), (1, heads))
    same_head = (jnp.arange(n)[:, None] // d) == (jnp.arange(n)[None, :] // d)
    return jnp.where(same_head, tiled, 0.0)


def _route_plan(cnt):
    nt = cnt.shape[0]
    run_len = (cnt + (SUBLANES - 1)) // SUBLANES * SUBLANES
    region = jnp.sum(run_len, axis=0)
    region_blk = (region + FFN_BLOCK - 1) // FFN_BLOCK * FFN_BLOCK
    earlier_e = jnp.arange(N_EXPERTS)[None, :] < jnp.arange(N_EXPERTS)[:, None]
    region_start = jnp.sum(jnp.where(earlier_e, region_blk[None, :], 0), axis=1)
    region_end = region_start + region_blk
    earlier_t = jnp.arange(nt)[None, :] < jnp.arange(nt)[:, None]
    run_before = jnp.sum(jnp.where(earlier_t[:, :, None], run_len[None, :, :], 0), axis=1)
    run_dst = region_start[None, :] + run_before
    tail_dst = region_start + region
    tail_len = region_blk - region
    total_rows = jnp.sum(region_blk)
    nused = total_rows // FFN_BLOCK
    max_rows = nt * TOKEN_TILE * TOP_K + nt * N_EXPERTS * (SUBLANES - 1) + N_EXPERTS * (FFN_BLOCK - SUBLANES)
    nb = -(-max_rows // FFN_BLOCK)
    blk_start = jnp.arange(nb, dtype=I32) * FFN_BLOCK
    blk_row = jnp.minimum(blk_start, total_rows - 1)
    blk_e = jnp.sum((blk_row[:, None] >= region_end[None, :]).astype(I32), axis=1)
    blk_e = jnp.minimum(blk_e, N_EXPERTS - 1).astype(I32)
    later_used = jnp.logical_and(jnp.arange(N_EXPERTS)[None, :] > jnp.arange(N_EXPERTS)[:, None],
                                 region_blk[None, :] > 0)
    next_e = jnp.min(jnp.where(later_used, jnp.arange(N_EXPERTS)[None, :], N_EXPERTS), axis=1)
    next_e = jnp.where(next_e == N_EXPERTS, -1, next_e)
    tile_rows = jnp.sum(run_len, axis=1)
    edge = ((PLAN_PAD, PLAN_PAD), (0, 0))
    run_dst, run_len = jnp.pad(run_dst, edge), jnp.pad(run_len, edge)
    tile_rows = jnp.pad(tile_rows, PLAN_PAD)
    return dict(run_dst=run_dst.reshape(-1).astype(I32), run_len=run_len.reshape(-1).astype(I32),
                tile_rows=tile_rows.astype(I32), tail_dst=tail_dst.astype(I32),
                tail_len=tail_len.astype(I32), blk_e=blk_e, next_e=next_e.astype(I32),
                nused=nused.reshape(1).astype(I32)), nb * FFN_BLOCK


def kernel(x, c, w_ada, b_ada, norm_mix_pre, norm_mix_post, w_in, conv_w, conv_b, gate_a_w, gate_a_b, gate_x_w, gate_x_b, lru_lambda, pool_w, pool_b, pool_scale, w_out, norm_ffn_pre, norm_ffn_post, router_w, router_b, expert_w1, expert_b1, expert_w2, expert_b2):
    depth = w_ada.shape[0]
    batch = x.shape[0]
    for l in range(depth):
        mod3 = _ada_call(c, w_ada[l], b_ada[l]).reshape(batch, 6, D_MODEL)

        nrm = jnp.stack([norm_mix_pre[l], norm_mix_post[l], norm_ffn_pre[l]])
        lruv = jnp.stack([conv_b[l], gate_a_b[l], gate_x_b[l], lru_lambda[l]])
        wg = jnp.concatenate([_block_diag(gate_a_w[l]), _block_diag(gate_x_w[l])], axis=1).astype(BF16)
        poolv = jnp.stack([pool_b[l], pool_scale[l]])
        wp = _block_diag(pool_w[l]).astype(BF16)
        rw_pad = jnp.pad(router_w[l], ((0, 0), (0, LANES - N_EXPERTS)))
        rw_hi = rw_pad.astype(BF16)
        rw_lo = (rw_pad - rw_hi.astype(F32)).astype(BF16)
        rwh = jnp.concatenate([rw_hi, rw_lo], axis=1)
        rb = router_b[l].reshape(N_EXPERTS, 1)

        x1, h2, gate, pos, cnt = _mixer_call(
            x, mod3, nrm, w_in[l].astype(BF16), conv_w[l], lruv, wg, poolv, wp,
            w_out[l].astype(BF16), rwh, rb)

        plan, cap = _route_plan(cnt[:, :, :, 0].reshape(-1, N_EXPERTS))
        xs = _dispatch_call(plan["run_dst"], plan["run_len"], plan["tile_rows"], plan["tail_dst"],
                            plan["tail_len"], plan["nused"], h2.reshape(-1, D_MODEL), pos, cap)

        y = _ffn_call(plan["blk_e"], plan["nused"], plan["next_e"], xs, expert_w1[l],
                      expert_b1[l].reshape(N_EXPERTS, 1, 2 * D_EXPERT),
                      expert_w2[l], expert_b2[l].reshape(N_EXPERTS, 1, D_MODEL))

        x = _combine_call(plan["run_dst"], plan["run_len"], plan["tile_rows"], y, pos, gate, x1, mod3,
                          norm_ffn_post[l].reshape(1, D_MODEL))
    return x
```

```python
import functools

import jax
import jax.numpy as jnp
from jax import lax
from jax.experimental import pallas as pl
from jax.experimental.pallas import tpu as pltpu

F32 = jnp.float32
BF16 = jnp.bfloat16
I32 = jnp.int32
U32 = jnp.uint32

D_MODEL = 1024
LRU_WIDTH = 512
LRU_HEADS = 8
POOL_WIDTH = 512
POOL_WINDOWS = (2, 4, 8, 16)
POOL_GROUP_DIM = 128
CONV_WIDTH = 4
IN_WIDTH = 2 * LRU_WIDTH + POOL_WIDTH
LRU_C = 8.0
N_EXPERTS = 32
TOP_K = 4
D_EXPERT = 1024
SWIGLU_LIMIT = 7.0
SWIGLU_ALPHA = 1.702
NORM_EPS = 1e-6

SUBLANES = 8
LANES = 128
TOKEN_TILE = 256
FFN_BLOCK = 256
MIXER_BATCH_GROUP = 4
MIXER_PHASES = 6
CONV_CARRY = SUBLANES
POOL_CARRY = 16
SORT_ROWS = 1280
RUN_SIZES = (256, 128, 64, 32, 16, 8)
UNPACK_ROWS = 64
VMEM_LIMIT = 52 * 1024 * 1024


def _rms(v, g):
    return v * lax.rsqrt(jnp.mean(v * v, axis=-1, keepdims=True) + NORM_EPS) * g


def _dot(a, b):
    return jnp.dot(a, b, preferred_element_type=F32)


def _dot_nt(a, b):
    return lax.dot_general(a, b, (((1,), (1,)), ((), ())), preferred_element_type=F32)


def _split_bf16(v):
    hi = v.astype(BF16)
    lo = (v - hi.astype(F32)).astype(BF16)
    return hi, lo


def _split_bf16_exact(v):
    hi = v.astype(BF16)
    return hi, v - hi.astype(F32)


def _pack_pair(a, b):
    ua = lax.bitcast_convert_type(a, U32)
    ub = lax.bitcast_convert_type(b, U32)
    return (ua >> 16) | (ub & jnp.uint32(0xFFFF0000))


def _unpack_pair(w):
    a = lax.bitcast_convert_type(w << 16, F32).astype(BF16)
    b = lax.bitcast_convert_type(w & jnp.uint32(0xFFFF0000), F32).astype(BF16)
    return a, b


def _ada_kernel(c_ref, w_ref, b_ref, o_ref):
    c = c_ref[...]
    batch = c.shape[0]
    ca = c * jax.nn.sigmoid(c)
    ca_pad = jnp.concatenate([ca, jnp.zeros((SUBLANES - batch, ca.shape[1]), F32)], axis=0)
    ca_t = jnp.transpose(ca_pad)
    w = w_ref[...]
    for b in range(batch):
        o_ref[b:b + 1, :] = jnp.sum(w * ca_t[:, b:b + 1], axis=0, keepdims=True) + b_ref[...]


def _ada_call(c, w_ada, b_ada):
    batch = c.shape[0]
    n_out = w_ada.shape[1]
    bn = D_MODEL
    return pl.pallas_call(
        _ada_kernel,
        grid=(n_out // bn,),
        in_specs=[
            pl.BlockSpec((batch, D_MODEL), lambda j: (0, 0)),
            pl.BlockSpec((D_MODEL, bn), lambda j: (0, j)),
            pl.BlockSpec((1, bn), lambda j: (0, j)),
        ],
        out_specs=pl.BlockSpec((batch, bn), lambda j: (0, j)),
        out_shape=jax.ShapeDtypeStruct((batch, n_out), F32),
        name="ada",
    )(c, w_ada, b_ada.reshape(1, n_out))


def _lru_scan(a, b, h0, out_ref):
    ts, c = a.shape
    groups = ts // SUBLANES
    a3 = a.reshape(groups, SUBLANES, c)
    b3 = b.reshape(groups, SUBLANES, c)
    row = lax.broadcasted_iota(I32, (groups, SUBLANES, c), 1)
    d = 1
    while d < SUBLANES:
        a_sh = pltpu.roll(a3, d, axis=1)
        b_sh = pltpu.roll(b3, d, axis=1)
        m = row >= d
        b3 = jnp.where(m, a3 * b_sh + b3, b3)
        a3 = jnp.where(m, a3 * a_sh, a3)
        d *= 2
    h = h0
    for g in range(groups):
        hg = a3[g] * h + b3[g]
        out_ref[pl.ds(g * SUBLANES, SUBLANES), :] = hg
        h = hg[SUBLANES - 1:SUBLANES, :]
    return h


def _mixer_kernel(x_ref, mod_ref, nrm_ref, w_in_ref, convw_ref, lruv_ref, wg_ref,
                  poolv_ref, wp_ref, w_out_ref, rwh_ref, rb_ref, rankw_ref,
                  x1_ref, h2_ref, gate_ref, pos_ref, cnt_ref,
                  xa_ext, xb_ext, hcar, hbuf):
    s = pl.program_id(1)
    shared = (nrm_ref, w_in_ref, convw_ref, lruv_ref, wg_ref, poolv_ref, wp_ref, w_out_ref,
              rwh_ref, rb_ref, rankw_ref)
    per_row = (x_ref, mod_ref, x1_ref, h2_ref, gate_ref, pos_ref, cnt_ref, xa_ext, xb_ext, hcar, hbuf)

    @pl.when(s == 0)
    def _():
        xa_ext[:, 0:CONV_CARRY, :] = jnp.zeros((xa_ext.shape[0], CONV_CARRY, LRU_WIDTH), F32)
        xb_ext[:, 0:POOL_CARRY, :] = jnp.zeros((xb_ext.shape[0], POOL_CARRY, POOL_WIDTH), F32)
        hcar[...] = jnp.zeros(hcar.shape, F32)

    tiles = [_mixer_tile(s, shared, *(ref.at[p] for ref in per_row)) for p in range(x_ref.shape[0])]
    _skewed(tiles, MIXER_PHASES - 1, finish=False)
    for tile in tiles:
        assert next(tile, "done") == "done"


def _mixer_tile(s, shared, x_ref, mod_ref, x1_ref, h2_ref, gate_ref, pos_ref, cnt_ref,
                xa_ext, xb_ext, hcar, hbuf):
    (nrm_ref, w_in_ref, convw_ref, lruv_ref, wg_ref, poolv_ref, wp_ref, w_out_ref,
     rwh_ref, rb_ref, rankw_ref) = shared
    ts = TOKEN_TILE
    x = x_ref[...]
    mod = mod_ref[...]
    shift_m, scale_m, gate_m = mod[0:1], mod[1:2], mod[2:3]
    shift_f, scale_f = mod[3:4], mod[4:5]
    nrm = nrm_ref[...]

    h = _rms(x, nrm[0:1] * (1.0 + scale_m)) + shift_m
    u = _dot(h.astype(BF16), w_in_ref[...])
    yield
    xa_raw = u[:, :LRU_WIDTH]
    ga = u[:, LRU_WIDTH:2 * LRU_WIDTH]
    xb = u[:, 2 * LRU_WIDTH:]

    lruv = lruv_ref[...]
    conv_b, gate_a_b, gate_x_b, lam = lruv[0:1], lruv[1:2], lruv[2:3], lruv[3:4]
    convw = convw_ref[...]
    xa_ext[CONV_CARRY:, :] = xa_raw
    ext = xa_ext[...]
    acc = ext * convw[0:1]
    for k in range(1, CONV_WIDTH):
        acc = acc + pltpu.roll(ext, k, axis=0) * convw[k:k + 1]
    xa = acc[CONV_CARRY:, :] + conv_b
    xa_ext[0:CONV_CARRY, :] = xa_raw[ts - CONV_CARRY:, :]

    g = _dot(xa.astype(BF16), wg_ref[...])
    yield
    r = jax.nn.sigmoid(g[:, :LRU_WIDTH] + gate_a_b)
    i = jax.nn.sigmoid(g[:, LRU_WIDTH:] + gate_x_b)
    softplus_neg_lam = jnp.maximum(-lam, 0.0) + jnp.log(1.0 + jnp.exp(-jnp.abs(lam)))
    log_a = -LRU_C * r * softplus_neg_lam
    a = jnp.exp(log_a)
    one_minus_a2 = 1.0 - a * a
    mult = jnp.where(one_minus_a2 > 0.0, one_minus_a2 * lax.rsqrt(one_minus_a2), 0.0)
    bterm = mult * (i * xa)
    h_last = _lru_scan(a, bterm, hcar[0:1, :], hbuf)
    hcar[0:1, :] = h_last
    ya = hbuf[...] * jax.nn.gelu(ga)
    yield

    xb_ext[POOL_CARRY:, :] = xb
    t_glob = s * ts + lax.broadcasted_iota(I32, (ts, POOL_GROUP_DIM), 0)
    pooled = []
    for gi, w in enumerate(POOL_WINDOWS):
        lo = gi * POOL_GROUP_DIM
        e = xb_ext[:, lo:lo + POOL_GROUP_DIM]
        acc_w = e
        span = 1
        while span < w:
            acc_w = acc_w + pltpu.roll(acc_w, span, axis=0)
            span *= 2
        count = jnp.minimum(t_glob + 1, w).astype(F32)
        pooled.append(acc_w[POOL_CARRY:, :] / count - xb[:, lo:lo + POOL_GROUP_DIM])
    xb_ext[0:POOL_CARRY, :] = xb[ts - POOL_CARRY:, :]
    p = jnp.concatenate(pooled, axis=-1)
    poolv = poolv_ref[...]
    yb = (_dot(p.astype(BF16), wp_ref[...]) + poolv[0:1]) * poolv[1:2]
    yield

    mix =_dot(jnp.concatenate([ya, yb], axis=-1).astype(BF16), w_out_ref[...])
    x1 = x + _rms(mix, nrm[1:2] * gate_m)
    x1_ref[...] = x1

    h2 = _rms(x1, nrm[2:3] * (1.0 + scale_f)) + shift_f
    h2_hi, h2_lo = _split_bf16(h2)
    h2_ref[...] = h2_hi
    yield

    hi_terms = _dot(h2_hi, rwh_ref[...])
    lo_term = _dot(h2_lo, rwh_ref[:, 0:LANES])
    logits_t = hi_terms[:, :LANES] + hi_terms[:, LANES:] + lo_term
    logits = jnp.transpose(logits_t)[0:N_EXPERTS, :] + rb_ref[...]

    e_iota = lax.broadcasted_iota(I32, (N_EXPERTS, ts), 0)
    beaten = jnp.zeros((N_EXPERTS, ts), F32)
    for e2 in range(N_EXPERTS):
        other = logits[e2:e2 + 1, :]
        ahead_strict = jnp.where(other > logits, 1.0, 0.0)
        ahead_or_tie = jnp.where(other >= logits, 1.0, 0.0)
        beaten = beaten + jnp.where(e_iota > e2, ahead_or_tie, ahead_strict)
    sels = [beaten == float(k) for k in range(TOP_K)]
    vals = [jnp.sum(jnp.where(sel, logits, 0.0), axis=0, keepdims=True) for sel in sels]
    exps = [jnp.exp(v - vals[0]) for v in vals]
    denom = exps[0] + exps[1] + exps[2] + exps[3]
    gates = [ex / denom for ex in exps]

    member = jnp.where(beaten < float(TOP_K), 1.0, 0.0)
    rank_cnt = _dot(member.astype(BF16), rankw_ref[...])
    rank, n_e = rank_cnt[:, :ts], rank_cnt[:, ts:].astype(I32)
    run_len = ((n_e + (SUBLANES - 1)) >> 3) << 3
    ltri = (lax.broadcasted_iota(I32, (N_EXPERTS, N_EXPERTS), 1)
            < lax.broadcasted_iota(I32, (N_EXPERTS, N_EXPERTS), 0))
    run_off = _dot(jnp.where(ltri, 1.0, 0.0).astype(BF16), run_len.astype(F32).astype(BF16))
    slot = rank + run_off
    pos_rows = [jnp.sum(jnp.where(sel, slot, 0.0), axis=0, keepdims=True) for sel in sels]

    zeros_f = jnp.zeros((SUBLANES - TOP_K, ts), F32)
    gate_ref[...] = jnp.concatenate(gates + [zeros_f], axis=0)
    pos_ref[...] = jnp.concatenate(pos_rows + [zeros_f], axis=0).astype(I32)
    cnt_ref[...] = n_e[:, :LANES]


def _mixer_call(x, mod3, nrm, w_in_b, convw, lruv, wg, poolv, wp, w_out_b, rwh, rb):
    batch, seq, _ = x.shape
    ns = seq // TOKEN_TILE
    t_row = jnp.arange(TOKEN_TILE)[:, None]
    t_col = jnp.arange(2 * TOKEN_TILE)[None, :]
    rankw = jnp.where(t_col >= TOKEN_TILE, 1.0, jnp.where(t_row < t_col, 1.0, 0.0)).astype(BF16)
    grp = MIXER_BATCH_GROUP if batch % MIXER_BATCH_GROUP == 0 else 1
    const = lambda shape: pl.BlockSpec(shape, lambda b, s: (0,) * len(shape))
    return pl.pallas_call(
        _mixer_kernel,
        grid=(batch // grp, ns),
        in_specs=[
            pl.BlockSpec((grp, TOKEN_TILE, D_MODEL), lambda b, s: (b, s, 0)),
            pl.BlockSpec((grp, 6, D_MODEL), lambda b, s: (b, 0, 0)),
            const(nrm.shape), const(w_in_b.shape), const(convw.shape), const(lruv.shape),
            const(wg.shape), const(poolv.shape), const(wp.shape), const(w_out_b.shape),
            const(rwh.shape), const(rb.shape), const(rankw.shape),
        ],
        out_specs=[
            pl.BlockSpec((grp, TOKEN_TILE, D_MODEL), lambda b, s: (b, s, 0)),
            pl.BlockSpec((grp, TOKEN_TILE, D_MODEL), lambda b, s: (b, s, 0)),
            pl.BlockSpec((grp, SUBLANES, TOKEN_TILE), lambda b, s: (b, 0, s)),
            pl.BlockSpec((grp, SUBLANES, TOKEN_TILE), lambda b, s: (b, 0, s)),
            pl.BlockSpec((grp, None, N_EXPERTS, LANES), lambda b, s: (b, s, 0, 0)),
        ],
        out_shape=[
            jax.ShapeDtypeStruct((batch, seq, D_MODEL), F32),
            jax.ShapeDtypeStruct((batch, seq, D_MODEL), BF16),
            jax.ShapeDtypeStruct((batch, SUBLANES, seq), F32),
            jax.ShapeDtypeStruct((batch, SUBLANES, seq), I32),
            jax.ShapeDtypeStruct((batch, ns, N_EXPERTS, LANES), I32),
        ],
        scratch_shapes=[
            pltpu.VMEM((grp, TOKEN_TILE + CONV_CARRY, LRU_WIDTH), F32),
            pltpu.VMEM((grp, TOKEN_TILE + POOL_CARRY, POOL_WIDTH), F32),
            pltpu.VMEM((grp, SUBLANES, LRU_WIDTH), F32),
            pltpu.VMEM((grp, TOKEN_TILE, LRU_WIDTH), F32),
        ],
        compiler_params=pltpu.CompilerParams(
            dimension_semantics=("arbitrary", "arbitrary"), vmem_limit_bytes=VMEM_LIMIT),
        name="mixer",
    )(x, mod3, nrm, w_in_b, convw, lruv, wg, poolv, wp, w_out_b, rwh, rb, rankw)


def _run_copies(length, src_ref, src_off, dst_ref, dst_off, sem, wait=False):
    for size in RUN_SIZES:
        hit = (length & size) != 0

        @pl.when(hit)
        def _(src_off=src_off, dst_off=dst_off, size=size):
            copy = pltpu.make_async_copy(
                src_ref.at[pl.ds(pl.multiple_of(src_off, SUBLANES), size)],
                dst_ref.at[pl.ds(pl.multiple_of(dst_off, SUBLANES), size)],
                sem)
            if wait:
                copy.wait()
            else:
                copy.start()

        step = jnp.where(hit, size, 0)
        src_off = src_off + step
        dst_off = dst_off + step
    return src_off, dst_off


DISPATCH_TILES = 4
COMBINE_TILES = 2
PLAN_PAD = 2 * max(DISPATCH_TILES, COMBINE_TILES)


def _plan_index(tile, expert):
    return (tile + PLAN_PAD) * N_EXPERTS + expert


def _wait_rows(rows, hbm_ref, vmem_ref, sem):
    @pl.when(rows > 0)
    def _():
        n = pl.multiple_of(rows, SUBLANES)
        pltpu.make_async_copy(hbm_ref.at[pl.ds(0, n)], vmem_ref.at[pl.ds(0, n)], sem).wait()


def _dispatch_kernel(run_dst_ref, run_len_ref, tile_rows_ref, tail_dst_ref, tail_len_ref, nused_ref,
                     h2_ref, pos_ref, xs_ref, sorted_ref, zero_ref, perm_ref, sem, zsem, *, steps):
    j = pl.program_id(0)
    last = steps - 1
    half = D_MODEL // 2

    def zero_fill(wait):
        def tail_body(e, carry):
            _run_copies(tail_len_ref[e], zero_ref, jnp.int32(0), xs_ref, tail_dst_ref[e], zsem,
                        wait=wait)
            return carry

        lax.fori_loop(0, N_EXPERTS, tail_body, 0)

        def spare_body(b, carry):
            copy = pltpu.make_async_copy(
                zero_ref, xs_ref.at[pl.ds(pl.multiple_of(b * FFN_BLOCK, FFN_BLOCK), FFN_BLOCK)], zsem)
            if wait:
                copy.wait()
            else:
                copy.start()
            return carry

        lax.fori_loop(nused_ref[0], xs_ref.shape[0] // FFN_BLOCK, spare_body, 0)

    @pl.when(j == 0)
    def _():
        zero_ref[...] = jnp.zeros(zero_ref.shape, U32)
        zero_fill(wait=False)

    cur = j % 2
    tps = DISPATCH_TILES

    def send_expert(t, s, i, e, src_off):
        k = _plan_index(t, e)
        src_off, _ = _run_copies(run_len_ref[k], sorted_ref.at[s, i], src_off, xs_ref,
                                 run_dst_ref[k], sem.at[s, i])
        return src_off

    def send_tile_stepwise(t, s, i):
        src_off = jnp.int32(0)
        for e in range(N_EXPERTS):
            src_off = send_expert(t, s, i, e, src_off)
            yield

    def wait_tile(t, s, i):
        _wait_rows(tile_rows_ref[t + PLAN_PAD], xs_ref, sorted_ref.at[s, i], sem.at[s, i])

    for i in range(tps):
        wait_tile((j - 2) * tps + i, cur, i)
    def sort_tile(i):
        cols = slice(i * TOKEN_TILE, (i + 1) * TOKEN_TILE)
        send = send_tile_stepwise((j - 1) * tps + i, 1 - cur, i)
        pos = pos_ref[:, cols]
        r_iota = lax.broadcasted_iota(I32, (LANES, TOKEN_TILE), 0)
        for r0 in range(0, SORT_ROWS, LANES):
            pos_in_chunk = pos - r0
            perm = jnp.zeros((LANES, TOKEN_TILE), F32)
            for k in range(TOP_K):
                perm = jnp.where(r_iota == pos_in_chunk[k:k + 1, :], 1.0, perm)
            perm_ref[i, r0:r0 + LANES, :] = perm.astype(BF16)
            for _ in range(4):
                next(send, None)
        yield
        rows = _dot(perm_ref[i], h2_ref[cols, :])
        yield
        sorted_ref[cur, i] = _pack_pair(rows[:, :half], rows[:, half:])

    _skewed([sort_tile(i) for i in range(tps)], phases=3)

    @pl.when(j == last)
    def _():
        for i in range(tps):
            lax.fori_loop(0, N_EXPERTS, functools.partial(send_expert, j * tps + i, cur, i), 0)
        for i in range(tps):
            wait_tile(j * tps + i, cur, i)
            if last >= 1:
                wait_tile((j - 1) * tps + i, 1 - cur, i)
        zero_fill(wait=True)


def _dispatch_call(run_dst, run_len, tile_rows, tail_dst, tail_len, nused, h2, pos, cap):
    tokens = h2.shape[0]
    tps = DISPATCH_TILES
    span = tps * TOKEN_TILE
    steps = tokens // span
    per_row = pos.shape[2] // span
    half = D_MODEL // 2
    grid_spec = pltpu.PrefetchScalarGridSpec(
        num_scalar_prefetch=6,
        grid=(steps,),
        in_specs=[
            pl.BlockSpec((span, D_MODEL), lambda j, *_: (j, 0)),
            pl.BlockSpec((None, SUBLANES, span), lambda j, *_: (j // per_row, 0, j % per_row)),
        ],
        out_specs=pl.BlockSpec(memory_space=pl.ANY),
        scratch_shapes=[
            pltpu.VMEM((2, tps, SORT_ROWS, half), U32),
            pltpu.VMEM((FFN_BLOCK, half), U32),
            pltpu.VMEM((tps, SORT_ROWS, TOKEN_TILE), BF16),
            pltpu.SemaphoreType.DMA((2, tps)),
            pltpu.SemaphoreType.DMA(()),
        ],
    )
    return pl.pallas_call(
        functools.partial(_dispatch_kernel, steps=steps),
        grid_spec=grid_spec,
        out_shape=jax.ShapeDtypeStruct((cap, half), U32),
        compiler_params=pltpu.CompilerParams(
            dimension_semantics=("arbitrary",), vmem_limit_bytes=VMEM_LIMIT),
        name="dispatch",
    )(run_dst, run_len, tile_rows, tail_dst, tail_len, nused, h2, pos)


W1_CHUNK = 256


def _ffn_kernel(blk_e_ref, nused_ref, next_e_ref, xs_ref, w1_hbm, b1_ref, w2_hbm, b2_ref, y_ref,
                w1_ref, w2_ref, w1s, w2s, b1s, wsem):
    i = pl.program_id(0)
    half = D_MODEL // 2
    prev = jnp.maximum(i - 1, 0)
    expert = blk_e_ref[i]
    new_expert = jnp.logical_or(i == 0, expert != blk_e_ref[prev])

    def fetch(e):
        return (pltpu.make_async_copy(w1_hbm.at[e], w1_ref, wsem.at[0]),
                pltpu.make_async_copy(w2_hbm.at[e], w2_ref, wsem.at[1]))

    @pl.when(i == 0)
    def _():
        for copy in fetch(expert):
            copy.start()

    @pl.when(jnp.logical_and(i < nused_ref[0], new_expert))
    def _():
        for copy in fetch(expert):
            copy.wait()
        hc = W1_CHUNK // 2
        r = lax.broadcasted_iota(I32, (W1_CHUNK, W1_CHUNK), 0)
        q = lax.broadcasted_iota(I32, (W1_CHUNK, W1_CHUNK), 1)
        src_col = jnp.where(q < hc, 2 * q, 2 * (q - hc) + 1)
        sel = jnp.where(r == src_col, 1.0, 0.0).astype(BF16)
        pad_rows = jnp.zeros((2 * SUBLANES - 3, W1_CHUNK), F32)
        for c in range(2 * D_EXPERT // W1_CHUNK):
            cols = slice(c * W1_CHUNK, (c + 1) * W1_CHUNK)
            chunk = w1_ref[:, cols].astype(BF16)
            b_hi, b_rest = _split_bf16_exact(b1_ref[:, cols])
            b_mid, b_lo = _split_bf16_exact(b_rest)
            b_rows = jnp.concatenate([b_hi.astype(F32), b_mid.astype(F32), b_lo, pad_rows], axis=0)
            lhs = jnp.concatenate([chunk, b_rows.astype(BF16)], axis=0)
            picked = _dot(lhs, sel)
            w_part = picked[:D_MODEL, :].astype(BF16)
            w1s[:, c * hc:(c + 1) * hc] = w_part[:, :hc]
            w1s[:, D_EXPERT + c * hc:D_EXPERT + (c + 1) * hc] = w_part[:, hc:]
            b_part = (picked[D_MODEL:D_MODEL + 1, :] + picked[D_MODEL + 1:D_MODEL + 2, :]
                      + picked[D_MODEL + 2:D_MODEL + 3, :])
            b1s[:, c * hc:(c + 1) * hc] = b_part[:, :hc]
            b1s[:, D_EXPERT + c * hc:D_EXPERT + (c + 1) * hc] = b_part[:, hc:]
        w2s[...] = w2_ref[...].astype(BF16)

        nxt = next_e_ref[expert]

        @pl.when(nxt >= 0)
        def _():
            for copy in fetch(nxt):
                copy.start()

    @pl.when(i < nused_ref[0])
    def _():
        x_a, x_b = _unpack_pair(xs_ref[...])
        u = _dot(jnp.concatenate([x_a, x_b], axis=-1), w1s[...]) + b1s[...]
        glu = jnp.minimum(u[:, :D_EXPERT], SWIGLU_LIMIT)
        lin = jnp.clip(u[:, D_EXPERT:], -SWIGLU_LIMIT, SWIGLU_LIMIT)
        act = glu * jax.nn.sigmoid(SWIGLU_ALPHA * glu) * (lin + 1.0)
        y = _dot(act.astype(BF16), w2s[...]) + b2_ref[...]
        yb = y.astype(BF16).astype(F32)
        y_ref[...] = _pack_pair(yb[:, :half], yb[:, half:])

    @pl.when(i >= nused_ref[0])
    def _():
        y_ref[...] = jnp.zeros(y_ref.shape, U32)


def _ffn_call(blk_e, nused, next_e, xs, w1, b1, w2, b2):
    cap, half = xs.shape
    nb = cap // FFN_BLOCK
    row_blk = lambda i, be, nu, ne: (jnp.minimum(i, nu[0] - 1), 0)
    grid_spec = pltpu.PrefetchScalarGridSpec(
        num_scalar_prefetch=3,
        grid=(nb,),
        in_specs=[
            pl.BlockSpec((FFN_BLOCK, half), row_blk),
            pl.BlockSpec(memory_space=pl.ANY),
            pl.BlockSpec((None, 1, 2 * D_EXPERT), lambda i, be, nu, ne: (be[i], 0, 0)),
            pl.BlockSpec(memory_space=pl.ANY),
            pl.BlockSpec((None, 1, D_MODEL), lambda i, be, nu, ne: (be[i], 0, 0)),
        ],
        out_specs=pl.BlockSpec((FFN_BLOCK, half), lambda i, be, nu, ne: (i, 0)),
        scratch_shapes=[
            pltpu.VMEM((D_MODEL, 2 * D_EXPERT), F32),
            pltpu.VMEM((D_EXPERT, D_MODEL), F32),
            pltpu.VMEM((D_MODEL, 2 * D_EXPERT), BF16),
            pltpu.VMEM((D_EXPERT, D_MODEL), BF16),
            pltpu.VMEM((1, 2 * D_EXPERT), F32),
            pltpu.SemaphoreType.DMA((2,)),
        ],
    )
    return pl.pallas_call(
        _ffn_kernel,
        grid_spec=grid_spec,
        out_shape=jax.ShapeDtypeStruct((cap, half), U32),
        compiler_params=pltpu.CompilerParams(
            dimension_semantics=("arbitrary",), vmem_limit_bytes=VMEM_LIMIT),
        name="ffn",
    )(blk_e, nused, next_e, xs, w1, b1, w2, b2)


def _skewed(tiles, phases, finish=True):
    for step in range(phases + len(tiles) - 1):
        for i, tile in enumerate(tiles):
            if finish and step - i == phases - 1:
                assert next(tile, "done") == "done"
            elif 0 <= step - i < phases:
                next(tile)


def _combine_kernel(run_dst_ref, run_len_ref, tile_rows_ref, y_ref, pos_ref, gate_ref, x1_ref, mod_ref,
                    nrm_ref, out_ref, ybuf, w_ref, y_a_ref, y_b_ref, sem):
    g = pl.program_id(0)
    cur = g % 2
    tps = COMBINE_TILES

    def fetch_expert(t, s, i, e, dst_off):
        k = _plan_index(t, e)
        _, dst_off = _run_copies(run_len_ref[k], y_ref, run_dst_ref[k], ybuf.at[s, i], dst_off,
                                 sem.at[s, i])
        return dst_off

    def fetch_tile_stepwise(t, s, i):
        dst_off = jnp.int32(0)
        for e in range(N_EXPERTS):
            dst_off = fetch_expert(t, s, i, e, dst_off)
            yield

    @pl.when(g == 0)
    def _():
        ybuf[...] = jnp.zeros(ybuf.shape, U32)
        for i in range(tps):
            lax.fori_loop(0, N_EXPERTS, functools.partial(fetch_expert, i, cur, i), 0)

    for i in range(tps):
        _wait_rows(tile_rows_ref[g * tps + i + PLAN_PAD], y_ref, ybuf.at[cur, i], sem.at[cur, i])
    def combine_tile(i):
        rows = slice(i * TOKEN_TILE, (i + 1) * TOKEN_TILE)
        slot = i % 2
        fetch = fetch_tile_stepwise((g + 1) * tps + i, 1 - cur, i)

        def issue(n_experts):
            for _ in range(n_experts):
                next(fetch, None)

        pos_t = jnp.transpose(pos_ref[:, rows].astype(F32))
        gate_t = jnp.transpose(gate_ref[:, rows])
        for c0 in range(0, SORT_ROWS, LANES):
            c_iota = (lax.broadcasted_iota(I32, (TOKEN_TILE, LANES), 1) + c0).astype(F32)
            wsel = jnp.zeros((TOKEN_TILE, LANES), F32)
            for k in range(TOP_K):
                wsel = jnp.where(c_iota == pos_t[:, k:k + 1], gate_t[:, k:k + 1], wsel)
            w_ref[slot, :, c0:c0 + LANES] = wsel.astype(BF16)
            issue(2)

        for r0 in range(0, SORT_ROWS, UNPACK_ROWS):
            a, b = _unpack_pair(ybuf[cur, i, r0:r0 + UNPACK_ROWS, :])
            y_a_ref[slot, r0:r0 + UNPACK_ROWS, :] = a
            y_b_ref[slot, r0:r0 + UNPACK_ROWS, :] = b
            issue(1)
        issue(N_EXPERTS)
        yield

        w = w_ref[slot]
        f = jnp.concatenate([_dot(w, y_a_ref[slot]), _dot(w, y_b_ref[slot])], axis=-1)
        yield

        gate_f = mod_ref[5:6, :]
        out_ref[rows, :] = x1_ref[rows, :] + _rms(f, nrm_ref[...] * gate_f)

    _skewed([combine_tile(i) for i in range(tps)], phases=3)


def _combine_call(run_dst, run_len, tile_rows, y, pos, gate, x1, mod3, nrm_post):
    batch, seq, _ = x1.shape
    tps = COMBINE_TILES
    span = tps * TOKEN_TILE
    per_row = seq // span
    half = D_MODEL // 2
    grid_spec = pltpu.PrefetchScalarGridSpec(
        num_scalar_prefetch=3,
        grid=(batch * per_row,),
        in_specs=[
            pl.BlockSpec(memory_space=pl.ANY),
            pl.BlockSpec((None, SUBLANES, span), lambda g, *_: (g // per_row, 0, g % per_row)),
            pl.BlockSpec((None, SUBLANES, span), lambda g, *_: (g // per_row, 0, g % per_row)),
            pl.BlockSpec((None, span, D_MODEL), lambda g, *_: (g // per_row, g % per_row, 0)),
            pl.BlockSpec((None, 6, D_MODEL), lambda g, *_: (g // per_row, 0, 0)),
            pl.BlockSpec((1, D_MODEL), lambda g, *_: (0, 0)),
        ],
        out_specs=pl.BlockSpec((None, span, D_MODEL), lambda g, *_: (g // per_row, g % per_row, 0)),
        scratch_shapes=[
            pltpu.VMEM((2, tps, SORT_ROWS, half), U32),
            pltpu.VMEM((2, TOKEN_TILE, SORT_ROWS), BF16),
            pltpu.VMEM((2, SORT_ROWS, half), BF16),
            pltpu.VMEM((2, SORT_ROWS, half), BF16),
            pltpu.SemaphoreType.DMA((2, tps)),
        ],
    )
    return pl.pallas_call(
        _combine_kernel,
        grid_spec=grid_spec,
        out_shape=jax.ShapeDtypeStruct((batch, seq, D_MODEL), F32),
        compiler_params=pltpu.CompilerParams(
            dimension_semantics=("arbitrary",), vmem_limit_bytes=VMEM_LIMIT),
        name="combine",
    )(run_dst, run_len, tile_rows, y, pos, gate, x1, mod3, nrm_post)


def _block_diag(w):
    heads, d, _ = w.shape
    n = heads * d
    tiled = jnp.tile(w.reshape(n, d), (1, heads))
    same_head = (jnp.arange(n)[:, None] // d) == (jnp.arange(n)[None, :] // d)
    return jnp.where(same_head, tiled, 0.0)


def _route_plan(cnt):
    nt = cnt.shape[0]
    run_len = (cnt + (SUBLANES - 1)) // SUBLANES * SUBLANES
    region = jnp.sum(run_len, axis=0)
    region_blk = (region + FFN_BLOCK - 1) // FFN_BLOCK * FFN_BLOCK
    earlier_e = jnp.arange(N_EXPERTS)[None, :] < jnp.arange(N_EXPERTS)[:, None]
    region_start = jnp.sum(jnp.where(earlier_e, region_blk[None, :], 0), axis=1)
    region_end = region_start + region_blk
    earlier_t = jnp.arange(nt)[None, :] < jnp.arange(nt)[:, None]
    run_before = jnp.sum(jnp.where(earlier_t[:, :, None], run_len[None, :, :], 0), axis=1)
    run_dst = region_start[None, :] + run_before
    tail_dst = region_start + region
    tail_len = region_blk - region
    total_rows = jnp.sum(region_blk)
    nused = total_rows // FFN_BLOCK
    max_rows = nt * TOKEN_TILE * TOP_K + nt * N_EXPERTS * (SUBLANES - 1) + N_EXPERTS * (FFN_BLOCK - SUBLANES)
    nb = -(-max_rows // FFN_BLOCK)
    blk_start = jnp.arange(nb, dtype=I32) * FFN_BLOCK
    blk_row = jnp.minimum(blk_start, total_rows - 1)
    blk_e = jnp.sum((blk_row[:, None] >= region_end[None, :]).astype(I32), axis=1)
    blk_e = jnp.minimum(blk_e, N_EXPERTS - 1).astype(I32)
    later_used = jnp.logical_and(jnp.arange(N_EXPERTS)[None, :] > jnp.arange(N_EXPERTS)[:, None],
                                 region_blk[None, :] > 0)
    next_e = jnp.min(jnp.where(later_used, jnp.arange(N_EXPERTS)[None, :], N_EXPERTS), axis=1)
    next_e = jnp.where(next_e == N_EXPERTS, -1, next_e)
    tile_rows = jnp.sum(run_len, axis=1)
    edge = ((PLAN_PAD, PLAN_PAD), (0, 0))
    run_dst, run_len = jnp.pad(run_dst, edge), jnp.pad(run_len, edge)
    tile_rows = jnp.pad(tile_rows, PLAN_PAD)
    return dict(run_dst=run_dst.reshape(-1).astype(I32), run_len=run_len.reshape(-1).astype(I32),
                tile_rows=tile_rows.astype(I32), tail_dst=tail_dst.astype(I32),
                tail_len=tail_len.astype(I32), blk_e=blk_e, next_e=next_e.astype(I32),
                nused=nused.reshape(1).astype(I32)), nb * FFN_BLOCK


def kernel(x, c, w_ada, b_ada, norm_mix_pre, norm_mix_post, w_in, conv_w, conv_b, gate_a_w, gate_a_b, gate_x_w, gate_x_b, lru_lambda, pool_w, pool_b, pool_scale, w_out, norm_ffn_pre, norm_ffn_post, router_w, router_b, expert_w1, expert_b1, expert_w2, expert_b2):
    depth = w_ada.shape[0]
    batch = x.shape[0]
    for l in range(depth):
        mod3 = _ada_call(c, w_ada[l], b_ada[l]).reshape(batch, 6, D_MODEL)

        nrm = jnp.stack([norm_mix_pre[l], norm_mix_post[l], norm_ffn_pre[l]])
        lruv = jnp.stack([conv_b[l], gate_a_b[l], gate_x_b[l], lru_lambda[l]])
        wg = jnp.concatenate([_block_diag(gate_a_w[l]), _block_diag(gate_x_w[l])], axis=1).astype(BF16)
        poolv = jnp.stack([pool_b[l], pool_scale[l]])
        wp = _block_diag(pool_w[l]).astype(BF16)
        rw_pad = jnp.pad(router_w[l], ((0, 0), (0, LANES - N_EXPERTS)))
        rw_hi = rw_pad.astype(BF16)
        rw_lo = (rw_pad - rw_hi.astype(F32)).astype(BF16)
        rwh = jnp.concatenate([rw_hi, rw_lo], axis=1)
        rb = router_b[l].reshape(N_EXPERTS, 1)

        x1, h2, gate, pos, cnt = _mixer_call(
            x, mod3, nrm, w_in[l].astype(BF16), conv_w[l], lruv, wg, poolv, wp,
            w_out[l].astype(BF16), rwh, rb)

        plan, cap = _route_plan(cnt[:, :, :, 0].reshape(-1, N_EXPERTS))
        xs = _dispatch_call(plan["run_dst"], plan["run_len"], plan["tile_rows"], plan["tail_dst"],
                            plan["tail_len"], plan["nused"], h2.reshape(-1, D_MODEL), pos, cap)

        y = _ffn_call(plan["blk_e"], plan["nused"], plan["next_e"], xs, expert_w1[l],
                      expert_b1[l].reshape(N_EXPERTS, 1, 2 * D_EXPERT),
                      expert_w2[l], expert_b2[l].reshape(N_EXPERTS, 1, D_MODEL))

        x = _combine_call(plan["run_dst"], plan["run_len"], plan["tile_rows"], y, pos, gate, x1, mod3,
                          norm_ffn_post[l].reshape(1, D_MODEL))
    return x
```
